```python
import math
import jax, jax.numpy as jnp
from jax import lax
import numpy as np

D_MODEL = 1024
BATCH = 8
SEQ = 2048
DEPTH = 4
DEC_BATCH = 128
DEC_SEQ = 1
PAST_LEN = 8192
PAGE_SIZE = 128

HEAD_DIM = 64
N_HEADS = 8
N_KV_HEADS = 2
GROUP = N_HEADS // N_KV_HEADS
Q_WIDTH = N_HEADS * HEAD_DIM
KV_WIDTH = N_KV_HEADS * HEAD_DIM
WINDOW = 128
N_BUCKETS = 32
MAX_DISTANCE = 128
LRU_WIDTH = D_MODEL
LRU_HEADS = 8
LRU_BLOCK = LRU_WIDTH // LRU_HEADS
LRU_C = 8.0
CONV_WIDTH = 4
D_FF = -(-8 * D_MODEL // (3 * 256)) * 256
PLE_DIM = 256
IN_WIDTH = Q_WIDTH + 2 * KV_WIDTH + 2 * LRU_WIDTH + 2 * D_MODEL
EPS = 1e-6
NEG_INF = -1e30

kernel_name = 'hybrid_swa_rglru_step'


def rms_norm(x, g):
    xf = x.astype(jnp.float32)
    y = xf * lax.rsqrt(jnp.mean(xf * xf, axis=-1, keepdims=True) + EPS)
    return (y * g.astype(jnp.float32)).astype(x.dtype)


def t5_bucket(dist):
    n = jnp.maximum(dist, 0)
    max_exact = N_BUCKETS // 2
    nf = jnp.maximum(n, 1).astype(jnp.float32)
    large = max_exact + (jnp.log(nf / max_exact) / math.log(MAX_DISTANCE / max_exact)
                         * (N_BUCKETS - max_exact)).astype(jnp.int32)
    large = jnp.minimum(large, N_BUCKETS - 1)
    return jnp.where(n < max_exact, n, large)


def window_attention(q, k, v, dist, valid, t5_table, sinks):
    s = jnp.einsum('bnqkgd,bnskd->bnkgqs', q, k,
                   preferred_element_type=jnp.float32) * (HEAD_DIM ** -0.5)
    bias = jnp.take(t5_table, t5_bucket(dist), axis=0)
    bias = jnp.transpose(bias, (2, 0, 1)).reshape(N_KV_HEADS, GROUP, *dist.shape)
    s = jnp.where(valid, s + bias.astype(jnp.float32), NEG_INF)
    sink = jnp.broadcast_to(sinks.astype(jnp.float32).reshape(N_KV_HEADS, GROUP, 1, 1),
                            s.shape[:-1] + (1,))
    pr = jax.nn.softmax(jnp.concatenate([s, sink], axis=-1), axis=-1)[..., :-1]
    return jnp.einsum('bnkgqs,bnskd->bnqkgd', pr.astype(v.dtype), v)


def causal_conv(xb, buf, w, b):
    L = xb.shape[1]
    xp = jnp.concatenate([buf.astype(xb.dtype), xb], axis=1)
    y = b + sum(w[j] * xp[:, j:j + L] for j in range(CONV_WIDTH))
    return y, xp[:, L:]


def rg_lru(xb, h0, w_a, b_a, w_x, b_x, lam):
    B, L, _ = xb.shape
    f32 = jnp.float32
    xf = xb.astype(f32)
    xh = xf.reshape(B, L, LRU_HEADS, LRU_BLOCK)
    r = jax.nn.sigmoid(jnp.einsum('blhi,hij->blhj', xh, w_a.astype(f32)).reshape(B, L, LRU_WIDTH)
                       + b_a.astype(f32))
    ig = jax.nn.sigmoid(jnp.einsum('blhi,hij->blhj', xh, w_x.astype(f32)).reshape(B, L, LRU_WIDTH)
                        + b_x.astype(f32))
    log_a = -LRU_C * r * jax.nn.softplus(-lam.astype(f32))
    a = jnp.exp(log_a)
    b = jnp.sqrt(-jnp.expm1(2.0 * log_a)) * (ig * xf)
    b = b.at[:, 0].add(a[:, 0] * h0.astype(f32))

    def combine(lhs, rhs):
        a1, b1 = lhs
        a2, b2 = rhs
        return a1 * a2, a2 * b1 + b2

    _, h = lax.associative_scan(combine, (a, b), axis=1)
    return h.astype(xb.dtype), h[:, -1].astype(h0.dtype)


def decoder_layer(x, p, k_buf, v_buf, h0, conv_buf, is_prompt, t5_table, ln1, w_in, q_gain,
                  k_gain, sinks, w_o_attn, conv_w, conv_b, w_a, b_a, w_x, b_x, lam, w_o_lru,
                  w_out, ln2, w_gate, w_up, w_down, ln3, w_ple, w_ple_gate):
    B, L, _ = x.shape
    h = rms_norm(x, ln1)
    proj = h @ w_in
    splits = np.cumsum([Q_WIDTH, KV_WIDTH, KV_WIDTH, LRU_WIDTH, LRU_WIDTH, D_MODEL]).tolist()
    q, k, v, xr, xg, g_att, g_lru = jnp.split(proj, splits, axis=-1)

    q = rms_norm(q.reshape(B, L, N_KV_HEADS, GROUP, HEAD_DIM), q_gain)
    k = rms_norm(k.reshape(B, L, N_KV_HEADS, HEAD_DIM), k_gain)
    v = v.reshape(B, L, N_KV_HEADS, HEAD_DIM)
    if is_prompt:
        nb = L // WINDOW
        qb = q.reshape(B, nb, WINDOW, N_KV_HEADS, GROUP, HEAD_DIM)

        def band(t):
            tb = t.reshape(B, nb, WINDOW, N_KV_HEADS, HEAD_DIM)
            prev = jnp.concatenate([jnp.zeros_like(tb[:, :1]), tb[:, :-1]], axis=1)
            return jnp.concatenate([prev, tb], axis=2)

        kb, vb = band(k), band(v)
        qloc = WINDOW + jnp.arange(WINDOW)
        kloc = jnp.arange(2 * WINDOW)
        dist = qloc[:, None] - kloc[None, :]
        kabs = jnp.arange(nb)[:, None, None] * WINDOW - WINDOW + kloc[None, None, :]
        valid = (((dist >= 0) & (dist < WINDOW))[None] & (kabs >= 0))[:, None, None]
        o = window_attention(qb, kb, vb, dist, valid, t5_table, sinks)
        new_k, new_v = k[:, L - WINDOW:], v[:, L - WINDOW:]
    else:
        W = k_buf.shape[1]
        kc = jnp.concatenate([k_buf.astype(k.dtype), k], axis=1)
        vc = jnp.concatenate([v_buf.astype(v.dtype), v], axis=1)
        qpos = PAST_LEN + jnp.arange(L)
        kpos = jnp.concatenate([PAST_LEN - W + jnp.arange(W), qpos])
        dist = qpos[:, None] - kpos[None, :]
        valid = ((dist >= 0) & (dist < WINDOW))[None, None, None]
        o = window_attention(q[:, None], kc[:, None], vc[:, None], dist, valid, t5_table, sinks)
        new_k, new_v = kc[:, L:], vc[:, L:]
    o_att = o.reshape(B, L, Q_WIDTH) @ w_o_attn

    xc, new_conv = causal_conv(xr, conv_buf, conv_w, conv_b)
    hseq, new_h = rg_lru(xc, h0, w_a, b_a, w_x, b_x, lam)
    o_lru = (hseq * jax.nn.gelu(xg)) @ w_o_lru

    x = x + (jax.nn.sigmoid(g_att) * o_att + jax.nn.sigmoid(g_lru) * o_lru) @ w_out

    h2 = rms_norm(x, ln2)
    x = x + (jax.nn.silu(h2 @ w_gate) * (h2 @ w_up)) @ w_down

    x = x + jax.nn.sigmoid(rms_norm(x, ln3) @ w_ple_gate) * (p @ w_ple)
    return x, new_k, new_v, new_h, new_conv


def setup_inputs(seed: int = 0) -> dict:
    key = jax.random.key(seed)
    ks = jax.random.split(key, 32)
    f32 = jnp.float32
    win_buf = min(WINDOW, PAST_LEN)

    def nrm(k, shape, scale=1.0):
        return jax.random.normal(k, shape, f32) * scale

    u = jax.random.uniform(ks[20], (DEPTH, LRU_WIDTH), f32, 0.9, 0.999)
    sa = u ** (1.0 / LRU_C)
    lam = jnp.log(sa) - jnp.log1p(-sa)
    return {
        'x_prompt': nrm(ks[0], (BATCH, SEQ, D_MODEL)),
        'x_sample': nrm(ks[1], (DEC_BATCH, DEC_SEQ, D_MODEL)),
        'cache_k_win': nrm(ks[2], (DEPTH, DEC_BATCH, win_buf, N_KV_HEADS, HEAD_DIM)),
        'cache_v_win': nrm(ks[3], (DEPTH, DEC_BATCH, win_buf, N_KV_HEADS, HEAD_DIM)),
        'state_lru_h': nrm(ks[4], (DEPTH, DEC_BATCH, LRU_WIDTH), 0.5),
        'state_conv': nrm(ks[5], (DEPTH, DEC_BATCH, CONV_WIDTH - 1, LRU_WIDTH)),
        'p_prompt': nrm(ks[6], (DEPTH, BATCH, SEQ, PLE_DIM)),
        'p_sample': nrm(ks[7], (DEPTH, DEC_BATCH, DEC_SEQ, PLE_DIM)),
        't5_table': nrm(ks[8], (N_BUCKETS, N_HEADS), 0.5),
        'ln1': 1.0 + nrm(ks[9], (DEPTH, D_MODEL), 0.1),
        'w_in': nrm(ks[10], (DEPTH, D_MODEL, IN_WIDTH), D_MODEL ** -0.5),
        'q_gain': 1.0 + nrm(ks[11], (DEPTH, HEAD_DIM), 0.1),
        'k_gain': 1.0 + nrm(ks[12], (DEPTH, HEAD_DIM), 0.1),
        'sinks': nrm(ks[13], (DEPTH, N_HEADS), 1.0),
        'w_o_attn': nrm(ks[14], (DEPTH, Q_WIDTH, D_MODEL), Q_WIDTH ** -0.5),
        'conv_w': nrm(ks[15], (DEPTH, CONV_WIDTH, LRU_WIDTH), CONV_WIDTH ** -0.5),
        'conv_b': nrm(ks[16], (DEPTH, LRU_WIDTH), 0.01),
        'w_a': nrm(ks[17], (DEPTH, LRU_HEADS, LRU_BLOCK, LRU_BLOCK), LRU_BLOCK ** -0.5),
        'b_a': nrm(ks[18], (DEPTH, LRU_WIDTH), 0.01),
        'w_x': nrm(ks[19], (DEPTH, LRU_HEADS, LRU_BLOCK, LRU_BLOCK), LRU_BLOCK ** -0.5),
        'b_x': nrm(ks[21], (DEPTH, LRU_WIDTH), 0.01),
        'lam': lam,
        'w_o_lru': nrm(ks[22], (DEPTH, LRU_WIDTH, D_MODEL), LRU_WIDTH ** -0.5),
        'w_out': nrm(ks[23], (DEPTH, D_MODEL, D_MODEL), D_MODEL ** -0.5),
        'ln2': 1.0 + nrm(ks[24], (DEPTH, D_MODEL), 0.1),
        'w_gate': nrm(ks[25], (DEPTH, D_MODEL, D_FF), D_MODEL ** -0.5),
        'w_up': nrm(ks[26], (DEPTH, D_MODEL, D_FF), D_MODEL ** -0.5),
        'w_down': nrm(ks[27], (DEPTH, D_FF, D_MODEL), D_FF ** -0.5),
        'ln3': 1.0 + nrm(ks[28], (DEPTH, D_MODEL), 0.1),
        'w_ple': nrm(ks[29], (DEPTH, PLE_DIM, D_MODEL), PLE_DIM ** -0.5),
        'w_ple_gate': nrm(ks[30], (DEPTH, D_MODEL, D_MODEL), D_MODEL ** -0.5),
    }


def reference(x_prompt, x_sample, cache_k_win, cache_v_win, state_lru_h, state_conv,
              p_prompt, p_sample, t5_table, ln1, w_in, q_gain, k_gain, sinks, w_o_attn,
              conv_w, conv_b, w_a, b_a, w_x, b_x, lam, w_o_lru, w_out, ln2, w_gate, w_up,
              w_down, ln3, w_ple, w_ple_gate):
    yp, ys = x_prompt, x_sample
    bp = x_prompt.shape[0]
    h0_p = jnp.zeros((bp, LRU_WIDTH), x_prompt.dtype)
    conv0_p = jnp.zeros((bp, CONV_WIDTH - 1, LRU_WIDTH), x_prompt.dtype)
    kp_l, vp_l, hp_l, cp_l = [], [], [], []
    ks_l, vs_l, hs_l, cs_l = [], [], [], []
    for i in range(DEPTH):
        lw = (t5_table, ln1[i], w_in[i], q_gain[i], k_gain[i], sinks[i], w_o_attn[i],
              conv_w[i], conv_b[i], w_a[i], b_a[i], w_x[i], b_x[i], lam[i], w_o_lru[i],
              w_out[i], ln2[i], w_gate[i], w_up[i], w_down[i], ln3[i], w_ple[i], w_ple_gate[i])
        yp, kp, vp, hp, cp = decoder_layer(yp, p_prompt[i], None, None, h0_p, conv0_p,
                                           True, *lw)
        ys, kss, vss, hss, css = decoder_layer(ys, p_sample[i], cache_k_win[i], cache_v_win[i],
                                               state_lru_h[i], state_conv[i], False, *lw)
        kp_l.append(kp); vp_l.append(vp); hp_l.append(hp); cp_l.append(cp)
        ks_l.append(kss); vs_l.append(vss); hs_l.append(hss); cs_l.append(css)
    new_k_win_prompt = jnp.stack(kp_l)
    new_v_win_prompt = jnp.stack(vp_l)
    new_lru_h_prompt = jnp.stack(hp_l)
    new_conv_prompt = jnp.stack(cp_l)
    new_k_win_sample = jnp.stack(ks_l)
    new_v_win_sample = jnp.stack(vs_l)
    new_lru_h_sample = jnp.stack(hs_l)
    new_conv_sample = jnp.stack(cs_l)
    return (yp, ys, new_k_win_prompt, new_v_win_prompt, new_lru_h_prompt, new_conv_prompt,
            new_k_win_sample, new_v_win_sample, new_lru_h_sample, new_conv_sample)
```

```python
import functools
import math

import numpy as np
import jax
import jax.numpy as jnp
from jax import lax
from jax.experimental import pallas as pl
from jax.experimental.pallas import tpu as pltpu

F32 = jnp.float32
BF16 = jnp.bfloat16

D_MODEL = 1024
HEAD_DIM = 64
N_HEADS = 8
N_KV_HEADS = 2
GROUP = N_HEADS // N_KV_HEADS
Q_WIDTH = N_HEADS * HEAD_DIM
KV_WIDTH = N_KV_HEADS * HEAD_DIM
WINDOW = 128
N_BUCKETS = 32
MAX_DISTANCE = 128
LRU_WIDTH = D_MODEL
LRU_HEADS = 8
LRU_BLOCK = LRU_WIDTH // LRU_HEADS
LRU_C = 8.0
CONV_WIDTH = 4
D_FF = 2816
PLE_DIM = 256
PAST_LEN = 8192
EPS = 1e-6
NEG_INF = -1e30

SUBLANES = 8
LANES = 128
VMEM_LIMIT_BYTES = 56 * 1024 * 1024

Q0 = 0
K0 = Q0 + Q_WIDTH
V0 = K0 + KV_WIDTH
XR0 = V0 + KV_WIDTH
XG0 = XR0 + LRU_WIDTH
GA0 = XG0 + LRU_WIDTH
GL0 = GA0 + D_MODEL
IN_WIDTH = GL0 + D_MODEL

R_LN1, R_CW0, R_CB, R_BA, R_BX, R_LAM, R_QG, R_KG, R_LN2, R_LN3 = 0, 1, 5, 6, 7, 8, 9, 10, 11, 12
VEC_ROWS = 16

PROMPT_TT = 64
FFN_TM = 512
SAMPLE_BB = 8
SLAB = 2 * LANES


def _dot(a, b):
    return jnp.dot(a, b, preferred_element_type=F32)


def _dot_nt(a, b):
    return lax.dot_general(a, b, (((1,), (1,)), ((), ())), preferred_element_type=F32)


def _rms(x, g):
    ms = jnp.mean(x * x, axis=-1, keepdims=True)
    return x * lax.rsqrt(ms + EPS) * g


def _seg_rms(x, seg, g):
    x2 = x * x
    hi = x2.astype(BF16)
    lo = (x2 - hi.astype(F32)).astype(BF16)
    ssq = _dot(hi, seg) + _dot(lo, seg)
    return x * lax.rsqrt(ssq * (1.0 / HEAD_DIM) + EPS) * g


def _softplus(z):
    return jnp.maximum(z, 0.0) + jnp.log1p(jnp.exp(-jnp.abs(z)))


def _gather_bias(bucket, t5_ref, head):
    acc = jnp.full(bucket.shape, NEG_INF, F32)
    for bkt in range(N_BUCKETS):
        acc = jnp.where(bucket == bkt, t5_ref[bkt * N_HEADS + head], acc)
    return acc


def _lru_gates(xch, wax, b_a, b_x, c_row):
    gts = _dot(xch.astype(BF16), wax)
    r = jax.nn.sigmoid(gts[:, :LRU_BLOCK] + b_a)
    ig = jax.nn.sigmoid(gts[:, LRU_BLOCK:] + b_x)
    log_a = c_row * r
    a = jnp.exp(log_a)
    return a, jnp.sqrt(1.0 - a * a) * (ig * xch)


def _mixer_prompt_kernel(t5_ref, sinks_ref, x_ref, vec_ref, bucket_ref, seg_ref, win_ref,
                         woa_ref, wax_ref, wol_ref, wout_ref,
                         x1_ref, nk_ref, nv_ref, nh_ref, nc_ref,
                         bias_scr, kband, vband, cs, a_scr, b_scr, hst, y_scr, o_scr,
                         *, nb, tt):
    tm = nb * tt
    pitch = tt + SUBLANES
    kw = WINDOW + tt
    t = pl.program_id(0)

    @pl.when(t == 0)
    def _init():
        bucket = bucket_ref[...]
        for hh in range(N_HEADS):
            bias_scr[hh] = _gather_bias(bucket, t5_ref, hh)
        kband[:, 0:WINDOW, :] = jnp.zeros((nb, WINDOW, KV_WIDTH), F32)
        vband[:, 0:WINDOW, :] = jnp.zeros((nb, WINDOW, KV_WIDTH), F32)
        cs[:, 0:SUBLANES, :] = jnp.zeros((nb, SUBLANES, LRU_WIDTH), F32)
        hst[...] = jnp.zeros((nb, LRU_WIDTH), F32)

    x = x_ref[...].reshape(tm, D_MODEL)
    h = _rms(x, vec_ref[R_LN1:R_LN1 + 1, :]).astype(BF16)

    qkv = _dot(h, win_ref[:, Q0:XR0])
    seg = seg_ref[...]
    qn = _seg_rms(qkv[:, Q0:K0], seg, vec_ref[R_QG:R_QG + 1, 0:Q_WIDTH] * (HEAD_DIM ** -0.5))
    kn = _seg_rms(qkv[:, K0:V0], seg[0:KV_WIDTH, 0:KV_WIDTH], vec_ref[R_KG:R_KG + 1, 0:KV_WIDTH])
    kband[:, WINDOW:kw, :] = kn.reshape(nb, tt, KV_WIDTH)
    vband[:, WINDOW:kw, :] = qkv[:, V0:XR0].reshape(nb, tt, KV_WIDTH)
    nk_ref[...] = kband[:, tt:kw, :]
    nv_ref[...] = vband[:, tt:kw, :]

    col = lax.broadcasted_iota(jnp.int32, (1, kw), 1)
    colmask = jnp.where(col >= WINDOW - t * tt, 0.0, NEG_INF)
    lane = lax.broadcasted_iota(jnp.int32, (1, LANES), 1)
    low = lane < HEAD_DIM
    bias_t = [bias_scr[hh, 0:tt, 0:kw] + colmask for hh in range(N_HEADS)]

    for b in range(nb):
        qb = qn[b * tt:(b + 1) * tt]
        kb = kband[b].astype(BF16)
        vb = vband[b].astype(BF16)
        o_kv = []
        for kh in range(N_KV_HEADS):
            keep = low if kh == 0 else jnp.logical_not(low)
            q4 = jnp.concatenate(
                [jnp.where(keep, qb[:, g * LANES:(g + 1) * LANES], 0.0) for g in range(GROUP)],
                axis=0).astype(BF16)
            s = _dot_nt(q4, kb)
            ps = []
            for g in range(GROUP):
                hh = kh * GROUP + g
                sg = s[g * tt:(g + 1) * tt] + bias_t[hh]
                sink = sinks_ref[hh]
                m = jnp.maximum(jnp.max(sg, axis=-1, keepdims=True), sink)
                e = jnp.exp(sg - m)
                den = jnp.sum(e, axis=-1, keepdims=True) + jnp.exp(sink - m)
                ps.append((e * (1.0 / den)).astype(BF16))
            o_kv.append(_dot(jnp.concatenate(ps, axis=0), vb))
        for g in range(GROUP):
            og = jnp.where(low, o_kv[0][g * tt:(g + 1) * tt], o_kv[1][g * tt:(g + 1) * tt])
            o_scr[b * tt:(b + 1) * tt, g * LANES:(g + 1) * LANES] = og.astype(BF16)

    kband[:, 0:WINDOW, :] = kband[:, tt:kw, :]
    vband[:, 0:WINDOW, :] = vband[:, tt:kw, :]

    m_att = jax.nn.sigmoid(_dot(h, win_ref[:, GA0:GL0])) * _dot(o_scr[...], woa_ref[...])

    c_row = -LRU_C * _softplus(-vec_ref[R_LAM:R_LAM + 1, :])
    for sp in range(LRU_WIDTH // SLAB):
        cols = slice(sp * SLAB, (sp + 1) * SLAB)
        xr = _dot(h, win_ref[:, XR0 + sp * SLAB:XR0 + (sp + 1) * SLAB])
        cs[:, SUBLANES:SUBLANES + tt, cols] = xr.reshape(nb, tt, SLAB)
        xc = vec_ref[R_CB:R_CB + 1, cols]
        for j in range(CONV_WIDTH):
            off = SUBLANES - (CONV_WIDTH - 1) + j
            xc = xc + vec_ref[R_CW0 + j:R_CW0 + j + 1, cols] * cs[:, off:off + tt, cols]
        tail = cs[:, tt:tt + SUBLANES, cols]
        nc_ref[:, :, cols] = tail
        cs[:, 0:SUBLANES, cols] = tail
        xc = xc.reshape(tm, SLAB)
        for i in range(SLAB // LRU_BLOCK):
            hd = sp * (SLAB // LRU_BLOCK) + i
            lc = slice(hd * LRU_BLOCK, (hd + 1) * LRU_BLOCK)
            a, bb = _lru_gates(xc[:, i * LRU_BLOCK:(i + 1) * LRU_BLOCK], wax_ref[hd],
                               vec_ref[R_BA:R_BA + 1, lc], vec_ref[R_BX:R_BX + 1, lc], c_row[:, lc])
            for b in range(nb):
                a_scr[hd, b * pitch:b * pitch + tt, :] = a[b * tt:(b + 1) * tt]
                b_scr[hd, b * pitch:b * pitch + tt, :] = bb[b * tt:(b + 1) * tt]

    def scan_step(ts, hs):
        out = []
        for hd in range(LRU_HEADS):
            rows = pl.ds(ts, nb, stride=pitch)
            hn = a_scr[hd, rows, :] * hs[hd] + b_scr[hd, rows, :]
            b_scr[hd, rows, :] = hn
            out.append(hn)
        return tuple(out)

    hs0 = tuple(hst[:, hd * LRU_BLOCK:(hd + 1) * LRU_BLOCK] for hd in range(LRU_HEADS))
    hs = lax.fori_loop(0, tt, scan_step, hs0, unroll=8)
    hfin = jnp.concatenate(hs, axis=1)
    hst[...] = hfin
    nh_ref[...] = hfin

    for sp in range(LRU_WIDTH // SLAB):
        cols = slice(sp * SLAB, (sp + 1) * SLAB)
        ge = jax.nn.gelu(_dot(h, win_ref[:, XG0 + sp * SLAB:XG0 + (sp + 1) * SLAB]))
        hseq = jnp.concatenate(
            [jnp.concatenate([b_scr[sp * (SLAB // LRU_BLOCK) + i, b * pitch:b * pitch + tt, :]
                              for b in range(nb)], axis=0)
             for i in range(SLAB // LRU_BLOCK)], axis=1)
        y_scr[:, cols] = (hseq * ge).astype(BF16)

    m_all = m_att + jax.nn.sigmoid(_dot(h, win_ref[:, GL0:IN_WIDTH])) * _dot(y_scr[...], wol_ref[...])
    x1 = x + _dot(m_all.astype(BF16), wout_ref[...])
    x1_ref[...] = x1.reshape(nb, tt, D_MODEL)


def _const_spec(shape, index):
    return pl.BlockSpec(shape, index, pipeline_mode=pl.Buffered(1))


def _smem_spec():
    return pl.BlockSpec(memory_space=pltpu.SMEM)


def _mixer_prompt(layer, x, prm):
    nb, seq, _ = x.shape
    tt = PROMPT_TT
    assert nb == SUBLANES and seq % tt == 0 and WINDOW % tt == 0
    tm = nb * tt
    pitch = tt + SUBLANES
    kw = WINDOW + tt
    wl = lambda *z: lambda t: (layer,) + z
    kern = functools.partial(_mixer_prompt_kernel, nb=nb, tt=tt)
    return pl.pallas_call(
        kern,
        grid=(seq // tt,),
        in_specs=[
            _smem_spec(), _smem_spec(),
            pl.BlockSpec((nb, tt, D_MODEL), lambda t: (0, t, 0)),
            _const_spec((None, VEC_ROWS, D_MODEL), wl(0, 0)),
            _const_spec((WINDOW, 2 * WINDOW), lambda t: (0, 0)),
            _const_spec((Q_WIDTH, Q_WIDTH), lambda t: (0, 0)),
            _const_spec((None, D_MODEL, IN_WIDTH), wl(0, 0)),
            _const_spec((None, Q_WIDTH, D_MODEL), wl(0, 0)),
            _const_spec((None, LRU_HEADS, LRU_BLOCK, 2 * LRU_BLOCK), wl(0, 0, 0)),
            _const_spec((None, LRU_WIDTH, D_MODEL), wl(0, 0)),
            _const_spec((None, D_MODEL, D_MODEL), wl(0, 0)),
        ],
        out_specs=[
            pl.BlockSpec((nb, tt, D_MODEL), lambda t: (0, t, 0)),
            pl.BlockSpec((nb, WINDOW, KV_WIDTH), lambda t: (0, 0, 0)),
            pl.BlockSpec((nb, WINDOW, KV_WIDTH), lambda t: (0, 0, 0)),
            pl.BlockSpec((nb, LRU_WIDTH), lambda t: (0, 0)),
            pl.BlockSpec((nb, SUBLANES, LRU_WIDTH), lambda t: (0, 0, 0)),
        ],
        out_shape=[
            jax.ShapeDtypeStruct((nb, seq, D_MODEL), F32),
            jax.ShapeDtypeStruct((nb, WINDOW, KV_WIDTH), F32),
            jax.ShapeDtypeStruct((nb, WINDOW, KV_WIDTH), F32),
            jax.ShapeDtypeStruct((nb, LRU_WIDTH), F32),
            jax.ShapeDtypeStruct((nb, SUBLANES, LRU_WIDTH), F32),
        ],
        scratch_shapes=[
            pltpu.VMEM((N_HEADS, WINDOW, 2 * WINDOW), F32),
            pltpu.VMEM((nb, kw, KV_WIDTH), F32),
            pltpu.VMEM((nb, kw, KV_WIDTH), F32),
            pltpu.VMEM((nb, tt + SUBLANES, LRU_WIDTH), F32),
            pltpu.VMEM((LRU_HEADS, nb * pitch, LRU_BLOCK), F32),
            pltpu.VMEM((LRU_HEADS, nb * pitch, LRU_BLOCK), F32),
            pltpu.VMEM((nb, LRU_WIDTH), F32),
            pltpu.VMEM((tm, LRU_WIDTH), BF16),
            pltpu.VMEM((tm, Q_WIDTH), BF16),
        ],
        compiler_params=pltpu.CompilerParams(
            dimension_semantics=("arbitrary",), vmem_limit_bytes=VMEM_LIMIT_BYTES),
        name=f"mixer_prompt_l{layer}",
    )(prm["t5"], prm["sinks"][layer], x, prm["vecs"], prm["bucket_p"], prm["seg"], prm["w_in"],
      prm["w_o_attn"], prm["w_ax"], prm["w_o_lru"], prm["w_out"])


def _mixer_sample_kernel(t5_ref, sinks_ref, x_ref, vec_ref, bucket_ref, seg_ref, win_ref,
                         woa_ref, wax_ref, wol_ref, wout_ref, ck_ref, cv_ref, h0_ref, sc_ref,
                         x1_ref, kn_ref, vn_ref, nh_ref, xr_ref,
                         h_scr, q_scr, o_scr, mb_scr,
                         *, nseq, bb):
    i = pl.program_id(0)
    nkeys = bb * WINDOW
    nrow = N_HEADS * bb

    @pl.when(i == 0)
    def _project():
        x = x_ref[...]
        h = _rms(x, vec_ref[R_LN1:R_LN1 + 1, :]).astype(BF16)
        h_scr[...] = h
        qkv = _dot(h, win_ref[:, Q0:XR0])
        seg = seg_ref[...]
        q_scr[...] = _seg_rms(qkv[:, Q0:K0], seg,
                              vec_ref[R_QG:R_QG + 1, 0:Q_WIDTH] * (HEAD_DIM ** -0.5))
        kn_ref[...] = _seg_rms(qkv[:, K0:V0], seg[0:KV_WIDTH, 0:KV_WIDTH],
                               vec_ref[R_KG:R_KG + 1, 0:KV_WIDTH])
        vn_ref[...] = qkv[:, V0:XR0]
        bucket = bucket_ref[...]
        rowb = lax.broadcasted_iota(jnp.int32, (bb, nkeys), 0)
        colb = lax.broadcasted_iota(jnp.int32, (bb, nkeys), 1) // WINDOW
        for hh in range(N_HEADS):
            brow = _gather_bias(bucket, t5_ref, hh)
            brow = jnp.concatenate([brow] * bb, axis=1)
            mb_scr[hh * bb:(hh + 1) * bb, :] = jnp.where(rowb == colb, brow, NEG_INF)

    rows = pl.ds(pl.multiple_of(i * bb, bb), bb)
    lane = lax.broadcasted_iota(jnp.int32, (1, LANES), 1)
    low = lane < HEAD_DIM
    qblk = q_scr[rows, :]
    qz = []
    for kh in range(N_KV_HEADS):
        keep = low if kh == 0 else jnp.logical_not(low)
        for g in range(GROUP):
            qz.append(jnp.where(keep, qblk[:, g * LANES:(g + 1) * LANES], 0.0))
    qz = jnp.concatenate(qz, axis=0).astype(BF16)
    kc = ck_ref[...].reshape(nkeys, KV_WIDTH).astype(BF16)
    vc = cv_ref[...].reshape(nkeys, KV_WIDTH).astype(BF16)
    s = _dot_nt(qz, kc) + mb_scr[...]
    knew = jnp.concatenate([kn_ref[rows, :].astype(BF16).astype(F32)] * N_HEADS, axis=0)
    vnew = jnp.concatenate([vn_ref[rows, :].astype(BF16).astype(F32)] * N_HEADS, axis=0)
    self_bias = jnp.concatenate(
        [jnp.full((bb, 1), t5_ref[hh], F32) for hh in range(N_HEADS)], axis=0)
    sink = jnp.concatenate(
        [jnp.full((bb, 1), sinks_ref[hh], F32) for hh in range(N_HEADS)], axis=0)
    s_self = jnp.sum(qz.astype(F32) * knew, axis=-1, keepdims=True) + self_bias
    m = jnp.maximum(jnp.maximum(jnp.max(s, axis=-1, keepdims=True), s_self), sink)
    e = jnp.exp(s - m)
    e_self = jnp.exp(s_self - m)
    den = jnp.sum(e, axis=-1, keepdims=True) + e_self + jnp.exp(sink - m)
    inv = 1.0 / den
    o = (_dot((e * inv).astype(BF16), vc)
         + (e_self * inv).astype(BF16).astype(F32) * vnew)
    half = GROUP * bb
    for g in range(GROUP):
        og = jnp.where(low, o[g * bb:(g + 1) * bb], o[half + g * bb:half + (g + 1) * bb])
        o_scr[rows, g * LANES:(g + 1) * LANES] = og

    @pl.when(i == pl.num_programs(0) - 1)
    def _finish():
        x = x_ref[...]
        h = h_scr[...]
        m_att = jax.nn.sigmoid(_dot(h, win_ref[:, GA0:GL0])) * _dot(o_scr[...].astype(BF16), woa_ref[...])
        xr =_dot(h, win_ref[:, XR0:XG0])
        xr_ref[...] = xr
        xc = vec_ref[R_CB:R_CB + 1, :] + vec_ref[R_CW0 + CONV_WIDTH - 1:R_CW0 + CONV_WIDTH, :] * xr
        for j in range(CONV_WIDTH - 1):
            xc = xc + vec_ref[R_CW0 + j:R_CW0 + j + 1, :] * sc_ref[j]
        c_row = -LRU_C * _softplus(-vec_ref[R_LAM:R_LAM + 1, :])
        hn = []
        for hd in range(LRU_HEADS):
            lc = slice(hd * LRU_BLOCK, (hd + 1) * LRU_BLOCK)
            a, bb_ = _lru_gates(xc[:, lc], wax_ref[hd], vec_ref[R_BA:R_BA + 1, lc],
                                vec_ref[R_BX:R_BX + 1, lc], c_row[:, lc])
            hn.append(a * h0_ref[:, lc] + bb_)
        hn = jnp.concatenate(hn, axis=1)
        nh_ref[...] = hn
        y = (hn * jax.nn.gelu(_dot(h, win_ref[:, XG0:GA0]))).astype(BF16)
        m_all = m_att + jax.nn.sigmoid(_dot(h, win_ref[:, GL0:IN_WIDTH])) * _dot(y, wol_ref[...])
        x1_ref[...] = x + _dot(m_all.astype(BF16), wout_ref[...])


def _mixer_sample(layer, x, ck, cv, h0, sc, prm):
    nseq = x.shape[0]
    bb = SAMPLE_BB
    assert nseq % bb == 0
    wl = lambda *z: lambda i: (layer,) + z
    full2 = lambda i: (0, 0)
    kern = functools.partial(_mixer_sample_kernel, nseq=nseq, bb=bb)
    return pl.pallas_call(
        kern,
        grid=(nseq // bb,),
        in_specs=[
            _smem_spec(), _smem_spec(),
            _const_spec((nseq, D_MODEL), full2),
            _const_spec((None, VEC_ROWS, D_MODEL), wl(0, 0)),
            _const_spec((1, WINDOW), full2),
            _const_spec((Q_WIDTH, Q_WIDTH), full2),
            _const_spec((None, D_MODEL, IN_WIDTH), wl(0, 0)),
            _const_spec((None, Q_WIDTH, D_MODEL), wl(0, 0)),
            _const_spec((None, LRU_HEADS, LRU_BLOCK, 2 * LRU_BLOCK), wl(0, 0, 0)),
            _const_spec((None, LRU_WIDTH, D_MODEL), wl(0, 0)),
            _const_spec((None, D_MODEL, D_MODEL), wl(0, 0)),
            pl.BlockSpec((None, bb, WINDOW, KV_WIDTH), lambda i: (layer, i, 0, 0)),
            pl.BlockSpec((None, bb, WINDOW, KV_WIDTH), lambda i: (layer, i, 0, 0)),
            _const_spec((None, nseq, LRU_WIDTH), wl(0, 0)),
            _const_spec((None, CONV_WIDTH - 1, nseq, LRU_WIDTH), wl(0, 0, 0)),
        ],
        out_specs=[
            pl.BlockSpec((nseq, D_MODEL), full2),
            pl.BlockSpec((nseq, KV_WIDTH), full2),
            pl.BlockSpec((nseq, KV_WIDTH), full2),
            pl.BlockSpec((nseq, LRU_WIDTH), full2),
            pl.BlockSpec((nseq, LRU_WIDTH), full2),
        ],
        out_shape=[
            jax.ShapeDtypeStruct((nseq, D_MODEL), F32),
            jax.ShapeDtypeStruct((nseq, KV_WIDTH), F32),
            jax.ShapeDtypeStruct((nseq, KV_WIDTH), F32),
            jax.ShapeDtypeStruct((nseq, LRU_WIDTH), F32),
            jax.ShapeDtypeStruct((nseq, LRU_WIDTH), F32),
        ],
        scratch_shapes=[
            pltpu.VMEM((nseq, D_MODEL), BF16),
            pltpu.VMEM((nseq, Q_WIDTH), F32),
            pltpu.VMEM((nseq, Q_WIDTH), F32),
            pltpu.VMEM((N_HEADS * bb, bb * WINDOW), F32),
        ],
        compiler_params=pltpu.CompilerParams(
            dimension_semantics=("arbitrary",), vmem_limit_bytes=VMEM_LIMIT_BYTES),
        name=f"mixer_sample_l{layer}",
    )(prm["t5"], prm["sinks"][layer], x, prm["vecs"], prm["bucket_s"], prm["seg"], prm["w_in"],
      prm["w_o_attn"], prm["w_ax"], prm["w_o_lru"], prm["w_out"], ck, cv, h0, sc)


FF_CHUNKS = ((0, 1024), (1024, 2048), (2048, D_FF))


def _ffn_kernel(x_ref, p_ref, vec_ref, wg_ref, wu_ref, wd_ref, wpg_ref, wp_ref, o_ref, act_scr):
    x = x_ref[...]
    h2 = _rms(x, vec_ref[R_LN2:R_LN2 + 1, :]).astype(BF16)
    for lo, hi in FF_CHUNKS:
        act = jax.nn.silu(_dot(h2, wg_ref[:, lo:hi])) * _dot(h2, wu_ref[:, lo:hi])
        act_scr[:, lo:hi] = act.astype(BF16)
    x = x + _dot(act_scr[...], wd_ref[...])
    h3 = _rms(x, vec_ref[R_LN3:R_LN3 + 1, :]).astype(BF16)
    gate = jax.nn.sigmoid(_dot(h3, wpg_ref[...]))
    o_ref[...] = x + gate * _dot(p_ref[...].astype(BF16), wp_ref[...])


def _ffn(layer, x, p, prm, tag):
    rows = x.shape[0]
    tm = min(FFN_TM, rows)
    assert rows % tm == 0
    wl = lambda *z: lambda r: (layer,) + z
    return pl.pallas_call(
        _ffn_kernel,
        grid=(rows // tm,),
        in_specs=[
            pl.BlockSpec((tm, D_MODEL), lambda r: (r, 0)),
            pl.BlockSpec((None, tm, PLE_DIM), lambda r: (layer, r, 0)),
            _const_spec((None, VEC_ROWS, D_MODEL), wl(0, 0)),
            _const_spec((None, D_MODEL, D_FF), wl(0, 0)),
            _const_spec((None, D_MODEL, D_FF), wl(0, 0)),
            _const_spec((None, D_FF, D_MODEL), wl(0, 0)),
            _const_spec((None, D_MODEL, D_MODEL), wl(0, 0)),
            _const_spec((None, PLE_DIM, D_MODEL), wl(0, 0)),
        ],
        out_specs=pl.BlockSpec((tm, D_MODEL), lambda r: (r, 0)),
        out_shape=jax.ShapeDtypeStruct((rows, D_MODEL), F32),
        scratch_shapes=[pltpu.VMEM((tm, D_FF), BF16)],
        compiler_params=pltpu.CompilerParams(
            dimension_semantics=("arbitrary",), vmem_limit_bytes=VMEM_LIMIT_BYTES),
        name=f"ffn_{tag}_l{layer}",
    )(x, p, prm["vecs"], prm["w_gate"], prm["w_up"], prm["w_down"], prm["w_ple_gate"], prm["w_ple"])


def _t5_bucket(dist):
    n = jnp.maximum(dist, 0)
    max_exact = N_BUCKETS // 2
    nf = jnp.maximum(n, 1).astype(F32)
    large = max_exact + (jnp.log(nf / max_exact) / math.log(MAX_DISTANCE / max_exact)
                         * (N_BUCKETS - max_exact)).astype(jnp.int32)
    large = jnp.minimum(large, N_BUCKETS - 1)
    return jnp.where(n < max_exact, n, large)


def _head_perm():
    perm = np.empty((Q_WIDTH,), np.int32)
    for g in range(GROUP):
        for kh in range(N_KV_HEADS):
            for d in range(HEAD_DIM):
                perm[g * LANES + kh * HEAD_DIM + d] = (kh * GROUP + g) * HEAD_DIM + d
    return perm


def _prepare(t5_table, ln1, w_in, q_gain, k_gain, sinks, w_o_attn, conv_w, conv_b, w_a, b_a,
             w_x, b_x, lam, w_o_lru, w_out, ln2, w_gate, w_up, w_down, ln3, w_ple, w_ple_gate):
    depth = w_in.shape[0]
    perm = _head_perm()
    w_in_p = jnp.concatenate([w_in[:, :, perm], w_in[:, :, Q_WIDTH:]], axis=2).astype(BF16)
    vecs = jnp.zeros((depth, VEC_ROWS, D_MODEL), F32)
    vecs = vecs.at[:, R_LN1].set(ln1)
    vecs = vecs.at[:, R_CW0:R_CW0 + CONV_WIDTH].set(conv_w)
    vecs = vecs.at[:, R_CB].set(conv_b)
    vecs = vecs.at[:, R_BA].set(b_a)
    vecs = vecs.at[:, R_BX].set(b_x)
    vecs = vecs.at[:, R_LAM].set(lam)
    vecs = vecs.at[:, R_QG, 0:Q_WIDTH].set(jnp.tile(q_gain, (1, N_HEADS)))
    vecs = vecs.at[:, R_KG, 0:KV_WIDTH].set(jnp.tile(k_gain, (1, N_KV_HEADS)))
    vecs = vecs.at[:, R_LN2].set(ln2)
    vecs = vecs.at[:, R_LN3].set(ln3)
    head_id = np.arange(Q_WIDTH) // HEAD_DIM
    seg = jnp.asarray(head_id[:, None] == head_id[None, :], BF16)
    dist_p = (WINDOW + jnp.arange(WINDOW))[:, None] - jnp.arange(2 * WINDOW)[None, :]
    bucket_p = jnp.where((dist_p >= 0) & (dist_p < WINDOW), _t5_bucket(dist_p), -1).astype(jnp.int32)
    dist_s = (WINDOW - jnp.arange(WINDOW))[None, :]
    bucket_s = jnp.where((dist_s >= 0) & (dist_s < WINDOW), _t5_bucket(dist_s), -1).astype(jnp.int32)
    return {
        "t5": t5_table.reshape(-1),
        "sinks": sinks,
        "vecs": vecs,
        "seg": seg,
        "bucket_p": bucket_p,
        "bucket_s": bucket_s,
        "w_in": w_in_p,
        "w_o_attn": w_o_attn[:, perm, :].astype(BF16),
        "w_ax": jnp.concatenate([w_a, w_x], axis=-1).astype(BF16),
        "w_o_lru": w_o_lru.astype(BF16),
        "w_out": w_out.astype(BF16),
        "w_gate": w_gate.astype(BF16),
        "w_up": w_up.astype(BF16),
        "w_down": w_down.astype(BF16),
        "w_ple_gate": w_ple_gate.astype(BF16),
        "w_ple": w_ple.astype(BF16),
    }


def kernel(x_prompt, x_sample, cache_k_win, cache_v_win, state_lru_h, state_conv, p_prompt,
           p_sample, t5_table, ln1, w_in, q_gain, k_gain, sinks, w_o_attn, conv_w, conv_b, w_a,
           b_a, w_x, b_x, lam, w_o_lru, w_out, ln2, w_gate, w_up, w_down, ln3, w_ple, w_ple_gate):
    depth = w_in.shape[0]
    nb, seq, _ = x_prompt.shape
    nseq = x_sample.shape[0]
    assert x_sample.shape[1] == 1 and cache_k_win.shape[2] == WINDOW
    prm = _prepare(t5_table, ln1, w_in, q_gain, k_gain, sinks, w_o_attn, conv_w, conv_b, w_a, b_a,
                   w_x, b_x, lam, w_o_lru, w_out, ln2, w_gate, w_up, w_down, ln3, w_ple, w_ple_gate)
    ck = cache_k_win.reshape(depth, nseq, WINDOW, KV_WIDTH)
    cv = cache_v_win.reshape(depth, nseq, WINDOW, KV_WIDTH)
    sc = jnp.transpose(state_conv, (0, 2, 1, 3))
    pp = p_prompt.reshape(depth, nb * seq, PLE_DIM)
    ps = p_sample.reshape(depth, nseq, PLE_DIM)

    yp = x_prompt
    ys = x_sample.reshape(nseq, D_MODEL)
    outs = [[] for _ in range(8)]
    for layer in range(depth):
        x1, nk, nv, nh, nc = _mixer_prompt(layer, yp, prm)
        yp = _ffn(layer, x1.reshape(nb * seq, D_MODEL), pp, prm, "prompt").reshape(nb, seq, D_MODEL)
        outs[0].append(nk.reshape(nb, WINDOW, N_KV_HEADS, HEAD_DIM))
        outs[1].append(nv.reshape(nb, WINDOW, N_KV_HEADS, HEAD_DIM))
        outs[2].append(nh)
        outs[3].append(nc[:, SUBLANES - (CONV_WIDTH - 1):, :])

        x1s, kn, vn, nhs, xr = _mixer_sample(layer, ys, ck, cv, state_lru_h, sc, prm)
        ys = _ffn(layer, x1s, ps, prm, "sample")
        outs[4].append(jnp.concatenate([ck[layer, :, 1:], kn[:, None, :]], axis=1)
                       .reshape(nseq, WINDOW, N_KV_HEADS, HEAD_DIM))
        outs[5].append(jnp.concatenate([cv[layer, :, 1:], vn[:, None, :]], axis=1)
                       .reshape(nseq, WINDOW, N_KV_HEADS, HEAD_DIM))
        outs[6].append(nhs)
        outs[7].append(jnp.concatenate([state_conv[layer, :, 1:], xr[:, None, :]], axis=1))
    return (yp, ys.reshape(nseq, 1, D_MODEL)) + tuple(jnp.stack(o) for o in outs)
```

```python
import functools
import math

import numpy as np
import jax
import jax.numpy as jnp
from jax import lax
from jax.experimental import pallas as pl
from jax.experimental.pallas import tpu as pltpu

F32 = jnp.float32
BF16 = jnp.bfloat16

D_MODEL = 1024
HEAD_DIM = 64
N_HEADS = 8
N_KV_HEADS = 2
GROUP = N_HEADS // N_KV_HEADS
Q_WIDTH = N_HEADS * HEAD_DIM
KV_WIDTH = N_KV_HEADS * HEAD_DIM
WINDOW = 128
N_BUCKETS = 32
MAX_DISTANCE = 128
LRU_WIDTH = D_MODEL
LRU_HEADS = 8
LRU_BLOCK = LRU_WIDTH // LRU_HEADS
LRU_C = 8.0
CONV_WIDTH = 4
D_FF = 2816
PLE_DIM = 256
PAST_LEN = 8192
EPS = 1e-6
NEG_INF = -1e30
TINY = 1e-30

SUBLANES = 8
LANES = 128
VMEM_LIMIT_BYTES = 56 * 1024 * 1024

Q0 = 0
K0 = Q0 + Q_WIDTH
V0 = K0 + KV_WIDTH
QKV_WIDTH = V0 + KV_WIDTH
XR0 = 0
XG0 = XR0 + LRU_WIDTH
GA0 = XG0 + LRU_WIDTH
GL0 = GA0 + D_MODEL
REST_WIDTH = GL0 + D_MODEL

R_LN1, R_CW0, R_CB, R_BA, R_BX, R_LAM, R_QG, R_KG, R_LN2, R_LN3 = 0, 1, 5, 6, 7, 8, 9, 10, 11, 12
VEC_ROWS = 16

PROMPT_TT = 64
FFN_TM = 512
SAMPLE_BB = 8
SLAB = 2 * LANES


def _dot(a, b):
    return jnp.dot(a, b, preferred_element_type=F32)


def _dot_nt(a, b):
    return lax.dot_general(a, b, (((1,), (1,)), ((), ())), preferred_element_type=F32)


def _rms(x, g):
    ms = jnp.mean(x * x, axis=-1, keepdims=True)
    return x * lax.rsqrt(ms + EPS) * g


def _seg_rms(x, seg, g):
    x2 = x * x
    hi = x2.astype(BF16)
    lo = (x2 - hi.astype(F32)).astype(BF16)
    ssq = _dot(hi, seg) + _dot(lo, seg)
    return x * lax.rsqrt(ssq * (1.0 / HEAD_DIM) + EPS) * g


def _sigmoid(x):
    return 0.5 * jnp.tanh(0.5 * x) + 0.5


def _softplus(z):
    return jnp.maximum(z, 0.0) + jnp.log1p(jnp.exp(-jnp.abs(z)))


def _gather_bias(bucket, t5_ref, head):
    acc = jnp.full(bucket.shape, NEG_INF, F32)
    for bkt in range(N_BUCKETS):
        acc = jnp.where(bucket == bkt, t5_ref[bkt * N_HEADS + head], acc)
    return acc


def _lru_gates(xch, wax, b_a, b_x, c_row):
    gts = _dot(xch.astype(BF16), wax)
    r = _sigmoid(gts[:, :LRU_BLOCK] + b_a)
    ig = _sigmoid(gts[:, LRU_BLOCK:] + b_x)
    log_a = c_row * r
    a = jnp.exp(log_a)
    y = 1.0 - a * a
    return a, y * lax.rsqrt(jnp.maximum(y, TINY)) * (ig * xch)


def _mixer_prompt_kernel(t5_ref, sinks_ref, x_ref, vec_ref, bucket_ref, seg_ref, wqkv_ref, win_ref,
                         woa_ref, wax_ref, wol_ref, wout_ref,
                         x1_ref, nk_ref, nv_ref, nh_ref, nc_ref,
                         bias_scr, kband, vband, cs, a_scr, b_scr, hst, y_scr, o_scr,
                         *, nb, tt):
    tm = nb * tt
    pitch = tt + SUBLANES
    kw = WINDOW + tt
    t = pl.program_id(0)

    @pl.when(t == 0)
    def _init():
        bucket = bucket_ref[...]
        for hh in range(N_HEADS):
            bias_scr[hh] = _gather_bias(bucket, t5_ref, hh)
        kband[:, 0:WINDOW, :] = jnp.zeros((nb, WINDOW, KV_WIDTH), F32)
        vband[:, 0:WINDOW, :] = jnp.zeros((nb, WINDOW, KV_WIDTH), F32)
        cs[:, 0:SUBLANES, :] = jnp.zeros((nb, SUBLANES, LRU_WIDTH), F32)
        hst[...] = jnp.zeros((nb, LRU_WIDTH), F32)

    x = x_ref[...].reshape(tm, D_MODEL)
    h = _rms(x, vec_ref[R_LN1:R_LN1 + 1, :]).astype(BF16)

    qkv = _dot(h, wqkv_ref[...])
    seg = seg_ref[...]
    qn = _seg_rms(qkv[:, Q0:K0], seg, vec_ref[R_QG:R_QG + 1, 0:Q_WIDTH] * (HEAD_DIM ** -0.5))
    kn = _seg_rms(qkv[:, K0:V0], seg[0:KV_WIDTH, 0:KV_WIDTH], vec_ref[R_KG:R_KG + 1, 0:KV_WIDTH])
    kband[:, WINDOW:kw, :] = kn.reshape(nb, tt, KV_WIDTH)
    vband[:, WINDOW:kw, :] = qkv[:, V0:QKV_WIDTH].reshape(nb, tt, KV_WIDTH)
    nk_ref[...] = kband[:, tt:kw, :]
    nv_ref[...] = vband[:, tt:kw, :]

    col = lax.broadcasted_iota(jnp.int32, (1, kw), 1)
    colmask = jnp.where(col >= WINDOW - t * tt, 0.0, NEG_INF)
    lane = lax.broadcasted_iota(jnp.int32, (1, LANES), 1)
    low = lane < HEAD_DIM
    bias_t = [bias_scr[hh, 0:tt, 0:kw] + colmask for hh in range(N_HEADS)]

    for b in range(nb):
        qb = qn[b * tt:(b + 1) * tt]
        kb = kband[b].astype(BF16)
        vb = vband[b].astype(BF16)
        o_kv = []
        for kh in range(N_KV_HEADS):
            keep = low if kh == 0 else jnp.logical_not(low)
            q4 = jnp.concatenate(
                [jnp.where(keep, qb[:, g * LANES:(g + 1) * LANES], 0.0) for g in range(GROUP)],
                axis=0).astype(BF16)
            s = _dot_nt(q4, kb)
            ps = []
            for g in range(GROUP):
                hh = kh * GROUP + g
                sg = s[g * tt:(g + 1) * tt] + bias_t[hh]
                sink = sinks_ref[hh]
                m = jnp.maximum(jnp.max(sg, axis=-1, keepdims=True), sink)
                e = jnp.exp(sg - m)
                den = jnp.sum(e, axis=-1, keepdims=True) + jnp.exp(sink - m)
                ps.append((e * (1.0 / den)).astype(BF16))
            o_kv.append(_dot(jnp.concatenate(ps, axis=0), vb))
        for g in range(GROUP):
            og = jnp.where(low, o_kv[0][g * tt:(g + 1) * tt], o_kv[1][g * tt:(g + 1) * tt])
            o_scr[b * tt:(b + 1) * tt, g * LANES:(g + 1) * LANES] = og.astype(BF16)

    kband[:, 0:WINDOW, :] = kband[:, tt:kw, :]
    vband[:, 0:WINDOW, :] = vband[:, tt:kw, :]

    m_att = _sigmoid(_dot(h, win_ref[:, GA0:GL0])) * _dot(o_scr[...], woa_ref[...])

    c_row = -LRU_C * _softplus(-vec_ref[R_LAM:R_LAM + 1, :])
    for sp in range(LRU_WIDTH // SLAB):
        cols = slice(sp * SLAB, (sp + 1) * SLAB)
        xr = _dot(h, win_ref[:, XR0 + sp * SLAB:XR0 + (sp + 1) * SLAB])
        cs[:, SUBLANES:SUBLANES + tt, cols] = xr.reshape(nb, tt, SLAB)
        xc = vec_ref[R_CB:R_CB + 1, cols]
        for j in range(CONV_WIDTH):
            off = SUBLANES - (CONV_WIDTH - 1) + j
            xc = xc + vec_ref[R_CW0 + j:R_CW0 + j + 1, cols] * cs[:, off:off + tt, cols]
        tail = cs[:, tt:tt + SUBLANES, cols]
        nc_ref[:, :, cols] = tail
        cs[:, 0:SUBLANES, cols] = tail
        xc = xc.reshape(tm, SLAB)
        for i in range(SLAB // LRU_BLOCK):
            hd = sp * (SLAB // LRU_BLOCK) + i
            lc = slice(hd * LRU_BLOCK, (hd + 1) * LRU_BLOCK)
            a, bb = _lru_gates(xc[:, i * LRU_BLOCK:(i + 1) * LRU_BLOCK], wax_ref[hd],
                               vec_ref[R_BA:R_BA + 1, lc], vec_ref[R_BX:R_BX + 1, lc], c_row[:, lc])
            for b in range(nb):
                a_scr[hd, b * pitch:b * pitch + tt, :] = a[b * tt:(b + 1) * tt]
                b_scr[hd, b * pitch:b * pitch + tt, :] = bb[b * tt:(b + 1) * tt]

    def scan_step(ts, hs):
        out = []
        for hd in range(LRU_HEADS):
            rows = pl.ds(ts, nb, stride=pitch)
            hn = a_scr[hd, rows, :] * hs[hd] + b_scr[hd, rows, :]
            b_scr[hd, rows, :] = hn
            out.append(hn)
        return tuple(out)

    hs0 = tuple(hst[:, hd * LRU_BLOCK:(hd + 1) * LRU_BLOCK] for hd in range(LRU_HEADS))
    hs = lax.fori_loop(0, tt, scan_step, hs0, unroll=8)
    hfin = jnp.concatenate(hs, axis=1)
    hst[...] = hfin
    nh_ref[...] = hfin

    for sp in range(LRU_WIDTH // SLAB):
        cols = slice(sp * SLAB, (sp + 1) * SLAB)
        ge = jax.nn.gelu(_dot(h, win_ref[:, XG0 + sp * SLAB:XG0 + (sp + 1) * SLAB]))
        hseq = jnp.concatenate(
            [jnp.concatenate([b_scr[sp * (SLAB // LRU_BLOCK) + i, b * pitch:b * pitch + tt, :]
                              for b in range(nb)], axis=0)
             for i in range(SLAB // LRU_BLOCK)], axis=1)
        y_scr[:, cols] = (hseq * ge).astype(BF16)

    m_all = m_att + _sigmoid(_dot(h, win_ref[:, GL0:REST_WIDTH])) * _dot(y_scr[...], wol_ref[...])
    x1 = x + _dot(m_all.astype(BF16), wout_ref[...])
    x1_ref[...] = x1.reshape(nb, tt, D_MODEL)


def _const_spec(shape, index):
    return pl.BlockSpec(shape, index, pipeline_mode=pl.Buffered(1))


def _smem_spec():
    return pl.BlockSpec(memory_space=pltpu.SMEM)


def _mixer_prompt(layer, x, prm):
    nb, seq, _ = x.shape
    tt = PROMPT_TT
    assert nb == SUBLANES and seq % tt == 0 and WINDOW % tt == 0
    tm = nb * tt
    pitch = tt + SUBLANES
    kw = WINDOW + tt
    wl = lambda *z: lambda t: (layer,) + z
    kern = functools.partial(_mixer_prompt_kernel, nb=nb, tt=tt)
    return pl.pallas_call(
        kern,
        grid=(seq // tt,),
        in_specs=[
            _smem_spec(), _smem_spec(),
            pl.BlockSpec((nb, tt, D_MODEL), lambda t: (0, t, 0)),
            _const_spec((None, VEC_ROWS, D_MODEL), wl(0, 0)),
            _const_spec((WINDOW, 2 * WINDOW), lambda t: (0, 0)),
            _const_spec((Q_WIDTH, Q_WIDTH), lambda t: (0, 0)),
            _const_spec((None, D_MODEL, QKV_WIDTH), wl(0, 0)),
            _const_spec((None, D_MODEL, REST_WIDTH), wl(0, 0)),
            _const_spec((None, Q_WIDTH, D_MODEL), wl(0, 0)),
            _const_spec((None, LRU_HEADS, LRU_BLOCK, 2 * LRU_BLOCK), wl(0, 0, 0)),
            _const_spec((None, LRU_WIDTH, D_MODEL), wl(0, 0)),
            _const_spec((None, D_MODEL, D_MODEL), wl(0, 0)),
        ],
        out_specs=[
            pl.BlockSpec((nb, tt, D_MODEL), lambda t: (0, t, 0)),
            pl.BlockSpec((nb, WINDOW, KV_WIDTH), lambda t: (0, 0, 0)),
            pl.BlockSpec((nb, WINDOW, KV_WIDTH), lambda t: (0, 0, 0)),
            pl.BlockSpec((nb, LRU_WIDTH), lambda t: (0, 0)),
            pl.BlockSpec((nb, SUBLANES, LRU_WIDTH), lambda t: (0, 0, 0)),
        ],
        out_shape=[
            jax.ShapeDtypeStruct((nb, seq, D_MODEL), F32),
            jax.ShapeDtypeStruct((nb, WINDOW, KV_WIDTH), F32),
            jax.ShapeDtypeStruct((nb, WINDOW, KV_WIDTH), F32),
            jax.ShapeDtypeStruct((nb, LRU_WIDTH), F32),
            jax.ShapeDtypeStruct((nb, SUBLANES, LRU_WIDTH), F32),
        ],
        scratch_shapes=[
            pltpu.VMEM((N_HEADS, WINDOW, 2 * WINDOW), F32),
            pltpu.VMEM((nb, kw, KV_WIDTH), F32),
            pltpu.VMEM((nb, kw, KV_WIDTH), F32),
            pltpu.VMEM((nb, tt + SUBLANES, LRU_WIDTH), F32),
            pltpu.VMEM((LRU_HEADS, nb * pitch, LRU_BLOCK), F32),
            pltpu.VMEM((LRU_HEADS, nb * pitch, LRU_BLOCK), F32),
            pltpu.VMEM((nb, LRU_WIDTH), F32),
            pltpu.VMEM((tm, LRU_WIDTH), BF16),
            pltpu.VMEM((tm, Q_WIDTH), BF16),
        ],
        compiler_params=pltpu.CompilerParams(
            dimension_semantics=("arbitrary",), vmem_limit_bytes=VMEM_LIMIT_BYTES),
        name=f"mixer_prompt_l{layer}",
    )(prm["t5"], prm["sinks"][layer], x, prm["vecs"], prm["bucket_p"], prm["seg"], prm["w_qkv"],
      prm["w_rest"], prm["w_o_attn"], prm["w_ax"], prm["w_o_lru"], prm["w_out"])


def _mixer_sample_kernel(t5_ref, sinks_ref, x_ref, vec_ref, bucket_ref, seg_ref, wqkv_ref, win_ref,
                         woa_ref, wax_ref, wol_ref, wout_ref, ck_ref, cv_ref, h0_ref, sc_ref,
                         x1_ref, kn_ref, vn_ref, nh_ref, xr_ref,
                         h_scr, q_scr, o_scr, mb_scr,
                         *, nseq, bb):
    i = pl.program_id(0)
    nkeys = bb * WINDOW
    nrow = N_HEADS * bb

    @pl.when(i == 0)
    def _project():
        x = x_ref[...]
        h = _rms(x, vec_ref[R_LN1:R_LN1 + 1, :]).astype(BF16)
        h_scr[...] = h
        qkv = _dot(h, wqkv_ref[...])
        seg = seg_ref[...]
        q_scr[...] = _seg_rms(qkv[:, Q0:K0], seg,
                              vec_ref[R_QG:R_QG + 1, 0:Q_WIDTH] * (HEAD_DIM ** -0.5))
        kn_ref[...] = _seg_rms(qkv[:, K0:V0], seg[0:KV_WIDTH, 0:KV_WIDTH],
                               vec_ref[R_KG:R_KG + 1, 0:KV_WIDTH])
        vn_ref[...] = qkv[:, V0:QKV_WIDTH]
        bucket = bucket_ref[...]
        rowb = lax.broadcasted_iota(jnp.int32, (bb, nkeys), 0)
        colb = lax.broadcasted_iota(jnp.int32, (bb, nkeys), 1) // WINDOW
        for hh in range(N_HEADS):
            brow = _gather_bias(bucket, t5_ref, hh)
            brow = jnp.concatenate([brow] * bb, axis=1)
            mb_scr[hh * bb:(hh + 1) * bb, :] = jnp.where(rowb == colb, brow, NEG_INF)

    rows = pl.ds(pl.multiple_of(i * bb, bb), bb)
    lane = lax.broadcasted_iota(jnp.int32, (1, LANES), 1)
    low = lane < HEAD_DIM
    qblk = q_scr[rows, :]
    qz = []
    for kh in range(N_KV_HEADS):
        keep = low if kh == 0 else jnp.logical_not(low)
        for g in range(GROUP):
            qz.append(jnp.where(keep, qblk[:, g * LANES:(g + 1) * LANES], 0.0))
    qz = jnp.concatenate(qz, axis=0).astype(BF16)
    kc = ck_ref[...].reshape(nkeys, KV_WIDTH).astype(BF16)
    vc = cv_ref[...].reshape(nkeys, KV_WIDTH).astype(BF16)
    s = _dot_nt(qz, kc) + mb_scr[...]
    knew = jnp.concatenate([kn_ref[rows, :].astype(BF16).astype(F32)] * N_HEADS, axis=0)
    vnew = jnp.concatenate([vn_ref[rows, :].astype(BF16).astype(F32)] * N_HEADS, axis=0)
    self_bias = jnp.concatenate(
        [jnp.full((bb, 1), t5_ref[hh], F32) for hh in range(N_HEADS)], axis=0)
    sink = jnp.concatenate(
        [jnp.full((bb, 1), sinks_ref[hh], F32) for hh in range(N_HEADS)], axis=0)
    s_self = jnp.sum(qz.astype(F32) * knew, axis=-1, keepdims=True) + self_bias
    m = jnp.maximum(jnp.maximum(jnp.max(s, axis=-1, keepdims=True), s_self), sink)
    e = jnp.exp(s - m)
    e_self = jnp.exp(s_self - m)
    den = jnp.sum(e, axis=-1, keepdims=True) + e_self + jnp.exp(sink - m)
    inv = 1.0 / den
    o = (_dot((e * inv).astype(BF16), vc)
         + (e_self * inv).astype(BF16).astype(F32) * vnew)
    half = GROUP * bb
    for g in range(GROUP):
        og = jnp.where(low, o[g * bb:(g + 1) * bb], o[half + g * bb:half + (g + 1) * bb])
        o_scr[rows, g * LANES:(g + 1) * LANES] = og

    @pl.when(i == pl.num_programs(0) - 1)
    def _finish():
        x = x_ref[...]
        h = h_scr[...]
        m_att = _sigmoid(_dot(h, win_ref[:, GA0:GL0])) * _dot(o_scr[...].astype(BF16), woa_ref[...])
        xr = _dot(h, win_ref[:, XR0:XG0])
        xr_ref[...] = xr
        xc = vec_ref[R_CB:R_CB + 1, :] + vec_ref[R_CW0 + CONV_WIDTH - 1:R_CW0 + CONV_WIDTH, :] * xr
        for j in range(CONV_WIDTH - 1):
            xc = xc + vec_ref[R_CW0 + j:R_CW0 + j + 1, :] * sc_ref[j]
        c_row = -LRU_C * _softplus(-vec_ref[R_LAM:R_LAM + 1, :])
        hn = []
        for hd in range(LRU_HEADS):
            lc = slice(hd * LRU_BLOCK, (hd + 1) * LRU_BLOCK)
            a, bb_ = _lru_gates(xc[:, lc], wax_ref[hd], vec_ref[R_BA:R_BA + 1, lc],
                                vec_ref[R_BX:R_BX + 1, lc], c_row[:, lc])
            hn.append(a * h0_ref[:, lc] + bb_)
        hn = jnp.concatenate(hn, axis=1)
        nh_ref[...] = hn
        y = (hn * jax.nn.gelu(_dot(h, win_ref[:, XG0:GA0]))).astype(BF16)
        m_all = m_att + _sigmoid(_dot(h, win_ref[:, GL0:REST_WIDTH])) * _dot(y, wol_ref[...])
        x1_ref[...] = x + _dot(m_all.astype(BF16), wout_ref[...])


def _mixer_sample(layer, x, ck, cv, h0, sc, prm):
    nseq = x.shape[0]
    bb = SAMPLE_BB
    assert nseq % bb == 0
    wl = lambda *z: lambda i: (layer,) + z
    full2 = lambda i: (0, 0)
    kern = functools.partial(_mixer_sample_kernel, nseq=nseq, bb=bb)
    return pl.pallas_call(
        kern,
        grid=(nseq // bb,),
        in_specs=[
            _smem_spec(), _smem_spec(),
            _const_spec((nseq, D_MODEL), full2),
            _const_spec((None, VEC_ROWS, D_MODEL), wl(0, 0)),
            _const_spec((1, WINDOW), full2),
            _const_spec((Q_WIDTH, Q_WIDTH), full2),
            _const_spec((None, D_MODEL, QKV_WIDTH), wl(0, 0)),
            _const_spec((None, D_MODEL, REST_WIDTH), wl(0, 0)),
            _const_spec((None, Q_WIDTH, D_MODEL), wl(0, 0)),
            _const_spec((None, LRU_HEADS, LRU_BLOCK, 2 * LRU_BLOCK), wl(0, 0, 0)),
            _const_spec((None, LRU_WIDTH, D_MODEL), wl(0, 0)),
            _const_spec((None, D_MODEL, D_MODEL), wl(0, 0)),
            pl.BlockSpec((None, bb, WINDOW, KV_WIDTH), lambda i: (layer, i, 0, 0)),
            pl.BlockSpec((None, bb, WINDOW, KV_WIDTH), lambda i: (layer, i, 0, 0)),
            _const_spec((None, nseq, LRU_WIDTH), wl(0, 0)),
            _const_spec((None, CONV_WIDTH - 1, nseq, LRU_WIDTH), wl(0, 0, 0)),
        ],
        out_specs=[
            pl.BlockSpec((nseq, D_MODEL), full2),
            pl.BlockSpec((nseq, KV_WIDTH), full2),
            pl.BlockSpec((nseq, KV_WIDTH), full2),
            pl.BlockSpec((nseq, LRU_WIDTH), full2),
            pl.BlockSpec((nseq, LRU_WIDTH), full2),
        ],
        out_shape=[
            jax.ShapeDtypeStruct((nseq, D_MODEL), F32),
            jax.ShapeDtypeStruct((nseq, KV_WIDTH), F32),
            jax.ShapeDtypeStruct((nseq, KV_WIDTH), F32),
            jax.ShapeDtypeStruct((nseq, LRU_WIDTH), F32),
            jax.ShapeDtypeStruct((nseq, LRU_WIDTH), F32),
        ],
        scratch_shapes=[
            pltpu.VMEM((nseq, D_MODEL), BF16),
            pltpu.VMEM((nseq, Q_WIDTH), F32),
            pltpu.VMEM((nseq, Q_WIDTH), F32),
            pltpu.VMEM((N_HEADS * bb, bb * WINDOW), F32),
        ],
        compiler_params=pltpu.CompilerParams(
            dimension_semantics=("arbitrary",), vmem_limit_bytes=VMEM_LIMIT_BYTES),
        name=f"mixer_sample_l{layer}",
    )(prm["t5"], prm["sinks"][layer], x, prm["vecs"], prm["bucket_s"], prm["seg"], prm["w_qkv"],
      prm["w_rest"], prm["w_o_attn"], prm["w_ax"], prm["w_o_lru"], prm["w_out"], ck, cv, h0, sc)


FF_CHUNKS = ((0, 1024), (1024, 2048), (2048, D_FF))


def _ffn_kernel(x_ref, p_ref, vec_ref, wg_ref, wu_ref, wd_ref, wpg_ref, wp_ref, o_ref, act_scr):
    x = x_ref[...]
    h2 = _rms(x, vec_ref[R_LN2:R_LN2 + 1, :]).astype(BF16)
    for lo, hi in FF_CHUNKS:
        g = _dot(h2, wg_ref[:, lo:hi])
        act = g * _sigmoid(g) * _dot(h2, wu_ref[:, lo:hi])
        act_scr[:, lo:hi] = act.astype(BF16)
    x = x + _dot(act_scr[...], wd_ref[...])
    h3 = _rms(x, vec_ref[R_LN3:R_LN3 + 1, :]).astype(BF16)
    gate = _sigmoid(_dot(h3, wpg_ref[...]))
    o_ref[...] = x + gate * _dot(p_ref[...].astype(BF16), wp_ref[...])


def _ffn(layer, x, p, prm, tag):
    rows = x.shape[0]
    tm = min(FFN_TM, rows)
    assert rows % tm == 0
    wl = lambda *z: lambda r: (layer,) + z
    return pl.pallas_call(
        _ffn_kernel,
        grid=(rows // tm,),
        in_specs=[
            pl.BlockSpec((tm, D_MODEL), lambda r: (r, 0)),
            pl.BlockSpec((None, tm, PLE_DIM), lambda r: (layer, r, 0)),
            _const_spec((None, VEC_ROWS, D_MODEL), wl(0, 0)),
            _const_spec((None, D_MODEL, D_FF), wl(0, 0)),
            _const_spec((None, D_MODEL, D_FF), wl(0, 0)),
            _const_spec((None, D_FF, D_MODEL), wl(0, 0)),
            _const_spec((None, D_MODEL, D_MODEL), wl(0, 0)),
            _const_spec((None, PLE_DIM, D_MODEL), wl(0, 0)),
        ],
        out_specs=pl.BlockSpec((tm, D_MODEL), lambda r: (r, 0)),
        out_shape=jax.ShapeDtypeStruct((rows, D_MODEL), F32),
        scratch_shapes=[pltpu.VMEM((tm, D_FF), BF16)],
        compiler_params=pltpu.CompilerParams(
            dimension_semantics=("arbitrary",), vmem_limit_bytes=VMEM_LIMIT_BYTES),
        name=f"ffn_{tag}_l{layer}",
    )(x, p, prm["vecs"], prm["w_gate"], prm["w_up"], prm["w_down"], prm["w_ple_gate"], prm["w_ple"])


def _t5_bucket(dist):
    n = jnp.maximum(dist, 0)
    max_exact = N_BUCKETS // 2
    nf = jnp.maximum(n, 1).astype(F32)
    large = max_exact + (jnp.log(nf / max_exact) / math.log(MAX_DISTANCE / max_exact)
                         * (N_BUCKETS - max_exact)).astype(jnp.int32)
    large = jnp.minimum(large, N_BUCKETS - 1)
    return jnp.where(n < max_exact, n, large)


def _regroup_heads(w, axis):
    shape = w.shape
    w = w.reshape(shape[:axis] + (N_KV_HEADS, GROUP, HEAD_DIM) + shape[axis + 1:])
    return jnp.swapaxes(w, axis, axis + 1).reshape(shape)


def _prepare(t5_table, ln1, w_in, q_gain, k_gain, sinks, w_o_attn, conv_w, conv_b, w_a, b_a,
             w_x, b_x, lam, w_o_lru, w_out, ln2, w_gate, w_up, w_down, ln3, w_ple, w_ple_gate):
    depth = w_in.shape[0]
    w_qkv = jnp.concatenate(
        [_regroup_heads(w_in[:, :, 0:Q_WIDTH], 2), w_in[:, :, Q_WIDTH:QKV_WIDTH]], axis=2).astype(BF16)
    vecs =jnp.zeros((depth, VEC_ROWS, D_MODEL), F32)
    vecs = vecs.at[:, R_LN1].set(ln1)
    vecs = vecs.at[:, R_CW0:R_CW0 + CONV_WIDTH].set(conv_w)
    vecs = vecs.at[:, R_CB].set(conv_b)
    vecs = vecs.at[:, R_BA].set(b_a)
    vecs = vecs.at[:, R_BX].set(b_x)
    vecs = vecs.at[:, R_LAM].set(lam)
    vecs = vecs.at[:, R_QG, 0:Q_WIDTH].set(jnp.tile(q_gain, (1, N_HEADS)))
    vecs = vecs.at[:, R_KG, 0:KV_WIDTH].set(jnp.tile(k_gain, (1, N_KV_HEADS)))
    vecs = vecs.at[:, R_LN2].set(ln2)
    vecs = vecs.at[:, R_LN3].set(ln3)
    head_id = np.arange(Q_WIDTH) // HEAD_DIM
    seg = jnp.asarray(head_id[:, None] == head_id[None, :], BF16)
    dist_p = (WINDOW + jnp.arange(WINDOW))[:, None] - jnp.arange(2 * WINDOW)[None, :]
    bucket_p = jnp.where((dist_p >= 0) & (dist_p < WINDOW), _t5_bucket(dist_p), -1).astype(jnp.int32)
    dist_s = (WINDOW - jnp.arange(WINDOW))[None, :]
    bucket_s = jnp.where((dist_s >= 0) & (dist_s < WINDOW), _t5_bucket(dist_s), -1).astype(jnp.int32)
    return {
        "t5": t5_table.reshape(-1),
        "sinks": sinks,
        "vecs": vecs,
        "seg": seg,
        "bucket_p": bucket_p,
        "bucket_s": bucket_s,
        "w_qkv": w_qkv,
        "w_rest": w_in[:, :, QKV_WIDTH:].astype(BF16),
        "w_o_attn": _regroup_heads(w_o_attn, 1).astype(BF16),
        "w_ax": jnp.concatenate([w_a, w_x], axis=-1).astype(BF16),
        "w_o_lru": w_o_lru.astype(BF16),
        "w_out": w_out.astype(BF16),
        "w_gate": w_gate.astype(BF16),
        "w_up": w_up.astype(BF16),
        "w_down": w_down.astype(BF16),
        "w_ple_gate": w_ple_gate.astype(BF16),
        "w_ple": w_ple.astype(BF16),
    }


def kernel(x_prompt, x_sample, cache_k_win, cache_v_win, state_lru_h, state_conv, p_prompt,
           p_sample, t5_table, ln1, w_in, q_gain, k_gain, sinks, w_o_attn, conv_w, conv_b, w_a,
           b_a, w_x, b_x, lam, w_o_lru, w_out, ln2, w_gate, w_up, w_down, ln3, w_ple, w_ple_gate):
    depth = w_in.shape[0]
    nb, seq, _ = x_prompt.shape
    nseq = x_sample.shape[0]
    assert x_sample.shape[1] == 1 and cache_k_win.shape[2] == WINDOW
    prm = _prepare(t5_table, ln1, w_in, q_gain, k_gain, sinks, w_o_attn, conv_w, conv_b, w_a, b_a,
                   w_x, b_x, lam, w_o_lru, w_out, ln2, w_gate, w_up, w_down, ln3, w_ple, w_ple_gate)
    ck = cache_k_win.reshape(depth, nseq, WINDOW, KV_WIDTH)
    cv = cache_v_win.reshape(depth, nseq, WINDOW, KV_WIDTH)
    sc = jnp.transpose(state_conv, (0, 2, 1, 3))
    pp = p_prompt.reshape(depth, nb * seq, PLE_DIM)
    ps = p_sample.reshape(depth, nseq, PLE_DIM)

    yp = x_prompt
    ys = x_sample.reshape(nseq, D_MODEL)
    outs = [[] for _ in range(8)]
    for layer in range(depth):
        x1, nk, nv, nh, nc = _mixer_prompt(layer, yp, prm)
        yp = _ffn(layer, x1.reshape(nb * seq, D_MODEL), pp, prm, "prompt").reshape(nb, seq, D_MODEL)
        outs[0].append(nk.reshape(nb, WINDOW, N_KV_HEADS, HEAD_DIM))
        outs[1].append(nv.reshape(nb, WINDOW, N_KV_HEADS, HEAD_DIM))
        outs[2].append(nh)
        outs[3].append(nc[:, SUBLANES - (CONV_WIDTH - 1):, :])

        x1s, kn, vn, nhs, xr = _mixer_sample(layer, ys, ck, cv, state_lru_h, sc, prm)
        ys = _ffn(layer, x1s, ps, prm, "sample")
        outs[4].append(jnp.concatenate([ck[layer, :, 1:], kn[:, None, :]], axis=1)
                       .reshape(nseq, WINDOW, N_KV_HEADS, HEAD_DIM))
        outs[5].append(jnp.concatenate([cv[layer, :, 1:], vn[:, None, :]], axis=1)
                       .reshape(nseq, WINDOW, N_KV_HEADS, HEAD_DIM))
        outs[6].append(nhs)
        outs[7].append(jnp.concatenate([state_conv[layer, :, 1:], xr[:, None, :]], axis=1))
    return (yp, ys.reshape(nseq, 1, D_MODEL)) + tuple(jnp.stack(o) for o in outs)
```

```python
import functools
import math

import numpy as np
import jax
import jax.numpy as jnp
from jax import lax
from jax.experimental import pallas as pl
from jax.experimental.pallas import tpu as pltpu

F32 = jnp.float32
BF16 = jnp.bfloat16

D_MODEL = 1024
HEAD_DIM = 64
N_HEADS = 8
N_KV_HEADS = 2
GROUP = N_HEADS // N_KV_HEADS
Q_WIDTH = N_HEADS * HEAD_DIM
KV_WIDTH = N_KV_HEADS * HEAD_DIM
WINDOW = 128
N_BUCKETS = 32
MAX_DISTANCE = 128
LRU_WIDTH = D_MODEL
LRU_HEADS = 8
LRU_BLOCK = LRU_WIDTH // LRU_HEADS
LRU_C = 8.0
CONV_WIDTH = 4
D_FF = 2816
PLE_DIM = 256
PAST_LEN = 8192
EPS = 1e-6
NEG_INF = -1e30
TINY = 1e-30
LOG2E = math.log2(math.e)

SUBLANES = 8
LANES = 128
VMEM_LIMIT_BYTES = 56 * 1024 * 1024

Q0 = 0
K0 = Q0 + Q_WIDTH
V0 = K0 + KV_WIDTH
QKV_WIDTH = V0 + KV_WIDTH
XR0 = QKV_WIDTH
XG0 = XR0 + LRU_WIDTH
GA0 = XG0 + LRU_WIDTH
GL0 = GA0 + D_MODEL
IN_WIDTH = GL0 + D_MODEL

R_LN1, R_CW0, R_CB, R_BA, R_BX, R_LAM, R_QG, R_KG, R_LN2, R_LN3 = 0, 1, 5, 6, 7, 8, 9, 10, 11, 12
VEC_ROWS = 16

PROMPT_TT = 64
FFN_TM = 512
SAMPLE_BB = 8
PROJ_CHUNK = 512


def _dot(a, b):
    return jnp.dot(a, b, preferred_element_type=F32)


def _dot_nt(a, b):
    return lax.dot_general(a, b, (((1,), (1,)), ((), ())), preferred_element_type=F32)


def _rms(x, g):
    ms = jnp.mean(x * x, axis=-1, keepdims=True)
    return x * lax.rsqrt(ms + EPS) * g


def _seg_rms(x, seg, g):
    x2 = x * x
    ssq = _dot(x2.astype(BF16), seg)
    return x * lax.rsqrt(ssq * (1.0 / HEAD_DIM) + EPS) * g


def _sigmoid(x):
    return 0.5 * jnp.tanh(0.5 * x) + 0.5


def _softplus(z):
    return jnp.maximum(z, 0.0) + jnp.log1p(jnp.exp(-jnp.abs(z)))


def _gather_bias(bucket, t5_ref, head):
    acc = jnp.full(bucket.shape, NEG_INF, F32)
    for bkt in range(N_BUCKETS):
        acc = jnp.where(bucket == bkt, t5_ref[bkt * N_HEADS + head], acc)
    return acc


def _lru_gates(xch, wax, b_a, b_x, c_row):
    gts = _dot(xch.astype(BF16), wax)
    r = _sigmoid(gts[:, :LRU_BLOCK] + b_a)
    ig = _sigmoid(gts[:, LRU_BLOCK:] + b_x)
    log_a = c_row * r
    a = jnp.exp(log_a)
    y = 1.0 - a * a
    return a, y * lax.rsqrt(jnp.maximum(y, TINY)) * (ig * xch)


def _mixer_prompt_kernel(t5_ref, sinks_ref, x_ref, vec_ref, bucket_ref, seg_ref, wqkv_ref, win_ref,
                         woa_ref, wax_ref, wol_ref, wout_ref,
                         x1_ref, nk_ref, nv_ref, nh_ref, nc_ref,
                         bias_scr, bias_t, kband, vband, cs, a_scr, b_scr, h_scr, hst, y_scr, o_scr,
                         proj_scr, q_scr, *, nb, tt):
    tm = nb * tt
    t = pl.program_id(0)

    @pl.when(t == 0)
    def _init():
        bucket = bucket_ref[...]
        for hh in range(N_HEADS):
            bias_scr[hh] = _gather_bias(bucket, t5_ref, hh) * LOG2E
        kband[...] = jnp.zeros((nb, 2 * WINDOW, KV_WIDTH), F32)
        vband[...] = jnp.zeros((nb, 2 * WINDOW, KV_WIDTH), F32)
        cs[:, 0:SUBLANES, :] = jnp.zeros((nb, SUBLANES, LRU_WIDTH), F32)
        hst[...] = jnp.zeros((nb, LRU_WIDTH), F32)

    x = x_ref[...].reshape(tm, D_MODEL)
    h = _rms(x, vec_ref[R_LN1:R_LN1 + 1, :]).astype(BF16)

    cs[:, SUBLANES:SUBLANES + tt, :] = _dot(h, win_ref[:, XR0:XG0]).reshape(nb, tt, LRU_WIDTH)
    c_row = -LRU_C * _softplus(-vec_ref[R_LAM:R_LAM + 1, :])
    n_chunks = (IN_WIDTH - XG0) // PROJ_CHUNK
    qkv = None
    for hd in range(LRU_HEADS):
        lc = slice(hd * LRU_BLOCK, (hd + 1) * LRU_BLOCK)
        xc = vec_ref[R_CB:R_CB + 1, lc]
        for j in range(CONV_WIDTH):
            off = SUBLANES - (CONV_WIDTH - 1) + j
            xc = xc + vec_ref[R_CW0 + j:R_CW0 + j + 1, lc] * cs[:, off:off + tt, lc]
        a, bb = _lru_gates(xc.reshape(tm, LRU_BLOCK), wax_ref[hd], vec_ref[R_BA:R_BA + 1, lc],
                           vec_ref[R_BX:R_BX + 1, lc], c_row[:, lc])
        for b in range(nb):
            rows = pl.ds(b, tt, stride=nb)
            a_scr[hd, rows, :] = a[b * tt:(b + 1) * tt]
            b_scr[hd, rows, :] = bb[b * tt:(b + 1) * tt]
        if hd < n_chunks:
            pc = slice(hd * PROJ_CHUNK, (hd + 1) * PROJ_CHUNK)
            proj_scr[:, pc] = _dot(h, win_ref[:, XG0 + hd * PROJ_CHUNK:XG0 + (hd + 1) * PROJ_CHUNK])
        elif hd == n_chunks:
            qkv = _dot(h, wqkv_ref[...])
    tail = cs[:, tt:tt + SUBLANES, :]
    nc_ref[...] = tail
    cs[:, 0:SUBLANES, :] = tail

    seg = seg_ref[...]
    q_scr[...] = _seg_rms(qkv[:, Q0:K0], seg,
                          vec_ref[R_QG:R_QG + 1, 0:Q_WIDTH] * (LOG2E * HEAD_DIM ** -0.5))
    kn = _seg_rms(qkv[:, K0:V0], seg[0:KV_WIDTH, 0:KV_WIDTH], vec_ref[R_KG:R_KG + 1, 0:KV_WIDTH])
    kband[:, WINDOW:WINDOW + tt, :] = kn.reshape(nb, tt, KV_WIDTH)
    vband[:, WINDOW:WINDOW + tt, :] = qkv[:, V0:QKV_WIDTH].reshape(nb, tt, KV_WIDTH)
    nk_ref[...] = kband[:, tt:tt + WINDOW, :]
    nv_ref[...] = vband[:, tt:tt + WINDOW, :]

    @pl.when(t * tt <= WINDOW)
    def _mask_bias():
        col = lax.broadcasted_iota(jnp.int32, (1, 2 * WINDOW), 1)
        colmask = jnp.where(col >= WINDOW - t * tt, 0.0, NEG_INF)
        for hh in range(N_HEADS):
            bias_t[hh] = bias_scr[hh, 0:tt, :] + colmask

    lane = lax.broadcasted_iota(jnp.int32, (1, LANES), 1)
    low = lane < HEAD_DIM
    for b in range(nb):
        kb = kband[b].astype(BF16)
        vb = vband[b].astype(BF16)
        o_kv = []
        for kh in range(N_KV_HEADS):
            keep = low if kh == 0 else jnp.logical_not(low)
            q4 = jnp.concatenate(
                [jnp.where(keep, q_scr[b * tt:(b + 1) * tt, g * LANES:(g + 1) * LANES], 0.0)
                 for g in range(GROUP)], axis=0).astype(BF16)
            s = _dot_nt(q4, kb)
            ps, invs = [], []
            for g in range(GROUP):
                hh = kh * GROUP + g
                sg = s[g * tt:(g + 1) * tt] + bias_t[hh]
                sink = sinks_ref[hh] * LOG2E
                m = jnp.maximum(jnp.max(sg, axis=-1, keepdims=True), sink)
                e = jnp.exp2(sg - m)
                invs.append(1.0 / (jnp.sum(e, axis=-1, keepdims=True) + jnp.exp2(sink - m)))
                ps.append(e.astype(BF16))
            o4 = _dot(jnp.concatenate(ps, axis=0), vb)
            o_kv.append([o4[g * tt:(g + 1) * tt] * invs[g] for g in range(GROUP)])
        for g in range(GROUP):
            og = jnp.where(low, o_kv[0][g], o_kv[1][g])
            o_scr[b * tt:(b + 1) * tt, g * LANES:(g + 1) * LANES] = og.astype(BF16)

    kband[:, 0:WINDOW, :] = kband[:, tt:tt + WINDOW, :]
    vband[:, 0:WINDOW, :] = vband[:, tt:tt + WINDOW, :]

    m_att = _sigmoid(proj_scr[:, GA0 - XG0:GL0 - XG0]) * _dot(o_scr[...], woa_ref[...])

    def scan_step(ts, hs):
        out = []
        for hd in range(LRU_HEADS):
            rows = pl.ds(pl.multiple_of(ts * nb, nb), nb)
            hn = a_scr[hd, rows, :] * hs[hd] + b_scr[hd, rows, :]
            h_scr[hd, rows, :] = hn
            out.append(hn)
        return tuple(out)

    hs0 = tuple(hst[:, hd * LRU_BLOCK:(hd + 1) * LRU_BLOCK] for hd in range(LRU_HEADS))
    hs = lax.fori_loop(0, tt, scan_step, hs0, unroll=8)
    hfin = jnp.concatenate(hs, axis=1)
    hst[...] = hfin
    nh_ref[...] = hfin

    for hd in range(LRU_HEADS):
        lc = slice(hd * LRU_BLOCK, (hd + 1) * LRU_BLOCK)
        hseq = jnp.concatenate([h_scr[hd, pl.ds(b, tt, stride=nb), :] for b in range(nb)], axis=0)
        y_scr[:, lc] = (hseq * jax.nn.gelu(proj_scr[:, lc])).astype(BF16)

    m_all = m_att + _sigmoid(proj_scr[:, GL0 - XG0:IN_WIDTH - XG0]) * _dot(y_scr[...], wol_ref[...])
    x1 = x + _dot(m_all.astype(BF16), wout_ref[...])
    x1_ref[...] = x1.reshape(nb, tt, D_MODEL)


def _const_spec(shape, index):
    return pl.BlockSpec(shape, index, pipeline_mode=pl.Buffered(1))


def _smem_spec():
    return pl.BlockSpec(memory_space=pltpu.SMEM)


def _mixer_prompt(layer, x, prm):
    nb, seq, _ = x.shape
    tt = PROMPT_TT
    assert nb == SUBLANES and seq % tt == 0 and WINDOW % tt == 0
    tm = nb * tt
    wl = lambda *z: lambda t: (layer,) + z
    kern = functools.partial(_mixer_prompt_kernel, nb=nb, tt=tt)
    return pl.pallas_call(
        kern,
        grid=(seq // tt,),
        in_specs=[
            _smem_spec(), _smem_spec(),
            pl.BlockSpec((nb, tt, D_MODEL), lambda t: (0, t, 0)),
            _const_spec((None, VEC_ROWS, D_MODEL), wl(0, 0)),
            _const_spec((WINDOW, 2 * WINDOW), lambda t: (0, 0)),
            _const_spec((Q_WIDTH, Q_WIDTH), lambda t: (0, 0)),
            _const_spec((None, D_MODEL, QKV_WIDTH), wl(0, 0)),
            _const_spec((None, D_MODEL, IN_WIDTH), wl(0, 0)),
            _const_spec((None, Q_WIDTH, D_MODEL), wl(0, 0)),
            _const_spec((None, LRU_HEADS, LRU_BLOCK, 2 * LRU_BLOCK), wl(0, 0, 0)),
            _const_spec((None, LRU_WIDTH, D_MODEL), wl(0, 0)),
            _const_spec((None, D_MODEL, D_MODEL), wl(0, 0)),
        ],
        out_specs=[
            pl.BlockSpec((nb, tt, D_MODEL), lambda t: (0, t, 0)),
            pl.BlockSpec((nb, WINDOW, KV_WIDTH), lambda t: (0, 0, 0)),
            pl.BlockSpec((nb, WINDOW, KV_WIDTH), lambda t: (0, 0, 0)),
            pl.BlockSpec((nb, LRU_WIDTH), lambda t: (0, 0)),
            pl.BlockSpec((nb, SUBLANES, LRU_WIDTH), lambda t: (0, 0, 0)),
        ],
        out_shape=[
            jax.ShapeDtypeStruct((nb, seq, D_MODEL), F32),
            jax.ShapeDtypeStruct((nb, WINDOW, KV_WIDTH), F32),
            jax.ShapeDtypeStruct((nb, WINDOW, KV_WIDTH), F32),
            jax.ShapeDtypeStruct((nb, LRU_WIDTH), F32),
            jax.ShapeDtypeStruct((nb, SUBLANES, LRU_WIDTH), F32),
        ],
        scratch_shapes=[
            pltpu.VMEM((N_HEADS, WINDOW, 2 * WINDOW), F32),
            pltpu.VMEM((N_HEADS, tt, 2 * WINDOW), F32),
            pltpu.VMEM((nb, 2 * WINDOW, KV_WIDTH), F32),
            pltpu.VMEM((nb, 2 * WINDOW, KV_WIDTH), F32),
            pltpu.VMEM((nb, tt + SUBLANES, LRU_WIDTH), F32),
            pltpu.VMEM((LRU_HEADS, tm, LRU_BLOCK), F32),
            pltpu.VMEM((LRU_HEADS, tm, LRU_BLOCK), F32),
            pltpu.VMEM((LRU_HEADS, tm, LRU_BLOCK), F32),
            pltpu.VMEM((nb, LRU_WIDTH), F32),
            pltpu.VMEM((tm, LRU_WIDTH), BF16),
            pltpu.VMEM((tm, Q_WIDTH), BF16),
            pltpu.VMEM((tm, IN_WIDTH - XG0), F32),
            pltpu.VMEM((tm, Q_WIDTH), F32),
        ],
        compiler_params=pltpu.CompilerParams(
            dimension_semantics=("arbitrary",), vmem_limit_bytes=VMEM_LIMIT_BYTES),
        name=f"mixer_prompt_l{layer}",
    )(prm["t5"], prm["sinks"][layer], x, prm["vecs"], prm["bucket_p"], prm["seg"], prm["w_qkv"],
      prm["w_in"], prm["w_o_attn"], prm["w_ax"], prm["w_o_lru"], prm["w_out"])


def _mixer_sample_kernel(t5_ref, sinks_ref, x_ref, vec_ref, bucket_ref, seg_ref, wqkv_ref, win_ref,
                         woa_ref, wax_ref, wol_ref, wout_ref, ck_ref, cv_ref, h0_ref, sc_ref,
                         nk_all_ref, nv_all_ref,
                         x1_ref, nh_ref, xr_ref, nk_ref, nv_ref,
                         h_scr, q_scr, o_scr, mb_scr, kn_scr, vn_scr,
                         *, nseq, bb):
    del nk_all_ref, nv_all_ref
    i = pl.program_id(0)
    nkeys = bb * WINDOW

    @pl.when(i == 0)
    def _project():
        x = x_ref[...]
        h = _rms(x, vec_ref[R_LN1:R_LN1 + 1, :]).astype(BF16)
        h_scr[...] = h
        qkv = _dot(h, wqkv_ref[...])
        seg = seg_ref[...]
        q_scr[...] = _seg_rms(qkv[:, Q0:K0], seg,
                              vec_ref[R_QG:R_QG + 1, 0:Q_WIDTH] * (HEAD_DIM ** -0.5))
        kn_scr[...] = _seg_rms(qkv[:, K0:V0], seg[0:KV_WIDTH, 0:KV_WIDTH],
                               vec_ref[R_KG:R_KG + 1, 0:KV_WIDTH])
        vn_scr[...] = qkv[:, V0:QKV_WIDTH]
        bucket = bucket_ref[...]
        rowb = lax.broadcasted_iota(jnp.int32, (bb, nkeys), 0)
        colb = lax.broadcasted_iota(jnp.int32, (bb, nkeys), 1) // WINDOW
        for hh in range(N_HEADS):
            brow = _gather_bias(bucket, t5_ref, hh)
            brow = jnp.concatenate([brow] * bb, axis=1)
            mb_scr[hh * bb:(hh + 1) * bb, :] = jnp.where(rowb == colb, brow, NEG_INF)

    rows = pl.ds(pl.multiple_of(i * bb, bb), bb)
    lane = lax.broadcasted_iota(jnp.int32, (1, LANES), 1)
    low = lane < HEAD_DIM
    qblk = q_scr[rows, :]
    qz = []
    for kh in range(N_KV_HEADS):
        keep = low if kh == 0 else jnp.logical_not(low)
        for g in range(GROUP):
            qz.append(jnp.where(keep, qblk[:, g * LANES:(g + 1) * LANES], 0.0))
    qz = jnp.concatenate(qz, axis=0).astype(BF16)
    kt = jnp.concatenate([ck_ref[b] for b in range(bb)], axis=1).astype(BF16)
    vt = jnp.concatenate([cv_ref[b] for b in range(bb)], axis=1).astype(BF16)
    s = _dot(qz, kt) + mb_scr[...]
    kn_blk = kn_scr[rows, :]
    vn_blk = vn_scr[rows, :]
    knew = jnp.concatenate([kn_blk.astype(BF16).astype(F32)] * N_HEADS, axis=0)
    vnew = jnp.concatenate([vn_blk.astype(BF16).astype(F32)] * N_HEADS, axis=0)
    self_bias = jnp.concatenate(
        [jnp.full((bb, 1), t5_ref[hh], F32) for hh in range(N_HEADS)], axis=0)
    sink = jnp.concatenate(
        [jnp.full((bb, 1), sinks_ref[hh], F32) for hh in range(N_HEADS)], axis=0)
    s_self = jnp.sum(qz.astype(F32) * knew, axis=-1, keepdims=True) + self_bias
    m = jnp.maximum(jnp.maximum(jnp.max(s, axis=-1, keepdims=True), s_self), sink)
    e = jnp.exp(s - m)
    e_self = jnp.exp(s_self - m)
    den = jnp.sum(e, axis=-1, keepdims=True) + e_self + jnp.exp(sink - m)
    inv = 1.0 / den
    o = (_dot_nt((e * inv).astype(BF16), vt)
         + (e_self * inv).astype(BF16).astype(F32) * vnew)
    half = GROUP * bb
    for g in range(GROUP):
        og = jnp.where(low, o[g * bb:(g + 1) * bb], o[half + g * bb:half + (g + 1) * bb])
        o_scr[rows, g * LANES:(g + 1) * LANES] = og

    knt = kn_blk.T
    vnt = vn_blk.T
    last = lax.broadcasted_iota(jnp.int32, (1, WINDOW), 1) == WINDOW - 1
    for b in range(bb):
        nk_ref[b] = jnp.where(last, knt[:, b:b + 1], pltpu.roll(ck_ref[b], WINDOW - 1, 1))
        nv_ref[b] = jnp.where(last, vnt[:, b:b + 1], pltpu.roll(cv_ref[b], WINDOW - 1, 1))

    @pl.when(i == pl.num_programs(0) - 1)
    def _finish():
        x = x_ref[...]
        h = h_scr[...]
        m_att = _sigmoid(_dot(h, win_ref[:, GA0:GL0])) * _dot(o_scr[...].astype(BF16), woa_ref[...])
        xr = _dot(h, win_ref[:, XR0:XG0])
        xr_ref[...] = xr
        xc = vec_ref[R_CB:R_CB + 1, :] + vec_ref[R_CW0 + CONV_WIDTH - 1:R_CW0 + CONV_WIDTH, :] * xr
        for j in range(CONV_WIDTH - 1):
            xc = xc + vec_ref[R_CW0 + j:R_CW0 + j + 1, :] * sc_ref[j]
        c_row = -LRU_C * _softplus(-vec_ref[R_LAM:R_LAM + 1, :])
        hn = []
        for hd in range(LRU_HEADS):
            lc = slice(hd * LRU_BLOCK, (hd + 1) * LRU_BLOCK)
            a, bb_ = _lru_gates(xc[:, lc], wax_ref[hd], vec_ref[R_BA:R_BA + 1, lc],
                                vec_ref[R_BX:R_BX + 1, lc], c_row[:, lc])
            hn.append(a * h0_ref[:, lc] + bb_)
        hn = jnp.concatenate(hn, axis=1)
        nh_ref[...] = hn
        y = (hn * jax.nn.gelu(_dot(h, win_ref[:, XG0:GA0]))).astype(BF16)
        m_all = m_att + _sigmoid(_dot(h, win_ref[:, GL0:IN_WIDTH])) * _dot(y, wol_ref[...])
        x1_ref[...] = x + _dot(m_all.astype(BF16), wout_ref[...])


def _mixer_sample(layer, x, ckt, cvt, h0, sc, prm, nk_buf, nv_buf):
    depth, nseq = ckt.shape[0], ckt.shape[1]
    bb = SAMPLE_BB
    assert nseq % bb == 0
    wl = lambda *z: lambda i: (layer,) + z
    full2 = lambda i: (0, 0)
    cache_spec = pl.BlockSpec((None, bb, KV_WIDTH, WINDOW), lambda i: (layer, i, 0, 0))
    n_in = 16
    kern = functools.partial(_mixer_sample_kernel, nseq=nseq, bb=bb)
    return pl.pallas_call(
        kern,
        grid=(nseq // bb,),
        in_specs=[
            _smem_spec(), _smem_spec(),
            _const_spec((nseq, D_MODEL), full2),
            _const_spec((None, VEC_ROWS, D_MODEL), wl(0, 0)),
            _const_spec((1, WINDOW), full2),
            _const_spec((Q_WIDTH, Q_WIDTH), full2),
            _const_spec((None, D_MODEL, QKV_WIDTH), wl(0, 0)),
            _const_spec((None, D_MODEL, IN_WIDTH), wl(0, 0)),
            _const_spec((None, Q_WIDTH, D_MODEL), wl(0, 0)),
            _const_spec((None, LRU_HEADS, LRU_BLOCK, 2 * LRU_BLOCK), wl(0, 0, 0)),
            _const_spec((None, LRU_WIDTH, D_MODEL), wl(0, 0)),
            _const_spec((None, D_MODEL, D_MODEL), wl(0, 0)),
            cache_spec, cache_spec,
            _const_spec((None, nseq, LRU_WIDTH), wl(0, 0)),
            _const_spec((None, CONV_WIDTH - 1, nseq, LRU_WIDTH), wl(0, 0, 0)),
            pl.BlockSpec(memory_space=pl.ANY), pl.BlockSpec(memory_space=pl.ANY),
        ],
        out_specs=[
            pl.BlockSpec((nseq, D_MODEL), full2),
            pl.BlockSpec((nseq, LRU_WIDTH), full2),
            pl.BlockSpec((nseq, LRU_WIDTH), full2),
            cache_spec, cache_spec,
        ],
        out_shape=[
            jax.ShapeDtypeStruct((nseq, D_MODEL), F32),
            jax.ShapeDtypeStruct((nseq, LRU_WIDTH), F32),
            jax.ShapeDtypeStruct((nseq, LRU_WIDTH), F32),
            jax.ShapeDtypeStruct((depth, nseq, KV_WIDTH, WINDOW), F32),
            jax.ShapeDtypeStruct((depth, nseq, KV_WIDTH, WINDOW), F32),
        ],
        input_output_aliases={n_in: 3, n_in + 1: 4},
        scratch_shapes=[
            pltpu.VMEM((nseq, D_MODEL), BF16),
            pltpu.VMEM((nseq, Q_WIDTH), F32),
            pltpu.VMEM((nseq, Q_WIDTH), F32),
            pltpu.VMEM((N_HEADS * bb, bb * WINDOW), F32),
            pltpu.VMEM((nseq, KV_WIDTH), F32),
            pltpu.VMEM((nseq, KV_WIDTH), F32),
        ],
        compiler_params=pltpu.CompilerParams(
            dimension_semantics=("arbitrary",), vmem_limit_bytes=VMEM_LIMIT_BYTES),
        name=f"mixer_sample_l{layer}",
    )(prm["t5"], prm["sinks"][layer], x, prm["vecs"], prm["bucket_s"], prm["seg"], prm["w_qkv"],
      prm["w_in"], prm["w_o_attn"], prm["w_ax"], prm["w_o_lru"], prm["w_out"], ckt, cvt, h0, sc,
      nk_buf, nv_buf)


FF_CHUNKS = ((0, 1024), (1024, 2048), (2048, D_FF))


def _ffn_kernel(x_ref, p_ref, vec_ref, wg_ref, wu_ref, wd_ref, wpg_ref, wp_ref, o_ref, act_scr):
    x = x_ref[...]
    h2 = _rms(x, vec_ref[R_LN2:R_LN2 + 1, :]).astype(BF16)
    for lo, hi in FF_CHUNKS:
        g = _dot(h2, wg_ref[:, lo:hi])
        act = g * _sigmoid(g) * _dot(h2, wu_ref[:, lo:hi])
        act_scr[:, lo:hi] = act.astype(BF16)
    x = x + _dot(act_scr[...], wd_ref[...])
    h3 = _rms(x, vec_ref[R_LN3:R_LN3 + 1, :]).astype(BF16)
    gate = _sigmoid(_dot(h3, wpg_ref[...]))
    o_ref[...] = x + gate * _dot(p_ref[...].astype(BF16), wp_ref[...])


def _ffn(layer, x, p, prm, tag):
    rows = x.shape[0]
    tm = min(FFN_TM, rows)
    assert rows % tm == 0
    wl = lambda *z: lambda r: (layer,) + z
    return pl.pallas_call(
        _ffn_kernel,
        grid=(rows // tm,),
        in_specs=[
            pl.BlockSpec((tm, D_MODEL), lambda r: (r, 0)),
            pl.BlockSpec((None, tm, PLE_DIM), lambda r: (layer, r, 0)),
            _const_spec((None, VEC_ROWS, D_MODEL), wl(0, 0)),
            _const_spec((None, D_MODEL, D_FF), wl(0, 0)),
            _const_spec((None, D_MODEL, D_FF), wl(0, 0)),
            _const_spec((None, D_FF, D_MODEL), wl(0, 0)),
            _const_spec((None, D_MODEL, D_MODEL), wl(0, 0)),
            _const_spec((None, PLE_DIM, D_MODEL), wl(0, 0)),
        ],
        out_specs=pl.BlockSpec((tm, D_MODEL), lambda r: (r, 0)),
        out_shape=jax.ShapeDtypeStruct((rows, D_MODEL), F32),
        scratch_shapes=[pltpu.VMEM((tm, D_FF), BF16)],
        compiler_params=pltpu.CompilerParams(
            dimension_semantics=("arbitrary",), vmem_limit_bytes=VMEM_LIMIT_BYTES),
        name=f"ffn_{tag}_l{layer}",
    )(x, p, prm["vecs"], prm["w_gate"], prm["w_up"], prm["w_down"], prm["w_ple_gate"], prm["w_ple"])


def _t5_bucket(dist):
    n = jnp.maximum(dist, 0)
    max_exact = N_BUCKETS // 2
    nf = jnp.maximum(n, 1).astype(F32)
    large = max_exact + (jnp.log(nf / max_exact) / math.log(MAX_DISTANCE / max_exact)
                         * (N_BUCKETS - max_exact)).astype(jnp.int32)
    large = jnp.minimum(large, N_BUCKETS - 1)
    return jnp.where(n < max_exact, n, large)


def _regroup_heads(w, axis):
    shape = w.shape
    w = w.reshape(shape[:axis] + (N_KV_HEADS, GROUP, HEAD_DIM) + shape[axis + 1:])
    return jnp.swapaxes(w, axis, axis + 1).reshape(shape)


def _prepare(t5_table, ln1, w_in, q_gain, k_gain, sinks, w_o_attn, conv_w, conv_b, w_a, b_a,
             w_x, b_x, lam, w_o_lru, w_out, ln2, w_gate, w_up, w_down, ln3, w_ple, w_ple_gate):
    depth = w_in.shape[0]
    w_qkv = jnp.concatenate(
        [_regroup_heads(w_in[:, :, 0:Q_WIDTH], 2), w_in[:, :, Q_WIDTH:QKV_WIDTH]], axis=2).astype(BF16)
    vecs = jnp.zeros((depth, VEC_ROWS, D_MODEL), F32)
    vecs = vecs.at[:, R_LN1].set(ln1)
    vecs = vecs.at[:, R_CW0:R_CW0 + CONV_WIDTH].set(conv_w)
    vecs = vecs.at[:, R_CB].set(conv_b)
    vecs = vecs.at[:, R_BA].set(b_a)
    vecs = vecs.at[:, R_BX].set(b_x)
    vecs = vecs.at[:, R_LAM].set(lam)
    vecs = vecs.at[:, R_QG, 0:Q_WIDTH].set(jnp.tile(q_gain, (1, N_HEADS)))
    vecs = vecs.at[:, R_KG, 0:KV_WIDTH].set(jnp.tile(k_gain, (1, N_KV_HEADS)))
    vecs = vecs.at[:, R_LN2].set(ln2)
    vecs = vecs.at[:, R_LN3].set(ln3)
    head_id = np.arange(Q_WIDTH) // HEAD_DIM
    seg = jnp.asarray(head_id[:, None] == head_id[None, :], BF16)
    dist_p = (WINDOW + jnp.arange(WINDOW))[:, None] - jnp.arange(2 * WINDOW)[None, :]
    bucket_p = jnp.where((dist_p >= 0) & (dist_p < WINDOW), _t5_bucket(dist_p), -1).astype(jnp.int32)
    dist_s = (WINDOW - jnp.arange(WINDOW))[None, :]
    bucket_s = jnp.where((dist_s >= 0) & (dist_s < WINDOW), _t5_bucket(dist_s), -1).astype(jnp.int32)
    return {
        "t5": t5_table.reshape(-1),
        "sinks": sinks,
        "vecs": vecs,
        "seg": seg,
        "bucket_p": bucket_p,
        "bucket_s": bucket_s,
        "w_qkv": w_qkv,
        "w_in": w_in.astype(BF16),
        "w_o_attn": _regroup_heads(w_o_attn, 1).astype(BF16),
        "w_ax": jnp.concatenate([w_a, w_x], axis=-1).astype(BF16),
        "w_o_lru": w_o_lru.astype(BF16),
        "w_out": w_out.astype(BF16),
        "w_gate": w_gate.astype(BF16),
        "w_up": w_up.astype(BF16),
        "w_down": w_down.astype(BF16),
        "w_ple_gate": w_ple_gate.astype(BF16),
        "w_ple": w_ple.astype(BF16),
    }


def kernel(x_prompt, x_sample, cache_k_win, cache_v_win, state_lru_h, state_conv, p_prompt,
           p_sample, t5_table, ln1, w_in, q_gain, k_gain, sinks, w_o_attn, conv_w, conv_b, w_a,
           b_a, w_x, b_x, lam, w_o_lru, w_out, ln2, w_gate, w_up, w_down, ln3, w_ple, w_ple_gate):
    depth = w_in.shape[0]
    nb, seq, _ = x_prompt.shape
    nseq = x_sample.shape[0]
    assert x_sample.shape[1] == 1 and cache_k_win.shape[2] == WINDOW
    prm = _prepare(t5_table, ln1, w_in, q_gain, k_gain, sinks, w_o_attn, conv_w, conv_b, w_a, b_a,
                   w_x, b_x, lam, w_o_lru, w_out, ln2, w_gate, w_up, w_down, ln3, w_ple, w_ple_gate)
    ckt = jnp.transpose(cache_k_win, (0, 1, 3, 4, 2)).reshape(depth, nseq, KV_WIDTH, WINDOW)
    cvt = jnp.transpose(cache_v_win, (0, 1, 3, 4, 2)).reshape(depth, nseq, KV_WIDTH, WINDOW)
    sc = jnp.transpose(state_conv, (0, 2, 1, 3))
    pp = p_prompt.reshape(depth, nb * seq, PLE_DIM)
    ps = p_sample.reshape(depth, nseq, PLE_DIM)

    yp = x_prompt
    ys = x_sample.reshape(nseq, D_MODEL)
    outs = [[] for _ in range(6)]
    nk_buf = jnp.zeros((depth, nseq, KV_WIDTH, WINDOW), F32)
    nv_buf = jnp.zeros((depth, nseq, KV_WIDTH, WINDOW), F32)
    for layer in range(depth):
        x1, nk, nv, nh, nc = _mixer_prompt(layer, yp, prm)
        yp = _ffn(layer, x1.reshape(nb * seq, D_MODEL), pp, prm, "prompt").reshape(nb, seq, D_MODEL)
        outs[0].append(nk.reshape(nb, WINDOW, N_KV_HEADS, HEAD_DIM))
        outs[1].append(nv.reshape(nb, WINDOW, N_KV_HEADS, HEAD_DIM))
        outs[2].append(nh)
        outs[3].append(nc[:, SUBLANES - (CONV_WIDTH - 1):, :])

        x1s, nhs, xr, nk_buf, nv_buf = _mixer_sample(layer, ys, ckt, cvt, state_lru_h, sc, prm,
                                                     nk_buf, nv_buf)
        ys = _ffn(layer, x1s, ps, prm, "sample")
        outs[4].append(nhs)
        outs[5].append(jnp.concatenate([state_conv[layer, :, 1:], xr[:, None, :]], axis=1))

    def untranspose(buf):
        return jnp.transpose(buf.reshape(depth, nseq, N_KV_HEADS, HEAD_DIM, WINDOW), (0, 1, 4, 2, 3))

    stacked = [jnp.stack(o) for o in outs]
    return (yp, ys.reshape(nseq, 1, D_MODEL), stacked[0], stacked[1], stacked[2], stacked[3],
            untranspose(nk_buf), untranspose(nv_buf), stacked[4], stacked[5])
```

```python
import functools
import math

import numpy as np
import jax
import jax.numpy as jnp
from jax import lax
from jax.experimental import pallas as pl
from jax.experimental.pallas import tpu as pltpu

F32 = jnp.float32
BF16 = jnp.bfloat16

D_MODEL = 1024
HEAD_DIM = 64
N_HEADS = 8
N_KV_HEADS = 2
GROUP = N_HEADS // N_KV_HEADS
Q_WIDTH = N_HEADS * HEAD_DIM
KV_WIDTH = N_KV_HEADS * HEAD_DIM
WINDOW = 128
N_BUCKETS = 32
MAX_DISTANCE = 128
LRU_WIDTH = D_MODEL
LRU_HEADS = 8
LRU_BLOCK = LRU_WIDTH // LRU_HEADS
LRU_C = 8.0
CONV_WIDTH = 4
D_FF = 2816
PLE_DIM = 256
PAST_LEN = 8192
EPS = 1e-6
NEG_INF = -1e30
TINY = 1e-30
LOG2E = math.log2(math.e)

SUBLANES = 8
LANES = 128
VMEM_LIMIT_BYTES = 56 * 1024 * 1024

Q0 = 0
K0 = Q0 + Q_WIDTH
V0 = K0 + KV_WIDTH
QKV_WIDTH = V0 + KV_WIDTH
XR0 = QKV_WIDTH
XG0 = XR0 + LRU_WIDTH
GA0 = XG0 + LRU_WIDTH
GL0 = GA0 + D_MODEL
IN_WIDTH = GL0 + D_MODEL

R_LN1, R_CW0, R_CB, R_BA, R_BX, R_LAM, R_QG, R_KG, R_LN2, R_LN3 = 0, 1, 5, 6, 7, 8, 9, 10, 11, 12
VEC_ROWS = 16

PROMPT_TT = 64
FFN_TM = 1024
SAMPLE_BB = 16
PROJ_CHUNK = 512


def _dot(a, b):
    return jnp.dot(a, b, preferred_element_type=F32)


def _dot_nt(a, b):
    return lax.dot_general(a, b, (((1,), (1,)), ((), ())), preferred_element_type=F32)


def _rms(x, g):
    ms = jnp.mean(x * x, axis=-1, keepdims=True)
    return x * lax.rsqrt(ms + EPS) * g


def _seg_rms(x, seg, g):
    x2 = x * x
    ssq = _dot(x2.astype(BF16), seg)
    return x * lax.rsqrt(ssq * (1.0 / HEAD_DIM) + EPS) * g


def _sigmoid(x):
    return 0.5 * jnp.tanh(0.5 * x) + 0.5


def _softplus(z):
    return jnp.maximum(z, 0.0) + jnp.log1p(jnp.exp(-jnp.abs(z)))


def _gather_bias(bucket, t5_ref, head):
    acc = jnp.full(bucket.shape, NEG_INF, F32)
    for bkt in range(N_BUCKETS):
        acc = jnp.where(bucket == bkt, t5_ref[bkt * N_HEADS + head], acc)
    return acc


def _lru_gates(xch, wax, b_a, b_x, c_row):
    gts = _dot(xch.astype(BF16), wax)
    r = _sigmoid(gts[:, :LRU_BLOCK] + b_a)
    ig = _sigmoid(gts[:, LRU_BLOCK:] + b_x)
    log_a = c_row * r
    a = jnp.exp(log_a)
    y = 1.0 - a * a
    return a, y * lax.rsqrt(jnp.maximum(y, TINY)) * (ig * xch)


def _mixer_prompt_kernel(t5_ref, sinks_ref, x_ref, vec_ref, bucket_ref, seg_ref, wqkv_ref, win_ref,
                         woa_ref, wax_ref, wol_ref, wout_ref,
                         x1_ref, nk_ref, nv_ref, nh_ref, nc_ref,
                         bias_scr, bias_t, kband, vband, cs, a_scr, b_scr, h_scr, hst, y_scr, o_scr,
                         act_scr, q_scr, *, nb, tt):
    tm = nb * tt
    t = pl.program_id(0)

    @pl.when(t == 0)
    def _init():
        bucket = bucket_ref[...]
        for hh in range(N_HEADS):
            bias_scr[hh] = _gather_bias(bucket, t5_ref, hh) * LOG2E
        kband[...] = jnp.zeros((nb, 2 * WINDOW, KV_WIDTH), F32)
        vband[...] = jnp.zeros((nb, 2 * WINDOW, KV_WIDTH), F32)
        cs[:, 0:SUBLANES, :] = jnp.zeros((nb, SUBLANES, LRU_WIDTH), F32)
        hst[...] = jnp.zeros((nb, LRU_WIDTH), F32)

    x = x_ref[...].reshape(tm, D_MODEL)
    h = _rms(x, vec_ref[R_LN1:R_LN1 + 1, :]).astype(BF16)

    cs[:, SUBLANES:SUBLANES + tt, :] = _dot(h, win_ref[:, XR0:XG0]).reshape(nb, tt, LRU_WIDTH)
    c_row = -LRU_C * _softplus(-vec_ref[R_LAM:R_LAM + 1, :])
    n_chunks = (IN_WIDTH - XG0) // PROJ_CHUNK
    first_chunk_head = LRU_HEADS - n_chunks
    qkv = None
    for hd in range(LRU_HEADS):
        lc = slice(hd * LRU_BLOCK, (hd + 1) * LRU_BLOCK)
        xc = vec_ref[R_CB:R_CB + 1, lc]
        for j in range(CONV_WIDTH):
            off = SUBLANES - (CONV_WIDTH - 1) + j
            xc = xc + vec_ref[R_CW0 + j:R_CW0 + j + 1, lc] * cs[:, off:off + tt, lc]
        a, bb = _lru_gates(xc.reshape(tm, LRU_BLOCK), wax_ref[hd], vec_ref[R_BA:R_BA + 1, lc],
                           vec_ref[R_BX:R_BX + 1, lc], c_row[:, lc])
        for b in range(nb):
            rows = pl.ds(b, tt, stride=nb)
            a_scr[hd, rows, :] = a[b * tt:(b + 1) * tt]
            b_scr[hd, rows, :] = bb[b * tt:(b + 1) * tt]
        if hd == 0:
            qkv = _dot(h, wqkv_ref[...])
        elif hd == 1:
            seg = seg_ref[...]
            q_scr[...] = _seg_rms(qkv[:, Q0:K0], seg,
                                  vec_ref[R_QG:R_QG + 1, 0:Q_WIDTH] * (LOG2E * HEAD_DIM ** -0.5))
            kn = _seg_rms(qkv[:, K0:V0], seg[0:KV_WIDTH, 0:KV_WIDTH],
                          vec_ref[R_KG:R_KG + 1, 0:KV_WIDTH])
            kband[:, WINDOW:WINDOW + tt, :] = kn.reshape(nb, tt, KV_WIDTH)
            vband[:, WINDOW:WINDOW + tt, :] = qkv[:, V0:QKV_WIDTH].reshape(nb, tt, KV_WIDTH)
            nk_ref[...] = kband[:, tt:tt + WINDOW, :]
            nv_ref[...] = vband[:, tt:tt + WINDOW, :]
        if hd >= first_chunk_head:
            c0 = XG0 + (hd - first_chunk_head) * PROJ_CHUNK
            pr = _dot(h, win_ref[:, c0:c0 + PROJ_CHUNK])
            act_scr[:, c0 - XG0:c0 - XG0 + PROJ_CHUNK] = jax.nn.gelu(pr) if c0 < GA0 else _sigmoid(pr)
    tail = cs[:, tt:tt + SUBLANES, :]
    nc_ref[...] = tail
    cs[:, 0:SUBLANES, :] = tail

    @pl.when(t * tt <= WINDOW)
    def _mask_bias():
        col = lax.broadcasted_iota(jnp.int32, (1, 2 * WINDOW), 1)
        colmask = jnp.where(col >= WINDOW - t * tt, 0.0, NEG_INF)
        for hh in range(N_HEADS):
            bias_t[hh] = bias_scr[hh, 0:tt, :] + colmask

    lane = lax.broadcasted_iota(jnp.int32, (1, LANES), 1)
    low = lane < HEAD_DIM
    for b in range(nb):
        kb = kband[b].astype(BF16)
        vb = vband[b].astype(BF16)
        o_kv = []
        for kh in range(N_KV_HEADS):
            keep = low if kh == 0 else jnp.logical_not(low)
            q4 = jnp.concatenate(
                [jnp.where(keep, q_scr[b * tt:(b + 1) * tt, g * LANES:(g + 1) * LANES], 0.0)
                 for g in range(GROUP)], axis=0).astype(BF16)
            s = _dot_nt(q4, kb)
            ps, invs = [], []
            for g in range(GROUP):
                hh = kh * GROUP + g
                sg = s[g * tt:(g + 1) * tt] + bias_t[hh]
                sink = sinks_ref[hh] * LOG2E
                m = jnp.maximum(jnp.max(sg, axis=-1, keepdims=True), sink)
                e = jnp.exp2(sg - m)
                invs.append(1.0 / (jnp.sum(e, axis=-1, keepdims=True) + jnp.exp2(sink - m)))
                ps.append(e.astype(BF16))
            o4 = _dot(jnp.concatenate(ps, axis=0), vb)
            o_kv.append([o4[g * tt:(g + 1) * tt] * invs[g] for g in range(GROUP)])
        for g in range(GROUP):
            og = jnp.where(low, o_kv[0][g], o_kv[1][g])
            o_scr[b * tt:(b + 1) * tt, g * LANES:(g + 1) * LANES] = og.astype(BF16)

    kband[:, 0:WINDOW, :] = kband[:, tt:tt + WINDOW, :]
    vband[:, 0:WINDOW, :] = vband[:, tt:tt + WINDOW, :]

    m_att = act_scr[:, GA0 - XG0:GL0 - XG0] * _dot(o_scr[...], woa_ref[...])

    def scan_step(ts, hs):
        out = []
        for hd in range(LRU_HEADS):
            rows = pl.ds(pl.multiple_of(ts * nb, nb), nb)
            hn = a_scr[hd, rows, :] * hs[hd] + b_scr[hd, rows, :]
            h_scr[hd, rows, :] = hn
            out.append(hn)
        return tuple(out)

    hs0 = tuple(hst[:, hd * LRU_BLOCK:(hd + 1) * LRU_BLOCK] for hd in range(LRU_HEADS))
    hs = lax.fori_loop(0, tt, scan_step, hs0, unroll=8)
    hfin = jnp.concatenate(hs, axis=1)
    hst[...] = hfin
    nh_ref[...] = hfin

    for hd in range(LRU_HEADS):
        lc = slice(hd * LRU_BLOCK, (hd + 1) * LRU_BLOCK)
        hseq = jnp.concatenate([h_scr[hd, pl.ds(b, tt, stride=nb), :] for b in range(nb)], axis=0)
        y_scr[:, lc] = (hseq * act_scr[:, lc]).astype(BF16)

    m_all = m_att + act_scr[:, GL0 - XG0:IN_WIDTH - XG0] * _dot(y_scr[...], wol_ref[...])
    x1 = x + _dot(m_all.astype(BF16), wout_ref[...])
    x1_ref[...] = x1.reshape(nb, tt, D_MODEL)


def _const_spec(shape, index):
    return pl.BlockSpec(shape, index, pipeline_mode=pl.Buffered(1))


def _smem_spec():
    return pl.BlockSpec(memory_space=pltpu.SMEM)


def _mixer_prompt(layer, x, prm):
    nb, seq, _ = x.shape
    tt = PROMPT_TT
    assert nb == SUBLANES and seq % tt == 0 and WINDOW % tt == 0
    tm = nb * tt
    wl = lambda *z: lambda t: (layer,) + z
    kern = functools.partial(_mixer_prompt_kernel, nb=nb, tt=tt)
    return pl.pallas_call(
        kern,
        grid=(seq // tt,),
        in_specs=[
            _smem_spec(), _smem_spec(),
            pl.BlockSpec((nb, tt, D_MODEL), lambda t: (0, t, 0)),
            _const_spec((None, VEC_ROWS, D_MODEL), wl(0, 0)),
            _const_spec((WINDOW, 2 * WINDOW), lambda t: (0, 0)),
            _const_spec((Q_WIDTH, Q_WIDTH), lambda t: (0, 0)),
            _const_spec((None, D_MODEL, QKV_WIDTH), wl(0, 0)),
            _const_spec((None, D_MODEL, IN_WIDTH), wl(0, 0)),
            _const_spec((None, Q_WIDTH, D_MODEL), wl(0, 0)),
            _const_spec((None, LRU_HEADS, LRU_BLOCK, 2 * LRU_BLOCK), wl(0, 0, 0)),
            _const_spec((None, LRU_WIDTH, D_MODEL), wl(0, 0)),
            _const_spec((None, D_MODEL, D_MODEL), wl(0, 0)),
        ],
        out_specs=[
            pl.BlockSpec((nb, tt, D_MODEL), lambda t: (0, t, 0)),
            pl.BlockSpec((nb, WINDOW, KV_WIDTH), lambda t: (0, 0, 0)),
            pl.BlockSpec((nb, WINDOW, KV_WIDTH), lambda t: (0, 0, 0)),
            pl.BlockSpec((nb, LRU_WIDTH), lambda t: (0, 0)),
            pl.BlockSpec((nb, SUBLANES, LRU_WIDTH), lambda t: (0, 0, 0)),
        ],
        out_shape=[
            jax.ShapeDtypeStruct((nb, seq, D_MODEL), F32),
            jax.ShapeDtypeStruct((nb, WINDOW, KV_WIDTH), F32),
            jax.ShapeDtypeStruct((nb, WINDOW, KV_WIDTH), F32),
            jax.ShapeDtypeStruct((nb, LRU_WIDTH), F32),
            jax.ShapeDtypeStruct((nb, SUBLANES, LRU_WIDTH), F32),
        ],
        scratch_shapes=[
            pltpu.VMEM((N_HEADS, WINDOW, 2 * WINDOW), F32),
            pltpu.VMEM((N_HEADS, tt, 2 * WINDOW), F32),
            pltpu.VMEM((nb, 2 * WINDOW, KV_WIDTH), F32),
            pltpu.VMEM((nb, 2 * WINDOW, KV_WIDTH), F32),
            pltpu.VMEM((nb, tt + SUBLANES, LRU_WIDTH), F32),
            pltpu.VMEM((LRU_HEADS, tm, LRU_BLOCK), F32),
            pltpu.VMEM((LRU_HEADS, tm, LRU_BLOCK), F32),
            pltpu.VMEM((LRU_HEADS, tm, LRU_BLOCK), F32),
            pltpu.VMEM((nb, LRU_WIDTH), F32),
            pltpu.VMEM((tm, LRU_WIDTH), BF16),
            pltpu.VMEM((tm, Q_WIDTH), BF16),
            pltpu.VMEM((tm, IN_WIDTH - XG0), F32),
            pltpu.VMEM((tm, Q_WIDTH), F32),
        ],
        compiler_params=pltpu.CompilerParams(
            dimension_semantics=("arbitrary",), vmem_limit_bytes=VMEM_LIMIT_BYTES),
        name=f"mixer_prompt_l{layer}",
    )(prm["t5"], prm["sinks"][layer], x, prm["vecs"], prm["bucket_p"], prm["seg"], prm["w_qkv"],
      prm["w_in"], prm["w_o_attn"], prm["w_ax"], prm["w_o_lru"], prm["w_out"])


def _mixer_sample_kernel(t5_ref, sinks_ref, x_ref, vec_ref, bucket_ref, seg_ref, wqkv_ref, win_ref,
                         woa_ref, wax_ref, wol_ref, wout_ref, ck_ref, cv_ref, h0_ref, sc_ref,
                         nk_all_ref, nv_all_ref,
                         x1_ref, nh_ref, xr_ref, nk_ref, nv_ref,
                         h_scr, q_scr, o_scr, mb_scr, kn_scr, vn_scr,
                         *, nseq, bb):
    del nk_all_ref, nv_all_ref
    i = pl.program_id(0)
    nkeys = bb * WINDOW

    @pl.when(i == 0)
    def _project():
        x = x_ref[...]
        h = _rms(x, vec_ref[R_LN1:R_LN1 + 1, :]).astype(BF16)
        h_scr[...] = h
        qkv = _dot(h, wqkv_ref[...])
        seg = seg_ref[...]
        q_scr[...] = _seg_rms(qkv[:, Q0:K0], seg,
                              vec_ref[R_QG:R_QG + 1, 0:Q_WIDTH] * (HEAD_DIM ** -0.5))
        kn_scr[...] = _seg_rms(qkv[:, K0:V0], seg[0:KV_WIDTH, 0:KV_WIDTH],
                               vec_ref[R_KG:R_KG + 1, 0:KV_WIDTH])
        vn_scr[...] = qkv[:, V0:QKV_WIDTH]
        bucket = bucket_ref[...]
        rowb = lax.broadcasted_iota(jnp.int32, (bb, nkeys), 0)
        colb = lax.broadcasted_iota(jnp.int32, (bb, nkeys), 1) // WINDOW
        for hh in range(N_HEADS):
            brow = _gather_bias(bucket, t5_ref, hh)
            brow = jnp.concatenate([brow] * bb, axis=1)
            mb_scr[hh * bb:(hh + 1) * bb, :] = jnp.where(rowb == colb, brow, NEG_INF)

    rows = pl.ds(pl.multiple_of(i * bb, bb), bb)
    lane = lax.broadcasted_iota(jnp.int32, (1, LANES), 1)
    low = lane < HEAD_DIM
    qblk = q_scr[rows, :]
    qz = []
    for kh in range(N_KV_HEADS):
        keep = low if kh == 0 else jnp.logical_not(low)
        for g in range(GROUP):
            qz.append(jnp.where(keep, qblk[:, g * LANES:(g + 1) * LANES], 0.0))
    qz = jnp.concatenate(qz, axis=0).astype(BF16)
    kt = jnp.concatenate([ck_ref[b] for b in range(bb)], axis=1).astype(BF16)
    vt = jnp.concatenate([cv_ref[b] for b in range(bb)], axis=1).astype(BF16)
    s = _dot(qz, kt) + mb_scr[...]
    kn_blk = kn_scr[rows, :]
    vn_blk = vn_scr[rows, :]
    knew = jnp.concatenate([kn_blk.astype(BF16).astype(F32)] * N_HEADS, axis=0)
    vnew = jnp.concatenate([vn_blk.astype(BF16).astype(F32)] * N_HEADS, axis=0)
    self_bias = jnp.concatenate(
        [jnp.full((bb, 1), t5_ref[hh], F32) for hh in range(N_HEADS)], axis=0)
    sink = jnp.concatenate(
        [jnp.full((bb, 1), sinks_ref[hh], F32) for hh in range(N_HEADS)], axis=0)
    s_self = jnp.sum(qz.astype(F32) * knew, axis=-1, keepdims=True) + self_bias
    m = jnp.maximum(jnp.maximum(jnp.max(s, axis=-1, keepdims=True), s_self), sink)
    e = jnp.exp(s - m)
    e_self = jnp.exp(s_self - m)
    den = jnp.sum(e, axis=-1, keepdims=True) + e_self + jnp.exp(sink - m)
    inv = 1.0 / den
    o = (_dot_nt((e * inv).astype(BF16), vt)
         + (e_self * inv).astype(BF16).astype(F32) * vnew)
    half = GROUP * bb
    for g in range(GROUP):
        og = jnp.where(low, o[g * bb:(g + 1) * bb], o[half + g * bb:half + (g + 1) * bb])
        o_scr[rows, g * LANES:(g + 1) * LANES] = og

    knt = kn_blk.T
    vnt = vn_blk.T
    last = lax.broadcasted_iota(jnp.int32, (1, WINDOW), 1) == WINDOW - 1
    for b in range(bb):
        nk_ref[b] = jnp.where(last, knt[:, b:b + 1], pltpu.roll(ck_ref[b], WINDOW - 1, 1))
        nv_ref[b] = jnp.where(last, vnt[:, b:b + 1], pltpu.roll(cv_ref[b], WINDOW - 1, 1))

    @pl.when(i == pl.num_programs(0) - 1)
    def _finish():
        x = x_ref[...]
        h = h_scr[...]
        m_att = _sigmoid(_dot(h, win_ref[:, GA0:GL0])) * _dot(o_scr[...].astype(BF16), woa_ref[...])
        xr = _dot(h, win_ref[:, XR0:XG0])
        xr_ref[...] = xr
        xc = vec_ref[R_CB:R_CB + 1, :] + vec_ref[R_CW0 + CONV_WIDTH - 1:R_CW0 + CONV_WIDTH, :] * xr
        for j in range(CONV_WIDTH - 1):
            xc = xc + vec_ref[R_CW0 + j:R_CW0 + j + 1, :] * sc_ref[j]
        c_row = -LRU_C * _softplus(-vec_ref[R_LAM:R_LAM + 1, :])
        hn = []
        for hd in range(LRU_HEADS):
            lc = slice(hd * LRU_BLOCK, (hd + 1) * LRU_BLOCK)
            a, bb_ = _lru_gates(xc[:, lc], wax_ref[hd], vec_ref[R_BA:R_BA + 1, lc],
                                vec_ref[R_BX:R_BX + 1, lc], c_row[:, lc])
            hn.append(a * h0_ref[:, lc] + bb_)
        hn = jnp.concatenate(hn, axis=1)
        nh_ref[...] = hn
        y = (hn * jax.nn.gelu(_dot(h, win_ref[:, XG0:GA0]))).astype(BF16)
        m_all = m_att + _sigmoid(_dot(h, win_ref[:, GL0:IN_WIDTH])) * _dot(y, wol_ref[...])
        x1_ref[...] = x + _dot(m_all.astype(BF16), wout_ref[...])


def _mixer_sample(layer, x, ckt, cvt, h0, sc, prm, nk_buf, nv_buf):
    depth, nseq = ckt.shape[0], ckt.shape[1]
    bb = SAMPLE_BB
    assert nseq % bb == 0
    wl = lambda *z: lambda i: (layer,) + z
    full2 = lambda i: (0, 0)
    cache_spec = pl.BlockSpec((None, bb, KV_WIDTH, WINDOW), lambda i: (layer, i, 0, 0))
    n_in = 16
    kern = functools.partial(_mixer_sample_kernel, nseq=nseq, bb=bb)
    return pl.pallas_call(
        kern,
        grid=(nseq // bb,),
        in_specs=[
            _smem_spec(), _smem_spec(),
            _const_spec((nseq, D_MODEL), full2),
            _const_spec((None, VEC_ROWS, D_MODEL), wl(0, 0)),
            _const_spec((1, WINDOW), full2),
            _const_spec((Q_WIDTH, Q_WIDTH), full2),
            _const_spec((None, D_MODEL, QKV_WIDTH), wl(0, 0)),
            _const_spec((None, D_MODEL, IN_WIDTH), wl(0, 0)),
            _const_spec((None, Q_WIDTH, D_MODEL), wl(0, 0)),
            _const_spec((None, LRU_HEADS, LRU_BLOCK, 2 * LRU_BLOCK), wl(0, 0, 0)),
            _const_spec((None, LRU_WIDTH, D_MODEL), wl(0, 0)),
            _const_spec((None, D_MODEL, D_MODEL), wl(0, 0)),
            cache_spec, cache_spec,
            _const_spec((None, nseq, LRU_WIDTH), wl(0, 0)),
            _const_spec((None, CONV_WIDTH - 1, nseq, LRU_WIDTH), wl(0, 0, 0)),
            pl.BlockSpec(memory_space=pl.ANY), pl.BlockSpec(memory_space=pl.ANY),
        ],
        out_specs=[
            pl.BlockSpec((nseq, D_MODEL), full2),
            pl.BlockSpec((nseq, LRU_WIDTH), full2),
            pl.BlockSpec((nseq, LRU_WIDTH), full2),
            cache_spec, cache_spec,
        ],
        out_shape=[
            jax.ShapeDtypeStruct((nseq, D_MODEL), F32),
            jax.ShapeDtypeStruct((nseq, LRU_WIDTH), F32),
            jax.ShapeDtypeStruct((nseq, LRU_WIDTH), F32),
            jax.ShapeDtypeStruct((depth, nseq, KV_WIDTH, WINDOW), F32),
            jax.ShapeDtypeStruct((depth, nseq, KV_WIDTH, WINDOW), F32),
        ],
        input_output_aliases={n_in: 3, n_in + 1: 4},
        scratch_shapes=[
            pltpu.VMEM((nseq, D_MODEL), BF16),
            pltpu.VMEM((nseq, Q_WIDTH), F32),
            pltpu.VMEM((nseq, Q_WIDTH), F32),
            pltpu.VMEM((N_HEADS * bb, bb * WINDOW), F32),
            pltpu.VMEM((nseq, KV_WIDTH), F32),
            pltpu.VMEM((nseq, KV_WIDTH), F32),
        ],
        compiler_params=pltpu.CompilerParams(
            dimension_semantics=("arbitrary",), vmem_limit_bytes=VMEM_LIMIT_BYTES),
        name=f"mixer_sample_l{layer}",
    )(prm["t5"], prm["sinks"][layer], x, prm["vecs"], prm["bucket_s"], prm["seg"], prm["w_qkv"],
      prm["w_in"], prm["w_o_attn"], prm["w_ax"], prm["w_o_lru"], prm["w_out"], ckt, cvt, h0, sc,
      nk_buf, nv_buf)


FF_CHUNKS = ((0, 1024), (1024, 2048), (2048, D_FF))


def _ffn_rows(x, p, vec_ref, wg_ref, wu_ref, wd_ref, wpg_ref, wp_ref, act_scr):
    h2 = _rms(x, vec_ref[R_LN2:R_LN2 + 1, :]).astype(BF16)
    for lo, hi in FF_CHUNKS:
        g = _dot(h2, wg_ref[:, lo:hi])
        act = g * _sigmoid(g) * _dot(h2, wu_ref[:, lo:hi])
        act_scr[:, lo:hi] = act.astype(BF16)
    x = x + _dot(act_scr[...], wd_ref[...])
    h3 = _rms(x, vec_ref[R_LN3:R_LN3 + 1, :]).astype(BF16)
    gate = _sigmoid(_dot(h3, wpg_ref[...]))
    return x + gate * _dot(p.astype(BF16), wp_ref[...])


def _ffn_kernel(x_ref, p_ref, xs_ref, ps_ref, vec_ref, wg_ref, wu_ref, wd_ref, wpg_ref, wp_ref,
                o_ref, os_ref, act_scr):
    weights = (vec_ref, wg_ref, wu_ref, wd_ref, wpg_ref, wp_ref)
    o_ref[...] = _ffn_rows(x_ref[...], p_ref[...], *weights, act_scr)

    @pl.when(pl.program_id(0) == pl.num_programs(0) - 1)
    def _sample():
        ns = xs_ref.shape[0]
        os_ref[...] = _ffn_rows(xs_ref[...], ps_ref[...], *weights, act_scr.at[0:ns, :])


def _ffn(layer, x, p, xs, ps, prm):
    rows, ns = x.shape[0], xs.shape[0]
    tm = min(FFN_TM, rows)
    assert rows % tm == 0 and ns <= tm
    wl = lambda *z: lambda r: (layer,) + z
    return pl.pallas_call(
        _ffn_kernel,
        grid=(rows // tm,),
        in_specs=[
            pl.BlockSpec((tm, D_MODEL), lambda r: (r, 0)),
            pl.BlockSpec((None, tm, PLE_DIM), lambda r: (layer, r, 0)),
            _const_spec((ns, D_MODEL), lambda r: (0, 0)),
            _const_spec((None, ns, PLE_DIM), wl(0, 0)),
            _const_spec((None, VEC_ROWS, D_MODEL), wl(0, 0)),
            _const_spec((None, D_MODEL, D_FF), wl(0, 0)),
            _const_spec((None, D_MODEL, D_FF), wl(0, 0)),
            _const_spec((None, D_FF, D_MODEL), wl(0, 0)),
            _const_spec((None, D_MODEL, D_MODEL), wl(0, 0)),
            _const_spec((None, PLE_DIM, D_MODEL), wl(0, 0)),
        ],
        out_specs=[
            pl.BlockSpec((tm, D_MODEL), lambda r: (r, 0)),
            pl.BlockSpec((ns, D_MODEL), lambda r: (0, 0)),
        ],
        out_shape=[
            jax.ShapeDtypeStruct((rows, D_MODEL), F32),
            jax.ShapeDtypeStruct((ns, D_MODEL), F32),
        ],
        scratch_shapes=[pltpu.VMEM((tm, D_FF), BF16)],
        compiler_params=pltpu.CompilerParams(
            dimension_semantics=("arbitrary",), vmem_limit_bytes=VMEM_LIMIT_BYTES),
        name=f"ffn_l{layer}",
    )(x, p, xs, ps, prm["vecs"], prm["w_gate"], prm["w_up"], prm["w_down"], prm["w_ple_gate"],
      prm["w_ple"])


def _t5_bucket(dist):
    n = jnp.maximum(dist, 0)
    max_exact = N_BUCKETS // 2
    nf = jnp.maximum(n, 1).astype(F32)
    large = max_exact + (jnp.log(nf / max_exact) / math.log(MAX_DISTANCE / max_exact)
                         * (N_BUCKETS - max_exact)).astype(jnp.int32)
    large = jnp.minimum(large, N_BUCKETS - 1)
    return jnp.where(n < max_exact, n, large)


def _regroup_heads(w, axis):
    shape = w.shape
    w = w.reshape(shape[:axis] + (N_KV_HEADS, GROUP, HEAD_DIM) + shape[axis + 1:])
    return jnp.swapaxes(w, axis, axis + 1).reshape(shape)


def _prepare(t5_table, ln1, w_in, q_gain, k_gain, sinks, w_o_attn, conv_w, conv_b, w_a, b_a,
             w_x, b_x, lam, w_o_lru, w_out, ln2, w_gate, w_up, w_down, ln3, w_ple, w_ple_gate):
    depth = w_in.shape[0]
    w_qkv = jnp.concatenate(
        [_regroup_heads(w_in[:, :, 0:Q_WIDTH], 2), w_in[:, :, Q_WIDTH:QKV_WIDTH]], axis=2).astype(BF16)
    vecs = jnp.zeros((depth, VEC_ROWS, D_MODEL), F32)
    vecs = vecs.at[:, R_LN1].set(ln1)
    vecs = vecs.at[:, R_CW0:R_CW0 + CONV_WIDTH].set(conv_w)
    vecs = vecs.at[:, R_CB].set(conv_b)
    vecs = vecs.at[:, R_BA].set(b_a)
    vecs = vecs.at[:, R_BX].set(b_x)
    vecs = vecs.at[:, R_LAM].set(lam)
    vecs = vecs.at[:, R_QG, 0:Q_WIDTH].set(jnp.tile(q_gain, (1, N_HEADS)))
    vecs = vecs.at[:, R_KG, 0:KV_WIDTH].set(jnp.tile(k_gain, (1, N_KV_HEADS)))
    vecs = vecs.at[:, R_LN2].set(ln2)
    vecs = vecs.at[:, R_LN3].set(ln3)
    head_id = np.arange(Q_WIDTH) // HEAD_DIM
    seg = jnp.asarray(head_id[:, None] == head_id[None, :], BF16)
    dist_p = (WINDOW + jnp.arange(WINDOW))[:, None] - jnp.arange(2 * WINDOW)[None, :]
    bucket_p = jnp.where((dist_p >= 0) & (dist_p < WINDOW), _t5_bucket(dist_p), -1).astype(jnp.int32)
    dist_s = (WINDOW - jnp.arange(WINDOW))[None, :]
    bucket_s = jnp.where((dist_s >= 0) & (dist_s < WINDOW), _t5_bucket(dist_s), -1).astype(jnp.int32)
    return {
        "t5": t5_table.reshape(-1),
        "sinks": sinks,
        "vecs": vecs,
        "seg": seg,
        "bucket_p": bucket_p,
        "bucket_s": bucket_s,
        "w_qkv": w_qkv,
        "w_in": w_in.astype(BF16),
        "w_o_attn": _regroup_heads(w_o_attn, 1).astype(BF16),
        "w_ax": jnp.concatenate([w_a, w_x], axis=-1).astype(BF16),
        "w_o_lru": w_o_lru.astype(BF16),
        "w_out": w_out.astype(BF16),
        "w_gate": w_gate.astype(BF16),
        "w_up": w_up.astype(BF16),
        "w_down": w_down.astype(BF16),
        "w_ple_gate": w_ple_gate.astype(BF16),
        "w_ple": w_ple.astype(BF16),
    }


def kernel(x_prompt, x_sample, cache_k_win, cache_v_win, state_lru_h, state_conv, p_prompt,
           p_sample, t5_table, ln1, w_in, q_gain, k_gain, sinks, w_o_attn, conv_w, conv_b, w_a,
           b_a, w_x, b_x, lam, w_o_lru, w_out, ln2, w_gate, w_up, w_down, ln3, w_ple, w_ple_gate):
    depth = w_in.shape[0]
    nb, seq, _ = x_prompt.shape
    nseq = x_sample.shape[0]
    assert x_sample.shape[1] == 1 and cache_k_win.shape[2] == WINDOW
    prm = _prepare(t5_table, ln1, w_in, q_gain, k_gain, sinks, w_o_attn, conv_w, conv_b, w_a, b_a,
                   w_x, b_x, lam, w_o_lru, w_out, ln2, w_gate, w_up, w_down, ln3, w_ple, w_ple_gate)
    ckt = jnp.transpose(cache_k_win, (0, 1, 3, 4, 2)).reshape(depth, nseq, KV_WIDTH, WINDOW)
    cvt = jnp.transpose(cache_v_win, (0, 1, 3, 4, 2)).reshape(depth, nseq, KV_WIDTH, WINDOW)
    sc = jnp.transpose(state_conv, (0, 2, 1, 3))
    pp = p_prompt.reshape(depth, nb * seq, PLE_DIM)
    ps = p_sample.reshape(depth, nseq, PLE_DIM)

    yp = x_prompt
    ys = x_sample.reshape(nseq, D_MODEL)
    outs = [[] for _ in range(6)]
    nk_buf = jnp.zeros((depth, nseq, KV_WIDTH, WINDOW), F32)
    nv_buf = jnp.zeros((depth, nseq, KV_WIDTH, WINDOW), F32)
    for layer in range(depth):
        x1, nk, nv, nh, nc = _mixer_prompt(layer, yp, prm)
        outs[0].append(nk.reshape(nb, WINDOW, N_KV_HEADS, HEAD_DIM))
        outs[1].append(nv.reshape(nb, WINDOW, N_KV_HEADS, HEAD_DIM))
        outs[2].append(nh)
        outs[3].append(nc[:, SUBLANES - (CONV_WIDTH - 1):, :])

        x1s, nhs, xr, nk_buf, nv_buf = _mixer_sample(layer, ys, ckt, cvt, state_lru_h, sc, prm,
                                                     nk_buf, nv_buf)
        yp, ys = _ffn(layer, x1.reshape(nb * seq, D_MODEL), pp, x1s, ps, prm)
        yp = yp.reshape(nb, seq, D_MODEL)
        outs[4].append(nhs)
        outs[5].append(jnp.concatenate([state_conv[layer, :, 1:], xr[:, None, :]], axis=1))

    def untranspose(buf):
        return jnp.transpose(buf.reshape(depth, nseq, N_KV_HEADS, HEAD_DIM, WINDOW), (0, 1, 4, 2, 3))

    stacked = [jnp.stack(o) for o in outs]
    return (yp, ys.reshape(nseq, 1, D_MODEL), stacked[0], stacked[1], stacked[2], stacked[3],
            untranspose(nk_buf), untranspose(nv_buf), stacked[4], stacked[5])
```

```python
import functools
import math

import numpy as np
import jax
import jax.numpy as jnp
from jax import lax
from jax.experimental import pallas as pl
from jax.experimental.pallas import tpu as pltpu

F32 = jnp.float32
BF16 = jnp.bfloat16

D_MODEL = 1024
HEAD_DIM = 64
N_HEADS = 8
N_KV_HEADS = 2
GROUP = N_HEADS // N_KV_HEADS
Q_WIDTH = N_HEADS * HEAD_DIM
KV_WIDTH = N_KV_HEADS * HEAD_DIM
WINDOW = 128
N_BUCKETS = 32
MAX_DISTANCE = 128
LRU_WIDTH = D_MODEL
LRU_HEADS = 8
LRU_BLOCK = LRU_WIDTH // LRU_HEADS
LRU_C = 8.0
CONV_WIDTH = 4
D_FF = 2816
PLE_DIM = 256
PAST_LEN = 8192
EPS = 1e-6
NEG_INF = -1e30
TINY = 1e-30
LOG2E = math.log2(math.e)

SUBLANES = 8
LANES = 128
VMEM_LIMIT_BYTES = 56 * 1024 * 1024

Q0 = 0
K0 = Q0 + Q_WIDTH
V0 = K0 + KV_WIDTH
QKV_WIDTH = V0 + KV_WIDTH
XR0 = QKV_WIDTH
XG0 = XR0 + LRU_WIDTH
GA0 = XG0 + LRU_WIDTH
GL0 = GA0 + D_MODEL
IN_WIDTH = GL0 + D_MODEL

R_LN1, R_CW0, R_CB, R_BA, R_BX, R_LAM, R_QG, R_KG, R_LN2, R_LN3 = 0, 1, 5, 6, 7, 8, 9, 10, 11, 12
VEC_ROWS = 16

PROMPT_TT = 64
FFN_TM = 1024
SAMPLE_BB = 16
WEIGHT_CHUNK_ROWS = 128
PROJ_CHUNK = 512


def _dot(a, b):
    return jnp.dot(a, b, preferred_element_type=F32)


def _dot_nt(a, b):
    return lax.dot_general(a, b, (((1,), (1,)), ((), ())), preferred_element_type=F32)


def _rms(x, g):
    ms = jnp.mean(x * x, axis=-1, keepdims=True)
    return x * lax.rsqrt(ms + EPS) * g


def _seg_rms(x, seg, g):
    x2 = x * x
    ssq = _dot(x2.astype(BF16), seg)
    return x * lax.rsqrt(ssq * (1.0 / HEAD_DIM) + EPS) * g


def _sigmoid(x):
    return 0.5 * jnp.tanh(0.5 * x) + 0.5


def _softplus(z):
    return jnp.maximum(z, 0.0) + jnp.log1p(jnp.exp(-jnp.abs(z)))


def _gather_bias(bucket, t5_ref, head):
    acc = jnp.full(bucket.shape, NEG_INF, F32)
    for bkt in range(N_BUCKETS):
        acc = jnp.where(bucket == bkt, t5_ref[bkt * N_HEADS + head], acc)
    return acc


def _lru_gates(xch, wax, b_a, b_x, c_row):
    gts = _dot(xch.astype(BF16), wax)
    r = _sigmoid(gts[:, :LRU_BLOCK] + b_a)
    ig = _sigmoid(gts[:, LRU_BLOCK:] + b_x)
    log_a = c_row * r
    a = jnp.exp(log_a)
    y = 1.0 - a * a
    return a, y * lax.rsqrt(jnp.maximum(y, TINY)) * (ig * xch)


def _mixer_prompt_kernel(t5_ref, sinks_ref, x_ref, vec_ref, bucket_ref, seg_ref, wqkv_ref, win_ref,
                         woa_ref, wax_ref, wol_ref, wout_ref,
                         x1_ref, nk_ref, nv_ref, nh_ref, nc_ref,
                         bias_scr, bias_t, kband, vband, cs, a_scr, b_scr, h_scr, hst, y_scr, o_scr,
                         proj_scr, q_scr, *, nb, tt):
    tm = nb * tt
    t = pl.program_id(0)

    @pl.when(t == 0)
    def _init():
        bucket = bucket_ref[...]
        for hh in range(N_HEADS):
            bias_scr[hh] = _gather_bias(bucket, t5_ref, hh) * LOG2E
        kband[...] = jnp.zeros((nb, 2 * WINDOW, KV_WIDTH), F32)
        vband[...] = jnp.zeros((nb, 2 * WINDOW, KV_WIDTH), F32)
        cs[:, 0:SUBLANES, :] = jnp.zeros((nb, SUBLANES, LRU_WIDTH), F32)
        hst[...] = jnp.zeros((nb, LRU_WIDTH), F32)

    x = x_ref[...].reshape(tm, D_MODEL)
    h = _rms(x, vec_ref[R_LN1:R_LN1 + 1, :]).astype(BF16)

    cs[:, SUBLANES:SUBLANES + tt, :] = _dot(h, win_ref[:, XR0:XG0]).reshape(nb, tt, LRU_WIDTH)
    c_row = -LRU_C * _softplus(-vec_ref[R_LAM:R_LAM + 1, :])
    n_chunks = (IN_WIDTH - XG0) // PROJ_CHUNK
    qkv = None
    for hd in range(LRU_HEADS):
        lc = slice(hd * LRU_BLOCK, (hd + 1) * LRU_BLOCK)
        xc = vec_ref[R_CB:R_CB + 1, lc]
        for j in range(CONV_WIDTH):
            off = SUBLANES - (CONV_WIDTH - 1) + j
            xc = xc + vec_ref[R_CW0 + j:R_CW0 + j + 1, lc] * cs[:, off:off + tt, lc]
        a, bb = _lru_gates(xc.reshape(tm, LRU_BLOCK), wax_ref[hd], vec_ref[R_BA:R_BA + 1, lc],
                           vec_ref[R_BX:R_BX + 1, lc], c_row[:, lc])
        for b in range(nb):
            rows = pl.ds(b, tt, stride=nb)
            a_scr[hd, rows, :] = a[b * tt:(b + 1) * tt]
            b_scr[hd, rows, :] = bb[b * tt:(b + 1) * tt]
        if hd < n_chunks:
            pc = slice(hd * PROJ_CHUNK, (hd + 1) * PROJ_CHUNK)
            proj_scr[:, pc] = _dot(h, win_ref[:, XG0 + hd * PROJ_CHUNK:XG0 + (hd + 1) * PROJ_CHUNK])
        elif hd == n_chunks:
            qkv = _dot(h, wqkv_ref[...])
    tail = cs[:, tt:tt + SUBLANES, :]
    nc_ref[...] = tail
    cs[:, 0:SUBLANES, :] = tail

    seg = seg_ref[...]
    q_scr[...] = _seg_rms(qkv[:, Q0:K0], seg,
                          vec_ref[R_QG:R_QG + 1, 0:Q_WIDTH] * (LOG2E * HEAD_DIM ** -0.5))
    kn = _seg_rms(qkv[:, K0:V0], seg[0:KV_WIDTH, 0:KV_WIDTH], vec_ref[R_KG:R_KG + 1, 0:KV_WIDTH])
    kband[:, WINDOW:WINDOW + tt, :] = kn.reshape(nb, tt, KV_WIDTH)
    vband[:, WINDOW:WINDOW + tt, :] = qkv[:, V0:QKV_WIDTH].reshape(nb, tt, KV_WIDTH)
    nk_ref[...] = kband[:, tt:tt + WINDOW, :]
    nv_ref[...] = vband[:, tt:tt + WINDOW, :]

    @pl.when(t * tt <= WINDOW)
    def _mask_bias():
        col = lax.broadcasted_iota(jnp.int32, (1, 2 * WINDOW), 1)
        colmask = jnp.where(col >= WINDOW - t * tt, 0.0, NEG_INF)
        for hh in range(N_HEADS):
            bias_t[hh] = bias_scr[hh, 0:tt, :] + colmask

    lane = lax.broadcasted_iota(jnp.int32, (1, LANES), 1)
    low = lane < HEAD_DIM
    for b in range(nb):
        kb = kband[b].astype(BF16)
        vb = vband[b].astype(BF16)
        o_kv = []
        for kh in range(N_KV_HEADS):
            keep = low if kh == 0 else jnp.logical_not(low)
            q4 = jnp.concatenate(
                [jnp.where(keep, q_scr[b * tt:(b + 1) * tt, g * LANES:(g + 1) * LANES], 0.0)
                 for g in range(GROUP)], axis=0).astype(BF16)
            s = _dot_nt(q4, kb)
            ps, invs = [], []
            for g in range(GROUP):
                hh = kh * GROUP + g
                sg = s[g * tt:(g + 1) * tt] + bias_t[hh]
                sink = sinks_ref[hh] * LOG2E
                m = jnp.maximum(jnp.max(sg, axis=-1, keepdims=True), sink)
                e = jnp.exp2(sg - m)
                invs.append(1.0 / (jnp.sum(e, axis=-1, keepdims=True) + jnp.exp2(sink - m)))
                ps.append(e.astype(BF16))
            o4 = _dot(jnp.concatenate(ps, axis=0), vb)
            o_kv.append([o4[g * tt:(g + 1) * tt] * invs[g] for g in range(GROUP)])
        for g in range(GROUP):
            og = jnp.where(low, o_kv[0][g], o_kv[1][g])
            o_scr[b * tt:(b + 1) * tt, g * LANES:(g + 1) * LANES] = og.astype(BF16)

    kband[:, 0:WINDOW, :] = kband[:, tt:tt + WINDOW, :]
    vband[:, 0:WINDOW, :] = vband[:, tt:tt + WINDOW, :]

    m_att = _sigmoid(proj_scr[:, GA0 - XG0:GL0 - XG0]) * _dot(o_scr[...], woa_ref[...])

    def scan_step(ts, hs):
        out = []
        for hd in range(LRU_HEADS):
            rows = pl.ds(pl.multiple_of(ts * nb, nb), nb)
            hn = a_scr[hd, rows, :] * hs[hd] + b_scr[hd, rows, :]
            h_scr[hd, rows, :] = hn
            out.append(hn)
        return tuple(out)

    hs0 = tuple(hst[:, hd * LRU_BLOCK:(hd + 1) * LRU_BLOCK] for hd in range(LRU_HEADS))
    hs = lax.fori_loop(0, tt, scan_step, hs0, unroll=8)
    hfin = jnp.concatenate(hs, axis=1)
    hst[...] = hfin
    nh_ref[...] = hfin

    for hd in range(LRU_HEADS):
        lc = slice(hd * LRU_BLOCK, (hd + 1) * LRU_BLOCK)
        hseq = jnp.concatenate([h_scr[hd, pl.ds(b, tt, stride=nb), :] for b in range(nb)], axis=0)
        y_scr[:, lc] = (hseq * jax.nn.gelu(proj_scr[:, lc])).astype(BF16)

    m_all = m_att + _sigmoid(proj_scr[:, GL0 - XG0:IN_WIDTH - XG0]) * _dot(y_scr[...], wol_ref[...])
    x1 = x + _dot(m_all.astype(BF16), wout_ref[...])
    x1_ref[...] = x1.reshape(nb, tt, D_MODEL)


def _const_spec(shape, index):
    return pl.BlockSpec(shape, index, pipeline_mode=pl.Buffered(1))


def _smem_spec():
    return pl.BlockSpec(memory_space=pltpu.SMEM)


def _mixer_prompt(layer, x, prm):
    nb, seq, _ = x.shape
    tt = PROMPT_TT
    assert nb == SUBLANES and seq % tt == 0 and WINDOW % tt == 0
    tm = nb * tt
    wl = lambda *z: lambda t: (layer,) + z
    kern = functools.partial(_mixer_prompt_kernel, nb=nb, tt=tt)
    return pl.pallas_call(
        kern,
        grid=(seq // tt,),
        in_specs=[
            _smem_spec(), _smem_spec(),
            pl.BlockSpec((nb, tt, D_MODEL), lambda t: (0, t, 0)),
            _const_spec((None, VEC_ROWS, D_MODEL), wl(0, 0)),
            _const_spec((WINDOW, 2 * WINDOW), lambda t: (0, 0)),
            _const_spec((Q_WIDTH, Q_WIDTH), lambda t: (0, 0)),
            _const_spec((None, D_MODEL, QKV_WIDTH), wl(0, 0)),
            _const_spec((None, D_MODEL, IN_WIDTH), wl(0, 0)),
            _const_spec((None, Q_WIDTH, D_MODEL), wl(0, 0)),
            _const_spec((None, LRU_HEADS, LRU_BLOCK, 2 * LRU_BLOCK), wl(0, 0, 0)),
            _const_spec((None, LRU_WIDTH, D_MODEL), wl(0, 0)),
            _const_spec((None, D_MODEL, D_MODEL), wl(0, 0)),
        ],
        out_specs=[
            pl.BlockSpec((nb, tt, D_MODEL), lambda t: (0, t, 0)),
            pl.BlockSpec((nb, WINDOW, KV_WIDTH), lambda t: (0, 0, 0)),
            pl.BlockSpec((nb, WINDOW, KV_WIDTH), lambda t: (0, 0, 0)),
            pl.BlockSpec((nb, LRU_WIDTH), lambda t: (0, 0)),
            pl.BlockSpec((nb, SUBLANES, LRU_WIDTH), lambda t: (0, 0, 0)),
        ],
        out_shape=[
            jax.ShapeDtypeStruct((nb, seq, D_MODEL), F32),
            jax.ShapeDtypeStruct((nb, WINDOW, KV_WIDTH), F32),
            jax.ShapeDtypeStruct((nb, WINDOW, KV_WIDTH), F32),
            jax.ShapeDtypeStruct((nb, LRU_WIDTH), F32),
            jax.ShapeDtypeStruct((nb, SUBLANES, LRU_WIDTH), F32),
        ],
        scratch_shapes=[
            pltpu.VMEM((N_HEADS, WINDOW, 2 * WINDOW), F32),
            pltpu.VMEM((N_HEADS, tt, 2 * WINDOW), F32),
            pltpu.VMEM((nb, 2 * WINDOW, KV_WIDTH), F32),
            pltpu.VMEM((nb, 2 * WINDOW, KV_WIDTH), F32),
            pltpu.VMEM((nb, tt + SUBLANES, LRU_WIDTH), F32),
            pltpu.VMEM((LRU_HEADS, tm, LRU_BLOCK), F32),
            pltpu.VMEM((LRU_HEADS, tm, LRU_BLOCK), F32),
            pltpu.VMEM((LRU_HEADS, tm, LRU_BLOCK), F32),
            pltpu.VMEM((nb, LRU_WIDTH), F32),
            pltpu.VMEM((tm, LRU_WIDTH), BF16),
            pltpu.VMEM((tm, Q_WIDTH), BF16),
            pltpu.VMEM((tm, IN_WIDTH - XG0), F32),
            pltpu.VMEM((tm, Q_WIDTH), F32),
        ],
        compiler_params=pltpu.CompilerParams(
            dimension_semantics=("arbitrary",), vmem_limit_bytes=VMEM_LIMIT_BYTES),
        name=f"mixer_prompt_l{layer}",
    )(prm["t5"], prm["sinks"][layer], x, prm["vecs"], prm["bucket_p"], prm["seg"], prm["w_qkv"],
      prm["w_in"], prm["w_o_attn"], prm["w_ax"], prm["w_o_lru"], prm["w_out"])


def _mixer_sample_kernel(t5_ref, sinks_ref, x_ref, vec_ref, bucket_ref, seg_ref, wqkv_ref, win_ref,
                         woa_ref, wax_ref, wol_ref, wout_ref, ck_ref, cv_ref, h0_ref, sc_ref,
                         nk_all_ref, nv_all_ref,
                         x1_ref, nh_ref, xr_ref, nk_ref, nv_ref,
                         h_scr, q_scr, o_scr, mb_scr, kn_scr, vn_scr,
                         *, nseq, bb):
    del nk_all_ref, nv_all_ref
    i = pl.program_id(0)
    nkeys = bb * WINDOW

    @pl.when(i == 0)
    def _project():
        x = x_ref[...]
        h = _rms(x, vec_ref[R_LN1:R_LN1 + 1, :]).astype(BF16)
        h_scr[...] = h
        qkv = _dot(h, wqkv_ref[...])
        seg = seg_ref[...]
        q_scr[...] = _seg_rms(qkv[:, Q0:K0], seg,
                              vec_ref[R_QG:R_QG + 1, 0:Q_WIDTH] * (HEAD_DIM ** -0.5))
        kn_scr[...] = _seg_rms(qkv[:, K0:V0], seg[0:KV_WIDTH, 0:KV_WIDTH],
                               vec_ref[R_KG:R_KG + 1, 0:KV_WIDTH])
        vn_scr[...] = qkv[:, V0:QKV_WIDTH]
        bucket = bucket_ref[...]
        rowb = lax.broadcasted_iota(jnp.int32, (bb, nkeys), 0)
        colb = lax.broadcasted_iota(jnp.int32, (bb, nkeys), 1) // WINDOW
        for hh in range(N_HEADS):
            brow = _gather_bias(bucket, t5_ref, hh)
            brow = jnp.concatenate([brow] * bb, axis=1)
            mb_scr[hh * bb:(hh + 1) * bb, :] = jnp.where(rowb == colb, brow, NEG_INF)

    rows = pl.ds(pl.multiple_of(i * bb, bb), bb)
    lane = lax.broadcasted_iota(jnp.int32, (1, LANES), 1)
    low = lane < HEAD_DIM
    qblk = q_scr[rows, :]
    qz = []
    for kh in range(N_KV_HEADS):
        keep = low if kh == 0 else jnp.logical_not(low)
        for g in range(GROUP):
            qz.append(jnp.where(keep, qblk[:, g * LANES:(g + 1) * LANES], 0.0))
    qz = jnp.concatenate(qz, axis=0).astype(BF16)
    kt = jnp.concatenate([ck_ref[b] for b in range(bb)], axis=1).astype(BF16)
    vt = jnp.concatenate([cv_ref[b] for b in range(bb)], axis=1).astype(BF16)
    s = _dot(qz, kt) + mb_scr[...]
    kn_blk = kn_scr[rows, :]
    vn_blk = vn_scr[rows, :]
    knew = jnp.concatenate([kn_blk.astype(BF16).astype(F32)] * N_HEADS, axis=0)
    vnew = jnp.concatenate([vn_blk.astype(BF16).astype(F32)] * N_HEADS, axis=0)
    self_bias = jnp.concatenate(
        [jnp.full((bb, 1), t5_ref[hh], F32) for hh in range(N_HEADS)], axis=0)
    sink = jnp.concatenate(
        [jnp.full((bb, 1), sinks_ref[hh], F32) for hh in range(N_HEADS)], axis=0)
    s_self = jnp.sum(qz.astype(F32) * knew, axis=-1, keepdims=True) + self_bias
    m = jnp.maximum(jnp.maximum(jnp.max(s, axis=-1, keepdims=True), s_self), sink)
    e = jnp.exp(s - m)
    e_self = jnp.exp(s_self - m)
    den = jnp.sum(e, axis=-1, keepdims=True) + e_self + jnp.exp(sink - m)
    inv = 1.0 / den
    o = (_dot_nt((e * inv).astype(BF16), vt)
         + (e_self * inv).astype(BF16).astype(F32) * vnew)
    half = GROUP * bb
    for g in range(GROUP):
        og = jnp.where(low, o[g * bb:(g + 1) * bb], o[half + g * bb:half + (g + 1) * bb])
        o_scr[rows, g * LANES:(g + 1) * LANES] = og

    knt = kn_blk.T
    vnt = vn_blk.T
    last = lax.broadcasted_iota(jnp.int32, (1, WINDOW), 1) == WINDOW - 1
    for b in range(bb):
        nk_ref[b] = jnp.where(last, knt[:, b:b + 1], pltpu.roll(ck_ref[b], WINDOW - 1, 1))
        nv_ref[b] = jnp.where(last, vnt[:, b:b + 1], pltpu.roll(cv_ref[b], WINDOW - 1, 1))

    @pl.when(i == pl.num_programs(0) - 1)
    def _finish():
        x = x_ref[...]
        h = h_scr[...]
        m_att = _sigmoid(_dot(h, win_ref[:, GA0:GL0])) * _dot(o_scr[...].astype(BF16), woa_ref[...])
        xr = _dot(h, win_ref[:, XR0:XG0])
        xr_ref[...] = xr
        xc = vec_ref[R_CB:R_CB + 1, :] + vec_ref[R_CW0 + CONV_WIDTH - 1:R_CW0 + CONV_WIDTH, :] * xr
        for j in range(CONV_WIDTH - 1):
            xc = xc + vec_ref[R_CW0 + j:R_CW0 + j + 1, :] * sc_ref[j]
        c_row = -LRU_C * _softplus(-vec_ref[R_LAM:R_LAM + 1, :])
        hn = []
        for hd in range(LRU_HEADS):
            lc = slice(hd * LRU_BLOCK, (hd + 1) * LRU_BLOCK)
            a, bb_ = _lru_gates(xc[:, lc], wax_ref[hd], vec_ref[R_BA:R_BA + 1, lc],
                                vec_ref[R_BX:R_BX + 1, lc], c_row[:, lc])
            hn.append(a * h0_ref[:, lc] + bb_)
        hn = jnp.concatenate(hn, axis=1)
        nh_ref[...] = hn
        y = (hn * jax.nn.gelu(_dot(h, win_ref[:, XG0:GA0]))).astype(BF16)
        m_all = m_att + _sigmoid(_dot(h, win_ref[:, GL0:IN_WIDTH])) * _dot(y, wol_ref[...])
        x1_ref[...] = x + _dot(m_all.astype(BF16), wout_ref[...])


def _mixer_sample(layer, x, ckt, cvt, h0, sc, prm, nk_buf, nv_buf):
    depth, nseq = ckt.shape[0], ckt.shape[1]
    bb = SAMPLE_BB
    assert nseq % bb == 0
    wl = lambda *z: lambda i: (layer,) + z
    full2 = lambda i: (0, 0)
    cache_spec = pl.BlockSpec((None, bb, KV_WIDTH, WINDOW), lambda i: (layer, i, 0, 0))
    n_in = 16
    kern = functools.partial(_mixer_sample_kernel, nseq=nseq, bb=bb)
    return pl.pallas_call(
        kern,
        grid=(nseq // bb,),
        in_specs=[
            _smem_spec(), _smem_spec(),
            _const_spec((nseq, D_MODEL), full2),
            _const_spec((None, VEC_ROWS, D_MODEL), wl(0, 0)),
            _const_spec((1, WINDOW), full2),
            _const_spec((Q_WIDTH, Q_WIDTH), full2),
            _const_spec((None, D_MODEL, QKV_WIDTH), wl(0, 0)),
            _const_spec((None, D_MODEL, IN_WIDTH), wl(0, 0)),
            _const_spec((None, Q_WIDTH, D_MODEL), wl(0, 0)),
            _const_spec((None, LRU_HEADS, LRU_BLOCK, 2 * LRU_BLOCK), wl(0, 0, 0)),
            _const_spec((None, LRU_WIDTH, D_MODEL), wl(0, 0)),
            _const_spec((None, D_MODEL, D_MODEL), wl(0, 0)),
            cache_spec, cache_spec,
            _const_spec((None, nseq, LRU_WIDTH), wl(0, 0)),
            _const_spec((None, CONV_WIDTH - 1, nseq, LRU_WIDTH), wl(0, 0, 0)),
            pl.BlockSpec(memory_space=pl.ANY), pl.BlockSpec(memory_space=pl.ANY),
        ],
        out_specs=[
            pl.BlockSpec((nseq, D_MODEL), full2),
            pl.BlockSpec((nseq, LRU_WIDTH), full2),
            pl.BlockSpec((nseq, LRU_WIDTH), full2),
            cache_spec, cache_spec,
        ],
        out_shape=[
            jax.ShapeDtypeStruct((nseq, D_MODEL), F32),
            jax.ShapeDtypeStruct((nseq, LRU_WIDTH), F32),
            jax.ShapeDtypeStruct((nseq, LRU_WIDTH), F32),
            jax.ShapeDtypeStruct((depth, nseq, KV_WIDTH, WINDOW), F32),
            jax.ShapeDtypeStruct((depth, nseq, KV_WIDTH, WINDOW), F32),
        ],
        input_output_aliases={n_in: 3, n_in + 1: 4},
        scratch_shapes=[
            pltpu.VMEM((nseq, D_MODEL), BF16),
            pltpu.VMEM((nseq, Q_WIDTH), F32),
            pltpu.VMEM((nseq, Q_WIDTH), F32),
            pltpu.VMEM((N_HEADS * bb, bb * WINDOW), F32),
            pltpu.VMEM((nseq, KV_WIDTH), F32),
            pltpu.VMEM((nseq, KV_WIDTH), F32),
        ],
        compiler_params=pltpu.CompilerParams(
            dimension_semantics=("arbitrary",), vmem_limit_bytes=VMEM_LIMIT_BYTES),
        name=f"mixer_sample_l{layer}",
    )(prm["t5"], prm["sinks"][layer], x, prm["vecs"], prm["bucket_s"], prm["seg"], prm["w_qkv"],
      prm["w_in"], prm["w_o_attn"], prm["w_ax"], prm["w_o_lru"], prm["w_out"], ckt, cvt, h0, sc,
      nk_buf, nv_buf)


FF_CHUNKS = ((0, 1024), (1024, 2048), (2048, D_FF))


def _ffn_rows(x, p, vec_ref, wg_ref, wu_ref, wd_ref, wpg_ref, wp_ref, act_scr):
    h2 = _rms(x, vec_ref[R_LN2:R_LN2 + 1, :]).astype(BF16)
    for lo, hi in FF_CHUNKS:
        g = _dot(h2, wg_ref[:, lo:hi])
        act = g * _sigmoid(g) * _dot(h2, wu_ref[:, lo:hi])
        act_scr[:, lo:hi] = act.astype(BF16)
    x = x + _dot(act_scr[...], wd_ref[...])
    h3 = _rms(x, vec_ref[R_LN3:R_LN3 + 1, :]).astype(BF16)
    gate = _sigmoid(_dot(h3, wpg_ref[...]))
    return x + gate * _dot(p.astype(BF16), wp_ref[...])


def _load_weights_bf16(jobs, stage, sem):
    chunk_rows = stage.shape[1]
    chunks = []
    for src, layer, dst in jobs:
        nrows, ncols = dst.shape
        for r0 in range(0, nrows, chunk_rows):
            nr = min(chunk_rows, nrows - r0)
            chunks.append((src.at[layer, pl.ds(r0, nr), :], dst.at[pl.ds(r0, nr), :], nr, ncols))

    def copy(i):
        src, _, nr, ncols = chunks[i]
        return pltpu.make_async_copy(src, stage.at[i % 2, pl.ds(0, nr), pl.ds(0, ncols)], sem.at[i % 2])

    copy(0).start()
    for i, (_, dst, nr, ncols) in enumerate(chunks):
        if i + 1 < len(chunks):
            copy(i + 1).start()
        copy(i).wait()
        dst[...] = stage[i % 2, 0:nr, 0:ncols].astype(BF16)


def _ffn_kernel(x_ref, p_ref, xs_ref, ps_ref, vec_ref, wg_hbm, wu_hbm, wd_hbm, wpg_hbm, wp_hbm,
                o_ref, os_ref, act_scr, wg_ref, wu_ref, wd_ref, wpg_ref, wp_ref, stage, sem, *, layer):
    @pl.when(pl.program_id(0) == 0)
    def _weights():
        _load_weights_bf16([(wg_hbm, layer, wg_ref), (wu_hbm, layer, wu_ref), (wd_hbm, layer, wd_ref),
                            (wpg_hbm, layer, wpg_ref), (wp_hbm, layer, wp_ref)], stage, sem)

    weights = (vec_ref, wg_ref, wu_ref, wd_ref, wpg_ref, wp_ref)
    o_ref[...] = _ffn_rows(x_ref[...], p_ref[...], *weights, act_scr)

    @pl.when(pl.program_id(0) == pl.num_programs(0) - 1)
    def _sample():
        ns = xs_ref.shape[0]
        os_ref[...] = _ffn_rows(xs_ref[...], ps_ref[...], *weights, act_scr.at[0:ns, :])


def _ffn(layer, x, p, xs, ps, prm):
    rows, ns = x.shape[0], xs.shape[0]
    tm = min(FFN_TM, rows)
    assert rows % tm == 0 and ns <= tm
    wl = lambda *z: lambda r: (layer,) + z
    hbm = pl.BlockSpec(memory_space=pl.ANY)
    return pl.pallas_call(
        functools.partial(_ffn_kernel, layer=layer),
        grid=(rows // tm,),
        in_specs=[
            pl.BlockSpec((tm, D_MODEL), lambda r: (r, 0)),
            pl.BlockSpec((None, tm, PLE_DIM), lambda r: (layer, r, 0)),
            _const_spec((ns, D_MODEL), lambda r: (0, 0)),
            _const_spec((None, ns, PLE_DIM), wl(0, 0)),
            _const_spec((None, VEC_ROWS, D_MODEL), wl(0, 0)),
            hbm, hbm, hbm, hbm, hbm,
        ],
        out_specs=[
            pl.BlockSpec((tm, D_MODEL), lambda r: (r, 0)),
            pl.BlockSpec((ns, D_MODEL), lambda r: (0, 0)),
        ],
        out_shape=[
            jax.ShapeDtypeStruct((rows, D_MODEL), F32),
            jax.ShapeDtypeStruct((ns, D_MODEL), F32),
        ],
        scratch_shapes=[
            pltpu.VMEM((tm, D_FF), BF16),
            pltpu.VMEM((D_MODEL, D_FF), BF16),
            pltpu.VMEM((D_MODEL, D_FF), BF16),
            pltpu.VMEM((D_FF, D_MODEL), BF16),
            pltpu.VMEM((D_MODEL, D_MODEL), BF16),
            pltpu.VMEM((PLE_DIM, D_MODEL), BF16),
            pltpu.VMEM((2, WEIGHT_CHUNK_ROWS, D_FF), F32),
            pltpu.SemaphoreType.DMA((2,)),
        ],
        compiler_params=pltpu.CompilerParams(
            dimension_semantics=("arbitrary",), vmem_limit_bytes=VMEM_LIMIT_BYTES),
        name=f"ffn_l{layer}",
    )(x, p, xs, ps, prm["vecs"], prm["w_gate"], prm["w_up"], prm["w_down"], prm["w_ple_gate"],
      prm["w_ple"])


def _t5_bucket(dist):
    n = jnp.maximum(dist, 0)
    max_exact = N_BUCKETS // 2
    nf = jnp.maximum(n, 1).astype(F32)
    large = max_exact + (jnp.log(nf / max_exact) / math.log(MAX_DISTANCE / max_exact)
                         * (N_BUCKETS - max_exact)).astype(jnp.int32)
    large = jnp.minimum(large, N_BUCKETS - 1)
    return jnp.where(n < max_exact, n, large)


def _regroup_heads(w, axis):
    shape = w.shape
    w = w.reshape(shape[:axis] + (N_KV_HEADS, GROUP, HEAD_DIM) + shape[axis + 1:])
    return jnp.swapaxes(w, axis, axis + 1).reshape(shape)


def _prepare(t5_table, ln1, w_in, q_gain, k_gain, sinks, w_o_attn, conv_w, conv_b, w_a, b_a,
             w_x, b_x, lam, w_o_lru, w_out, ln2, w_gate, w_up, w_down, ln3, w_ple, w_ple_gate):
    depth = w_in.shape[0]
    w_qkv = jnp.concatenate(
        [_regroup_heads(w_in[:, :, 0:Q_WIDTH], 2), w_in[:, :, Q_WIDTH:QKV_WIDTH]], axis=2).astype(BF16)
    vecs = jnp.zeros((depth, VEC_ROWS, D_MODEL), F32)
    vecs = vecs.at[:, R_LN1].set(ln1)
    vecs = vecs.at[:, R_CW0:R_CW0 + CONV_WIDTH].set(conv_w)
    vecs = vecs.at[:, R_CB].set(conv_b)
    vecs = vecs.at[:, R_BA].set(b_a)
    vecs = vecs.at[:, R_BX].set(b_x)
    vecs = vecs.at[:, R_LAM].set(lam)
    vecs = vecs.at[:, R_QG, 0:Q_WIDTH].set(jnp.tile(q_gain, (1, N_HEADS)))
    vecs = vecs.at[:, R_KG, 0:KV_WIDTH].set(jnp.tile(k_gain, (1, N_KV_HEADS)))
    vecs = vecs.at[:, R_LN2].set(ln2)
    vecs = vecs.at[:, R_LN3].set(ln3)
    head_id = np.arange(Q_WIDTH) // HEAD_DIM
    seg = jnp.asarray(head_id[:, None] == head_id[None, :], BF16)
    dist_p = (WINDOW + jnp.arange(WINDOW))[:, None] - jnp.arange(2 * WINDOW)[None, :]
    bucket_p = jnp.where((dist_p >= 0) & (dist_p < WINDOW), _t5_bucket(dist_p), -1).astype(jnp.int32)
    dist_s = (WINDOW - jnp.arange(WINDOW))[None, :]
    bucket_s = jnp.where((dist_s >= 0) & (dist_s < WINDOW), _t5_bucket(dist_s), -1).astype(jnp.int32)
    return {
        "t5": t5_table.reshape(-1),
        "sinks": sinks,
        "vecs": vecs,
        "seg": seg,
        "bucket_p": bucket_p,
        "bucket_s": bucket_s,
        "w_qkv": w_qkv,
        "w_in": w_in.astype(BF16),
        "w_o_attn": _regroup_heads(w_o_attn, 1).astype(BF16),
        "w_ax": jnp.concatenate([w_a, w_x], axis=-1).astype(BF16),
        "w_o_lru": w_o_lru.astype(BF16),
        "w_out": w_out.astype(BF16),
        "w_gate": w_gate,
        "w_up": w_up,
        "w_down": w_down,
        "w_ple_gate": w_ple_gate,
        "w_ple": w_ple,
    }


def kernel(x_prompt, x_sample, cache_k_win, cache_v_win, state_lru_h, state_conv, p_prompt,
           p_sample, t5_table, ln1, w_in, q_gain, k_gain, sinks, w_o_attn, conv_w, conv_b, w_a,
           b_a, w_x, b_x, lam, w_o_lru, w_out, ln2, w_gate, w_up, w_down, ln3, w_ple, w_ple_gate):
    depth = w_in.shape[0]
    nb, seq, _ = x_prompt.shape
    nseq = x_sample.shape[0]
    assert x_sample.shape[1] == 1 and cache_k_win.shape[2] == WINDOW
    prm = _prepare(t5_table, ln1, w_in, q_gain, k_gain, sinks, w_o_attn, conv_w, conv_b, w_a, b_a,
                   w_x, b_x, lam, w_o_lru, w_out, ln2, w_gate, w_up, w_down, ln3, w_ple, w_ple_gate)
    ckt = jnp.transpose(cache_k_win, (0, 1, 3, 4, 2)).reshape(depth, nseq, KV_WIDTH, WINDOW)
    cvt = jnp.transpose(cache_v_win, (0, 1, 3, 4, 2)).reshape(depth, nseq, KV_WIDTH, WINDOW)
    sc = jnp.transpose(state_conv, (0, 2, 1, 3))
    pp = p_prompt.reshape(depth, nb * seq, PLE_DIM)
    ps = p_sample.reshape(depth, nseq, PLE_DIM)

    yp = x_prompt
    ys = x_sample.reshape(nseq, D_MODEL)
    outs = [[] for _ in range(6)]
    nk_buf = jnp.zeros((depth, nseq, KV_WIDTH, WINDOW), F32)
    nv_buf = jnp.zeros((depth, nseq, KV_WIDTH, WINDOW), F32)
    for layer in range(depth):
        x1, nk, nv, nh, nc = _mixer_prompt(layer, yp, prm)
        outs[0].append(nk.reshape(nb, WINDOW, N_KV_HEADS, HEAD_DIM))
        outs[1].append(nv.reshape(nb, WINDOW, N_KV_HEADS, HEAD_DIM))
        outs[2].append(nh)
        outs[3].append(nc[:, SUBLANES - (CONV_WIDTH - 1):, :])

        x1s, nhs, xr, nk_buf, nv_buf = _mixer_sample(layer, ys, ckt, cvt, state_lru_h, sc, prm,
                                                     nk_buf, nv_buf)
        yp, ys = _ffn(layer, x1.reshape(nb * seq, D_MODEL), pp, x1s, ps, prm)
        yp = yp.reshape(nb, seq, D_MODEL)
        outs[4].append(nhs)
        outs[5].append(jnp.concatenate([state_conv[layer, :, 1:], xr[:, None, :]], axis=1))

    def untranspose(buf):
        return jnp.transpose(buf.reshape(depth, nseq, N_KV_HEADS, HEAD_DIM, WINDOW), (0, 1, 4, 2, 3))

    stacked = [jnp.stack(o) for o in outs]
    return (yp, ys.reshape(nseq, 1, D_MODEL), stacked[0], stacked[1], stacked[2], stacked[3],
            untranspose(nk_buf), untranspose(nv_buf), stacked[4], stacked[5])
```

```python
import functools
import math

import numpy as np
import jax
import jax.numpy as jnp
from jax import lax
from jax.experimental import pallas as pl
from jax.experimental.pallas import tpu as pltpu

F32 = jnp.float32
BF16 = jnp.bfloat16

D_MODEL = 1024
HEAD_DIM = 64
N_HEADS = 8
N_KV_HEADS = 2
GROUP = N_HEADS // N_KV_HEADS
Q_WIDTH = N_HEADS * HEAD_DIM
KV_WIDTH = N_KV_HEADS * HEAD_DIM
WINDOW = 128
N_BUCKETS = 32
MAX_DISTANCE = 128
LRU_WIDTH = D_MODEL
LRU_HEADS = 8
LRU_BLOCK = LRU_WIDTH // LRU_HEADS
LRU_C = 8.0
CONV_WIDTH = 4
D_FF = 2816
PLE_DIM = 256
PAST_LEN = 8192
EPS = 1e-6
NEG_INF = -1e30
TINY = 1e-30
LOG2E = math.log2(math.e)

SUBLANES = 8
LANES = 128
VMEM_LIMIT_BYTES = 56 * 1024 * 1024

Q0 = 0
K0 = Q0 + Q_WIDTH
V0 = K0 + KV_WIDTH
QKV_WIDTH = V0 + KV_WIDTH
XR0 = QKV_WIDTH
XG0 = XR0 + LRU_WIDTH
GA0 = XG0 + LRU_WIDTH
GL0 = GA0 + D_MODEL
IN_WIDTH = GL0 + D_MODEL

R_LN1, R_CW0, R_CB, R_BA, R_BX, R_LAM, R_QG, R_KG, R_LN2, R_LN3 = 0, 1, 5, 6, 7, 8, 9, 10, 11, 12
VEC_ROWS = 16

PROMPT_TT = 64
FFN_TM = 1024
SAMPLE_BB = 16
PROJ_CHUNK = 512


def _dot(a, b):
    return jnp.dot(a, b, preferred_element_type=F32)


def _dot_nt(a, b):
    return lax.dot_general(a, b, (((1,), (1,)), ((), ())), preferred_element_type=F32)


def _rms(x, g):
    ms = jnp.mean(x * x, axis=-1, keepdims=True)
    return x * lax.rsqrt(ms + EPS) * g


def _seg_rms(x, seg, g):
    x2 = x * x
    ssq = _dot(x2.astype(BF16), seg)
    return x * lax.rsqrt(ssq * (1.0 / HEAD_DIM) + EPS) * g


def _sigmoid(x):
    return 0.5 * jnp.tanh(0.5 * x) + 0.5


def _softplus(z):
    return jnp.maximum(z, 0.0) + jnp.log1p(jnp.exp(-jnp.abs(z)))


def _gather_bias(bucket, t5_ref, head):
    acc = jnp.full(bucket.shape, NEG_INF, F32)
    for bkt in range(N_BUCKETS):
        acc = jnp.where(bucket == bkt, t5_ref[bkt * N_HEADS + head], acc)
    return acc


def _lru_gates(xch, wax, b_a, b_x, c_row):
    gts = _dot(xch.astype(BF16), wax)
    r = _sigmoid(gts[:, :LRU_BLOCK] + b_a)
    ig = _sigmoid(gts[:, LRU_BLOCK:] + b_x)
    log_a = c_row * r
    a = jnp.exp(log_a)
    y = 1.0 - a * a
    return a, y * lax.rsqrt(jnp.maximum(y, TINY)) * (ig * xch)


def _mixer_prompt_kernel(t5_ref, sinks_ref, x_ref, vec_ref, bucket_ref, seg_ref, wqkv_ref, win_ref,
                         woa_ref, wax_ref, wol_ref, wout_ref,
                         x1_ref, nk_ref, nv_ref, nh_ref, nc_ref,
                         bias_scr, bias_t, kband, vband, cs, a_scr, b_scr, h_scr, hst, y_scr, o_scr,
                         proj_scr, q_scr, *, nb, tt):
    tm = nb * tt
    t = pl.program_id(0)

    @pl.when(t == 0)
    def _init():
        bucket = bucket_ref[...]
        for hh in range(N_HEADS):
            bias_scr[hh] = _gather_bias(bucket, t5_ref, hh) * LOG2E
        kband[...] = jnp.zeros((nb, 2 * WINDOW, KV_WIDTH), F32)
        vband[...] = jnp.zeros((nb, 2 * WINDOW, KV_WIDTH), F32)
        cs[:, 0:SUBLANES, :] = jnp.zeros((nb, SUBLANES, LRU_WIDTH), F32)
        hst[...] = jnp.zeros((nb, LRU_WIDTH), F32)

    x = x_ref[...].reshape(tm, D_MODEL)
    h = _rms(x, vec_ref[R_LN1:R_LN1 + 1, :]).astype(BF16)

    cs[:, SUBLANES:SUBLANES + tt, :] = _dot(h, win_ref[:, XR0:XG0]).reshape(nb, tt, LRU_WIDTH)
    c_row = -LRU_C * _softplus(-vec_ref[R_LAM:R_LAM + 1, :])
    n_chunks = (IN_WIDTH - XG0) // PROJ_CHUNK
    qkv = None
    for hd in range(LRU_HEADS):
        lc = slice(hd * LRU_BLOCK, (hd + 1) * LRU_BLOCK)
        xc = vec_ref[R_CB:R_CB + 1, lc]
        for j in range(CONV_WIDTH):
            off = SUBLANES - (CONV_WIDTH - 1) + j
            xc = xc + vec_ref[R_CW0 + j:R_CW0 + j + 1, lc] * cs[:, off:off + tt, lc]
        a, bb = _lru_gates(xc.reshape(tm, LRU_BLOCK), wax_ref[hd], vec_ref[R_BA:R_BA + 1, lc],
                           vec_ref[R_BX:R_BX + 1, lc], c_row[:, lc])
        for b in range(nb):
            rows = pl.ds(b, tt, stride=nb)
            a_scr[hd, rows, :] = a[b * tt:(b + 1) * tt]
            b_scr[hd, rows, :] = bb[b * tt:(b + 1) * tt]
        if hd < n_chunks:
            pc = slice(hd * PROJ_CHUNK, (hd + 1) * PROJ_CHUNK)
            proj_scr[:, pc] = _dot(h, win_ref[:, XG0 + hd * PROJ_CHUNK:XG0 + (hd + 1) * PROJ_CHUNK])
        elif hd == n_chunks:
            qkv = _dot(h, wqkv_ref[...])
    tail = cs[:, tt:tt + SUBLANES, :]
    nc_ref[...] = tail
    cs[:, 0:SUBLANES, :] = tail

    seg = seg_ref[...]
    q_scr[...] = _seg_rms(qkv[:, Q0:K0], seg,
                          vec_ref[R_QG:R_QG + 1, 0:Q_WIDTH] * (LOG2E * HEAD_DIM ** -0.5))
    kn = _seg_rms(qkv[:, K0:V0], seg[0:KV_WIDTH, 0:KV_WIDTH], vec_ref[R_KG:R_KG + 1, 0:KV_WIDTH])
    kband[:, WINDOW:WINDOW + tt, :] = kn.reshape(nb, tt, KV_WIDTH)
    vband[:, WINDOW:WINDOW + tt, :] = qkv[:, V0:QKV_WIDTH].reshape(nb, tt, KV_WIDTH)
    nk_ref[...] = kband[:, tt:tt + WINDOW, :]
    nv_ref[...] = vband[:, tt:tt + WINDOW, :]

    @pl.when(t * tt <= WINDOW)
    def _mask_bias():
        col = lax.broadcasted_iota(jnp.int32, (1, 2 * WINDOW), 1)
        colmask = jnp.where(col >= WINDOW - t * tt, 0.0, NEG_INF)
        for hh in range(N_HEADS):
            bias_t[hh] = bias_scr[hh, 0:tt, :] + colmask

    lane = lax.broadcasted_iota(jnp.int32, (1, LANES), 1)
    low = lane < HEAD_DIM
    for b in range(nb):
        kb = kband[b].astype(BF16)
        vb = vband[b].astype(BF16)
        o_kv = []
        for kh in range(N_KV_HEADS):
            keep = low if kh == 0 else jnp.logical_not(low)
            q4 = jnp.concatenate(
                [jnp.where(keep, q_scr[b * tt:(b + 1) * tt, g * LANES:(g + 1) * LANES], 0.0)
                 for g in range(GROUP)], axis=0).astype(BF16)
            s = _dot_nt(q4, kb)
            ps, invs = [], []
            for g in range(GROUP):
                hh = kh * GROUP + g
                sg = s[g * tt:(g + 1) * tt] + bias_t[hh]
                sink = sinks_ref[hh] * LOG2E
                m = jnp.maximum(jnp.max(sg, axis=-1, keepdims=True), sink)
                e = jnp.exp2(sg - m)
                invs.append(1.0 / (jnp.sum(e, axis=-1, keepdims=True) + jnp.exp2(sink - m)))
                ps.append(e.astype(BF16))
            o4 = _dot(jnp.concatenate(ps, axis=0), vb)
            o_kv.append([o4[g * tt:(g + 1) * tt] * invs[g] for g in range(GROUP)])
        for g in range(GROUP):
            og = jnp.where(low, o_kv[0][g], o_kv[1][g])
            o_scr[b * tt:(b + 1) * tt, g * LANES:(g + 1) * LANES] = og.astype(BF16)

    kband[:, 0:WINDOW, :] = kband[:, tt:tt + WINDOW, :]
    vband[:, 0:WINDOW, :] = vband[:, tt:tt + WINDOW, :]

    m_att = _sigmoid(proj_scr[:, GA0 - XG0:GL0 - XG0]) * _dot(o_scr[...], woa_ref[...])

    def scan_step(ts, hs):
        out = []
        for hd in range(LRU_HEADS):
            rows = pl.ds(pl.multiple_of(ts * nb, nb), nb)
            hn = a_scr[hd, rows, :] * hs[hd] + b_scr[hd, rows, :]
            h_scr[hd, rows, :] = hn
            out.append(hn)
        return tuple(out)

    hs0 = tuple(hst[:, hd * LRU_BLOCK:(hd + 1) * LRU_BLOCK] for hd in range(LRU_HEADS))
    hs = lax.fori_loop(0, tt, scan_step, hs0, unroll=True)
    hfin = jnp.concatenate(hs, axis=1)
    hst[...] = hfin
    nh_ref[...] = hfin

    for hd in range(LRU_HEADS):
        lc = slice(hd * LRU_BLOCK, (hd + 1) * LRU_BLOCK)
        hseq = jnp.concatenate([h_scr[hd, pl.ds(b, tt, stride=nb), :] for b in range(nb)], axis=0)
        y_scr[:, lc] = (hseq * jax.nn.gelu(proj_scr[:, lc])).astype(BF16)

    m_all = m_att + _sigmoid(proj_scr[:, GL0 - XG0:IN_WIDTH - XG0]) * _dot(y_scr[...], wol_ref[...])
    x1 = x + _dot(m_all.astype(BF16), wout_ref[...])
    x1_ref[...] = x1.reshape(nb, tt, D_MODEL)


def _const_spec(shape, index):
    return pl.BlockSpec(shape, index, pipeline_mode=pl.Buffered(1))


def _smem_spec():
    return pl.BlockSpec(memory_space=pltpu.SMEM)


def _mixer_prompt(layer, x, prm):
    nb, seq, _ = x.shape
    tt = PROMPT_TT
    assert nb == SUBLANES and seq % tt == 0 and WINDOW % tt == 0
    tm = nb * tt
    wl = lambda *z: lambda t: (layer,) + z
    kern = functools.partial(_mixer_prompt_kernel, nb=nb, tt=tt)
    return pl.pallas_call(
        kern,
        grid=(seq // tt,),
        in_specs=[
            _smem_spec(), _smem_spec(),
            pl.BlockSpec((nb, tt, D_MODEL), lambda t: (0, t, 0)),
            _const_spec((None, VEC_ROWS, D_MODEL), wl(0, 0)),
            _const_spec((WINDOW, 2 * WINDOW), lambda t: (0, 0)),
            _const_spec((Q_WIDTH, Q_WIDTH), lambda t: (0, 0)),
            _const_spec((None, D_MODEL, QKV_WIDTH), wl(0, 0)),
            _const_spec((None, D_MODEL, IN_WIDTH), wl(0, 0)),
            _const_spec((None, Q_WIDTH, D_MODEL), wl(0, 0)),
            _const_spec((None, LRU_HEADS, LRU_BLOCK, 2 * LRU_BLOCK), wl(0, 0, 0)),
            _const_spec((None, LRU_WIDTH, D_MODEL), wl(0, 0)),
            _const_spec((None, D_MODEL, D_MODEL), wl(0, 0)),
        ],
        out_specs=[
            pl.BlockSpec((nb, tt, D_MODEL), lambda t: (0, t, 0)),
            pl.BlockSpec((nb, WINDOW, KV_WIDTH), lambda t: (0, 0, 0)),
            pl.BlockSpec((nb, WINDOW, KV_WIDTH), lambda t: (0, 0, 0)),
            pl.BlockSpec((nb, LRU_WIDTH), lambda t: (0, 0)),
            pl.BlockSpec((nb, SUBLANES, LRU_WIDTH), lambda t: (0, 0, 0)),
        ],
        out_shape=[
            jax.ShapeDtypeStruct((nb, seq, D_MODEL), F32),
            jax.ShapeDtypeStruct((nb, WINDOW, KV_WIDTH), F32),
            jax.ShapeDtypeStruct((nb, WINDOW, KV_WIDTH), F32),
            jax.ShapeDtypeStruct((nb, LRU_WIDTH), F32),
            jax.ShapeDtypeStruct((nb, SUBLANES, LRU_WIDTH), F32),
        ],
        scratch_shapes=[
            pltpu.VMEM((N_HEADS, WINDOW, 2 * WINDOW), F32),
            pltpu.VMEM((N_HEADS, tt, 2 * WINDOW), F32),
            pltpu.VMEM((nb, 2 * WINDOW, KV_WIDTH), F32),
            pltpu.VMEM((nb, 2 * WINDOW, KV_WIDTH), F32),
            pltpu.VMEM((nb, tt + SUBLANES, LRU_WIDTH), F32),
            pltpu.VMEM((LRU_HEADS, tm, LRU_BLOCK), F32),
            pltpu.VMEM((LRU_HEADS, tm, LRU_BLOCK), F32),
            pltpu.VMEM((LRU_HEADS, tm, LRU_BLOCK), F32),
            pltpu.VMEM((nb, LRU_WIDTH), F32),
            pltpu.VMEM((tm, LRU_WIDTH), BF16),
            pltpu.VMEM((tm, Q_WIDTH), BF16),
            pltpu.VMEM((tm, IN_WIDTH - XG0), F32),
            pltpu.VMEM((tm, Q_WIDTH), F32),
        ],
        compiler_params=pltpu.CompilerParams(
            dimension_semantics=("arbitrary",), vmem_limit_bytes=VMEM_LIMIT_BYTES),
        name=f"mixer_prompt_l{layer}",
    )(prm["t5"], prm["sinks"][layer], x, prm["vecs"], prm["bucket_p"], prm["seg"], prm["w_qkv"],
      prm["w_in"], prm["w_o_attn"], prm["w_ax"], prm["w_o_lru"], prm["w_out"])


def _mixer_sample_kernel(t5_ref, sinks_ref, x_ref, vec_ref, bucket_ref, seg_ref, wqkv_ref, win_ref,
                         woa_ref, wax_ref, wol_ref, wout_ref, ck_ref, cv_ref, h0_ref, sc_ref,
                         nk_all_ref, nv_all_ref,
                         x1_ref, nh_ref, xr_ref, nk_ref, nv_ref,
                         h_scr, q_scr, o_scr, mb_scr, kn_scr, vn_scr,
                         *, nseq, bb):
    del nk_all_ref, nv_all_ref
    i = pl.program_id(0)
    nkeys = bb * WINDOW

    @pl.when(i == 0)
    def _project():
        x = x_ref[...]
        h = _rms(x, vec_ref[R_LN1:R_LN1 + 1, :]).astype(BF16)
        h_scr[...] = h
        qkv = _dot(h, wqkv_ref[...])
        seg = seg_ref[...]
        q_scr[...] = _seg_rms(qkv[:, Q0:K0], seg,
                              vec_ref[R_QG:R_QG + 1, 0:Q_WIDTH] * (HEAD_DIM ** -0.5))
        kn_scr[...] = _seg_rms(qkv[:, K0:V0], seg[0:KV_WIDTH, 0:KV_WIDTH],
                               vec_ref[R_KG:R_KG + 1, 0:KV_WIDTH])
        vn_scr[...] = qkv[:, V0:QKV_WIDTH]
        bucket = bucket_ref[...]
        rowb = lax.broadcasted_iota(jnp.int32, (bb, nkeys), 0)
        colb = lax.broadcasted_iota(jnp.int32, (bb, nkeys), 1) // WINDOW
        for hh in range(N_HEADS):
            brow = _gather_bias(bucket, t5_ref, hh)
            brow = jnp.concatenate([brow] * bb, axis=1)
            mb_scr[hh * bb:(hh + 1) * bb, :] = jnp.where(rowb == colb, brow, NEG_INF)

    rows = pl.ds(pl.multiple_of(i * bb, bb), bb)
    lane = lax.broadcasted_iota(jnp.int32, (1, LANES), 1)
    low = lane < HEAD_DIM
    qblk = q_scr[rows, :]
    qz = []
    for kh in range(N_KV_HEADS):
        keep = low if kh == 0 else jnp.logical_not(low)
        for g in range(GROUP):
            qz.append(jnp.where(keep, qblk[:, g * LANES:(g + 1) * LANES], 0.0))
    qz = jnp.concatenate(qz, axis=0).astype(BF16)
    kt = jnp.concatenate([ck_ref[b] for b in range(bb)], axis=1).astype(BF16)
    vt = jnp.concatenate([cv_ref[b] for b in range(bb)], axis=1).astype(BF16)
    s = _dot(qz, kt) + mb_scr[...]
    kn_blk = kn_scr[rows, :]
    vn_blk = vn_scr[rows, :]
    knew = jnp.concatenate([kn_blk.astype(BF16).astype(F32)] * N_HEADS, axis=0)
    vnew = jnp.concatenate([vn_blk.astype(BF16).astype(F32)] * N_HEADS, axis=0)
    self_bias = jnp.concatenate(
        [jnp.full((bb, 1), t5_ref[hh], F32) for hh in range(N_HEADS)], axis=0)
    sink = jnp.concatenate(
        [jnp.full((bb, 1), sinks_ref[hh], F32) for hh in range(N_HEADS)], axis=0)
    s_self = jnp.sum(qz.astype(F32) * knew, axis=-1, keepdims=True) + self_bias
    m = jnp.maximum(jnp.maximum(jnp.max(s, axis=-1, keepdims=True), s_self), sink)
    e = jnp.exp(s - m)
    e_self = jnp.exp(s_self - m)
    den = jnp.sum(e, axis=-1, keepdims=True) + e_self + jnp.exp(sink - m)
    inv = 1.0 / den
    o = (_dot_nt((e * inv).astype(BF16), vt)
         + (e_self * inv).astype(BF16).astype(F32) * vnew)
    half = GROUP * bb
    for g in range(GROUP):
        og = jnp.where(low, o[g * bb:(g + 1) * bb], o[half + g * bb:half + (g + 1) * bb])
        o_scr[rows, g * LANES:(g + 1) * LANES] = og

    knt = kn_blk.T
    vnt = vn_blk.T
    last = lax.broadcasted_iota(jnp.int32, (1, WINDOW), 1) == WINDOW - 1
    for b in range(bb):
        nk_ref[b] = jnp.where(last, knt[:, b:b + 1], pltpu.roll(ck_ref[b], WINDOW - 1, 1))
        nv_ref[b] = jnp.where(last, vnt[:, b:b + 1], pltpu.roll(cv_ref[b], WINDOW - 1, 1))

    @pl.when(i == pl.num_programs(0) - 1)
    def _finish():
        x = x_ref[...]
        h = h_scr[...]
        m_att = _sigmoid(_dot(h, win_ref[:, GA0:GL0])) * _dot(o_scr[...].astype(BF16), woa_ref[...])
        xr = _dot(h, win_ref[:, XR0:XG0])
        xr_ref[...] = xr
        xc = vec_ref[R_CB:R_CB + 1, :] + vec_ref[R_CW0 + CONV_WIDTH - 1:R_CW0 + CONV_WIDTH, :] * xr
        for j in range(CONV_WIDTH - 1):
            xc = xc + vec_ref[R_CW0 + j:R_CW0 + j + 1, :] * sc_ref[j]
        c_row = -LRU_C * _softplus(-vec_ref[R_LAM:R_LAM + 1, :])
        hn = []
        for hd in range(LRU_HEADS):
            lc = slice(hd * LRU_BLOCK, (hd + 1) * LRU_BLOCK)
            a, bb_ = _lru_gates(xc[:, lc], wax_ref[hd], vec_ref[R_BA:R_BA + 1, lc],
                                vec_ref[R_BX:R_BX + 1, lc], c_row[:, lc])
            hn.append(a * h0_ref[:, lc] + bb_)
        hn = jnp.concatenate(hn, axis=1)
        nh_ref[...] = hn
        y = (hn * jax.nn.gelu(_dot(h, win_ref[:, XG0:GA0]))).astype(BF16)
        m_all = m_att + _sigmoid(_dot(h, win_ref[:, GL0:IN_WIDTH])) * _dot(y, wol_ref[...])
        x1_ref[...] = x + _dot(m_all.astype(BF16), wout_ref[...])


def _mixer_sample(layer, x, ckt, cvt, h0, sc, prm, nk_buf, nv_buf):
    depth, nseq = ckt.shape[0], ckt.shape[1]
    bb = SAMPLE_BB
    assert nseq % bb == 0
    wl = lambda *z: lambda i: (layer,) + z
    full2 = lambda i: (0, 0)
    cache_spec = pl.BlockSpec((None, bb, KV_WIDTH, WINDOW), lambda i: (layer, i, 0, 0))
    n_in = 16
    kern = functools.partial(_mixer_sample_kernel, nseq=nseq, bb=bb)
    return pl.pallas_call(
        kern,
        grid=(nseq // bb,),
        in_specs=[
            _smem_spec(), _smem_spec(),
            _const_spec((nseq, D_MODEL), full2),
            _const_spec((None, VEC_ROWS, D_MODEL), wl(0, 0)),
            _const_spec((1, WINDOW), full2),
            _const_spec((Q_WIDTH, Q_WIDTH), full2),
            _const_spec((None, D_MODEL, QKV_WIDTH), wl(0, 0)),
            _const_spec((None, D_MODEL, IN_WIDTH), wl(0, 0)),
            _const_spec((None, Q_WIDTH, D_MODEL), wl(0, 0)),
            _const_spec((None, LRU_HEADS, LRU_BLOCK, 2 * LRU_BLOCK), wl(0, 0, 0)),
            _const_spec((None, LRU_WIDTH, D_MODEL), wl(0, 0)),
            _const_spec((None, D_MODEL, D_MODEL), wl(0, 0)),
            cache_spec, cache_spec,
            _const_spec((None, nseq, LRU_WIDTH), wl(0, 0)),
            _const_spec((None, CONV_WIDTH - 1, nseq, LRU_WIDTH), wl(0, 0, 0)),
            pl.BlockSpec(memory_space=pl.ANY), pl.BlockSpec(memory_space=pl.ANY),
        ],
        out_specs=[
            pl.BlockSpec((nseq, D_MODEL), full2),
            pl.BlockSpec((nseq, LRU_WIDTH), full2),
            pl.BlockSpec((nseq, LRU_WIDTH), full2),
            cache_spec, cache_spec,
        ],
        out_shape=[
            jax.ShapeDtypeStruct((nseq, D_MODEL), F32),
            jax.ShapeDtypeStruct((nseq, LRU_WIDTH), F32),
            jax.ShapeDtypeStruct((nseq, LRU_WIDTH), F32),
            jax.ShapeDtypeStruct((depth, nseq, KV_WIDTH, WINDOW), F32),
            jax.ShapeDtypeStruct((depth, nseq, KV_WIDTH, WINDOW), F32),
        ],
        input_output_aliases={n_in: 3, n_in + 1: 4},
        scratch_shapes=[
            pltpu.VMEM((nseq, D_MODEL), BF16),
            pltpu.VMEM((nseq, Q_WIDTH), F32),
            pltpu.VMEM((nseq, Q_WIDTH), F32),
            pltpu.VMEM((N_HEADS * bb, bb * WINDOW), F32),
            pltpu.VMEM((nseq, KV_WIDTH), F32),
            pltpu.VMEM((nseq, KV_WIDTH), F32),
        ],
        compiler_params=pltpu.CompilerParams(
            dimension_semantics=("arbitrary",), vmem_limit_bytes=VMEM_LIMIT_BYTES),
        name=f"mixer_sample_l{layer}",
    )(prm["t5"], prm["sinks"][layer], x, prm["vecs"], prm["bucket_s"], prm["seg"], prm["w_qkv"],
      prm["w_in"], prm["w_o_attn"], prm["w_ax"], prm["w_o_lru"], prm["w_out"], ckt, cvt, h0, sc,
      nk_buf, nv_buf)


FF_CHUNKS = ((0, 1024), (1024, 2048), (2048, D_FF))


def _ffn_rows(x, p, vec_ref, wg_ref, wu_ref, wd_ref, wpg_ref, wp_ref, act_scr):
    h2 = _rms(x, vec_ref[R_LN2:R_LN2 + 1, :]).astype(BF16)
    for lo, hi in FF_CHUNKS:
        g = _dot(h2, wg_ref[:, lo:hi])
        act = g * _sigmoid(g) * _dot(h2, wu_ref[:, lo:hi])
        act_scr[:, lo:hi] = act.astype(BF16)
    x = x + _dot(act_scr[...], wd_ref[...])
    h3 = _rms(x, vec_ref[R_LN3:R_LN3 + 1, :]).astype(BF16)
    gate = _sigmoid(_dot(h3, wpg_ref[...]))
    return x + gate * _dot(p.astype(BF16), wp_ref[...])


def _ffn_kernel(x_ref, p_ref, xs_ref, ps_ref, vec_ref, wg_ref, wu_ref, wd_ref, wpg_ref, wp_ref,
                o_ref, os_ref, act_scr):
    weights = (vec_ref, wg_ref, wu_ref, wd_ref, wpg_ref, wp_ref)
    o_ref[...] = _ffn_rows(x_ref[...], p_ref[...], *weights, act_scr)

    @pl.when(pl.program_id(0) == pl.num_programs(0) - 1)
    def _sample():
        ns = xs_ref.shape[0]
        os_ref[...] = _ffn_rows(xs_ref[...], ps_ref[...], *weights, act_scr.at[0:ns, :])


def _ffn(layer, x, p, xs, ps, prm):
    rows, ns = x.shape[0], xs.shape[0]
    tm = min(FFN_TM, rows)
    assert rows % tm == 0 and ns <= tm
    wl = lambda *z: lambda r: (layer,) + z
    return pl.pallas_call(
        _ffn_kernel,
        grid=(rows // tm,),
        in_specs=[
            pl.BlockSpec((tm, D_MODEL), lambda r: (r, 0)),
            pl.BlockSpec((None, tm, PLE_DIM), lambda r: (layer, r, 0)),
            _const_spec((ns, D_MODEL), lambda r: (0, 0)),
            _const_spec((None, ns, PLE_DIM), wl(0, 0)),
            _const_spec((None, VEC_ROWS, D_MODEL), wl(0, 0)),
            _const_spec((None, D_MODEL, D_FF), wl(0, 0)),
            _const_spec((None, D_MODEL, D_FF), wl(0, 0)),
            _const_spec((None, D_FF, D_MODEL), wl(0, 0)),
            _const_spec((None, D_MODEL, D_MODEL), wl(0, 0)),
            _const_spec((None, PLE_DIM, D_MODEL), wl(0, 0)),
        ],
        out_specs=[
            pl.BlockSpec((tm, D_MODEL), lambda r: (r, 0)),
            pl.BlockSpec((ns, D_MODEL), lambda r: (0, 0)),
        ],
        out_shape=[
            jax.ShapeDtypeStruct((rows, D_MODEL), F32),
            jax.ShapeDtypeStruct((ns, D_MODEL), F32),
        ],
        scratch_shapes=[pltpu.VMEM((tm, D_FF), BF16)],
        compiler_params=pltpu.CompilerParams(
            dimension_semantics=("arbitrary",), vmem_limit_bytes=VMEM_LIMIT_BYTES),
        name=f"ffn_l{layer}",
    )(x, p, xs, ps, prm["vecs"], prm["w_gate"], prm["w_up"], prm["w_down"], prm["w_ple_gate"],
      prm["w_ple"])


def _t5_bucket(dist):
    n = np.maximum(dist, 0)
    max_exact = N_BUCKETS // 2
    nf = np.maximum(n, 1).astype(np.float32)
    large = max_exact + (np.log(nf / max_exact) / math.log(MAX_DISTANCE / max_exact)
                         * (N_BUCKETS - max_exact)).astype(np.int32)
    large = np.minimum(large, N_BUCKETS - 1)
    return np.where(n < max_exact, n, large)


def _bucket_table(dist):
    return np.where((dist >= 0) & (dist < WINDOW), _t5_bucket(dist), -1).astype(np.int32)


def _regroup_heads(w, axis):
    shape = w.shape
    w = w.reshape(shape[:axis] + (N_KV_HEADS, GROUP, HEAD_DIM) + shape[axis + 1:])
    return jnp.swapaxes(w, axis, axis + 1).reshape(shape)


def _prepare(t5_table, ln1, w_in, q_gain, k_gain, sinks, w_o_attn, conv_w, conv_b, w_a, b_a,
             w_x, b_x, lam, w_o_lru, w_out, ln2, w_gate, w_up, w_down, ln3, w_ple, w_ple_gate):
    depth = w_in.shape[0]
    w_qkv = jnp.concatenate(
        [_regroup_heads(w_in[:, :, 0:Q_WIDTH], 2), w_in[:, :, Q_WIDTH:QKV_WIDTH]], axis=2).astype(BF16)

    def row(v):
        return jnp.pad(v, ((0, 0), (0, D_MODEL - v.shape[1])))[:, None, :]

    parts = [(R_LN1, row(ln1)), (R_CW0, conv_w), (R_CB, row(conv_b)), (R_BA, row(b_a)),
             (R_BX, row(b_x)), (R_LAM, row(lam)), (R_QG, row(jnp.tile(q_gain, (1, N_HEADS)))),
             (R_KG, row(jnp.tile(k_gain, (1, N_KV_HEADS)))), (R_LN2, row(ln2)), (R_LN3, row(ln3))]
    used = 0
    for first_row, part in parts:
        assert first_row == used
        used += part.shape[1]
    vecs = jnp.concatenate([part for _, part in parts]
                           + [jnp.zeros((depth, VEC_ROWS - used, D_MODEL), F32)], axis=1)
    head_id = np.arange(Q_WIDTH) // HEAD_DIM
    seg = jnp.asarray(head_id[:, None] == head_id[None, :], BF16)
    bucket_p = _bucket_table((WINDOW + np.arange(WINDOW))[:, None] - np.arange(2 * WINDOW)[None, :])
    bucket_s = _bucket_table((WINDOW - np.arange(WINDOW))[None, :])
    return {
        "t5": t5_table.reshape(-1),
        "sinks": sinks,
        "vecs": vecs,
        "seg": seg,
        "bucket_p": jnp.asarray(bucket_p),
        "bucket_s": jnp.asarray(bucket_s),
        "w_qkv": w_qkv,
        "w_in": w_in.astype(BF16),
        "w_o_attn": _regroup_heads(w_o_attn, 1).astype(BF16),
        "w_ax": jnp.concatenate([w_a, w_x], axis=-1).astype(BF16),
        "w_o_lru": w_o_lru.astype(BF16),
        "w_out": w_out.astype(BF16),
        "w_gate": w_gate.astype(BF16),
        "w_up": w_up.astype(BF16),
        "w_down": w_down.astype(BF16),
        "w_ple_gate": w_ple_gate.astype(BF16),
        "w_ple": w_ple.astype(BF16),
    }


def kernel(x_prompt, x_sample, cache_k_win, cache_v_win, state_lru_h, state_conv, p_prompt,
           p_sample, t5_table, ln1, w_in, q_gain, k_gain, sinks, w_o_attn, conv_w, conv_b, w_a,
           b_a, w_x, b_x, lam, w_o_lru, w_out, ln2, w_gate, w_up, w_down, ln3, w_ple, w_ple_gate):
    depth = w_in.shape[0]
    nb, seq, _ = x_prompt.shape
    nseq = x_sample.shape[0]
    assert x_sample.shape[1] == 1 and cache_k_win.shape[2] == WINDOW
    prm = _prepare(t5_table, ln1, w_in, q_gain, k_gain, sinks, w_o_attn, conv_w, conv_b, w_a, b_a,
                   w_x, b_x, lam, w_o_lru, w_out, ln2, w_gate, w_up, w_down, ln3, w_ple, w_ple_gate)
    ckt = jnp.transpose(cache_k_win, (0, 1, 3, 4, 2)).reshape(depth, nseq, KV_WIDTH, WINDOW)
    cvt = jnp.transpose(cache_v_win, (0, 1, 3, 4, 2)).reshape(depth, nseq, KV_WIDTH, WINDOW)
    sc = jnp.transpose(state_conv, (0, 2, 1, 3))
    pp = p_prompt.reshape(depth, nb * seq, PLE_DIM)
    ps = p_sample.reshape(depth, nseq, PLE_DIM)

    yp = x_prompt
    ys = x_sample.reshape(nseq, D_MODEL)
    outs = [[] for _ in range(6)]
    nk_buf = jnp.zeros((depth, nseq, KV_WIDTH, WINDOW), F32)
    nv_buf = jnp.zeros((depth, nseq, KV_WIDTH, WINDOW), F32)
    for layer in range(depth):
        x1, nk, nv, nh, nc = _mixer_prompt(layer, yp, prm)
        outs[0].append(nk.reshape(nb, WINDOW, N_KV_HEADS, HEAD_DIM))
        outs[1].append(nv.reshape(nb, WINDOW, N_KV_HEADS, HEAD_DIM))
        outs[2].append(nh)
        outs[3].append(nc[:, SUBLANES - (CONV_WIDTH - 1):, :])

        x1s, nhs, xr, nk_buf, nv_buf = _mixer_sample(layer, ys, ckt, cvt, state_lru_h, sc, prm,
                                                     nk_buf, nv_buf)
        yp, ys = _ffn(layer, x1.reshape(nb * seq, D_MODEL), pp, x1s, ps, prm)
        yp = yp.reshape(nb, seq, D_MODEL)
        outs[4].append(nhs)
        outs[5].append(jnp.concatenate([state_conv[layer, :, 1:], xr[:, None, :]], axis=1))

    def untranspose(buf):
        return jnp.transpose(buf.reshape(depth, nseq, N_KV_HEADS, HEAD_DIM, WINDOW), (0, 1, 4, 2, 3))

    stacked = [jnp.stack(o) for o in outs]
    return (yp, ys.reshape(nseq, 1, D_MODEL), stacked[0], stacked[1], stacked[2], stacked[3],
            untranspose(nk_buf), untranspose(nv_buf), stacked[4], stacked[5])
```

```python
import functools
import math

import numpy as np
import jax
import jax.numpy as jnp
from jax import lax
from jax.experimental import pallas as pl
from jax.experimental.pallas import tpu as pltpu

F32 = jnp.float32
BF16 = jnp.bfloat16

D_MODEL = 1024
HEAD_DIM = 64
N_HEADS = 8
N_KV_HEADS = 2
GROUP = N_HEADS // N_KV_HEADS
Q_WIDTH = N_HEADS * HEAD_DIM
KV_WIDTH = N_KV_HEADS * HEAD_DIM
WINDOW = 128
N_BUCKETS = 32
MAX_DISTANCE = 128
LRU_WIDTH = D_MODEL
LRU_HEADS = 8
LRU_BLOCK = LRU_WIDTH // LRU_HEADS
LRU_C = 8.0
CONV_WIDTH = 4
D_FF = 2816
PLE_DIM = 256
PAST_LEN = 8192
EPS = 1e-6
NEG_INF = -1e30
TINY = 1e-30
LOG2E = math.log2(math.e)

SUBLANES = 8
BF16_SUBLANES = 16
LANES = 128
VMEM_LIMIT_BYTES = 56 * 1024 * 1024
FFN_VMEM_LIMIT_BYTES = 58 * 1024 * 1024

Q0 = 0
K0 = Q0 + Q_WIDTH
V0 = K0 + KV_WIDTH
QKV_WIDTH = V0 + KV_WIDTH
XR0 = QKV_WIDTH
XG0 = XR0 + LRU_WIDTH
GA0 = XG0 + LRU_WIDTH
GL0 = GA0 + D_MODEL
IN_WIDTH = GL0 + D_MODEL

R_LN1, R_CW0, R_CB, R_BA, R_BX, R_LAM, R_QG, R_KG, R_LN2, R_LN3 = 0, 1, 5, 6, 7, 8, 9, 10, 11, 12
VEC_ROWS = 16

PROMPT_TT = 64
FFN_TM = 1024
SAMPLE_BB = 16
PROJ_CHUNK = 512
N_FFN_WEIGHTS = 5


def _dot(a, b):
    return jnp.dot(a, b, preferred_element_type=F32)


def _dot_nt(a, b):
    return lax.dot_general(a, b, (((1,), (1,)), ((), ())), preferred_element_type=F32)


def _rms(x, g):
    ms = jnp.mean(x * x, axis=-1, keepdims=True)
    return x * lax.rsqrt(ms + EPS) * g


def _seg_rms(x, seg, g):
    x2 = x * x
    ssq = _dot(x2.astype(BF16), seg)
    return x * lax.rsqrt(ssq * (1.0 / HEAD_DIM) + EPS) * g


def _sigmoid(x):
    return 0.5 * jnp.tanh(0.5 * x) + 0.5


def _softplus(z):
    return jnp.maximum(z, 0.0) + jnp.log1p(jnp.exp(-jnp.abs(z)))


def _gather_bias(bucket, t5_ref, head):
    acc = jnp.full(bucket.shape, NEG_INF, F32)
    for bkt in range(N_BUCKETS):
        acc = jnp.where(bucket == bkt, t5_ref[bkt * N_HEADS + head], acc)
    return acc


def _lru_gates(xch, wax, b_a, b_x, c_row):
    gts = _dot(xch.astype(BF16), wax)
    r = _sigmoid(gts[:, :LRU_BLOCK] + b_a)
    ig = _sigmoid(gts[:, LRU_BLOCK:] + b_x)
    log_a = c_row * r
    a = jnp.exp(log_a)
    y = 1.0 - a * a
    return a, y * lax.rsqrt(jnp.maximum(y, TINY)) * (ig * xch)


def _rows_per_step(rows, nsteps):
    need = -(-rows // nsteps)
    for per in range(BF16_SUBLANES, rows + 1, BF16_SUBLANES):
        if rows % per == 0 and per >= need:
            return per
    raise ValueError((rows, nsteps))


def _convert_specs(stacked, layer, nsteps):
    _, rows, cols = stacked.shape
    per = _rows_per_step(rows, nsteps)
    last = rows // per - 1
    return (pl.BlockSpec((None, per, cols), lambda t: (layer, jnp.minimum(t, last), 0)),
            pl.BlockSpec((per, cols), lambda t: (jnp.minimum(t, last), 0)),
            jax.ShapeDtypeStruct((rows, cols), BF16))


def _convert_blocks(src_refs, dst_refs):
    for src, dst in zip(src_refs, dst_refs):
        dst[...] = src[...].astype(BF16)


def _mixer_prompt_kernel(t5_ref, sinks_ref, x_ref, vec_ref, bucket_ref, seg_ref, wqkv_ref, win_ref,
                         woa_ref, wax_ref, wol_ref, wout_ref, *rest, nb, tt):
    ffn_f32 = rest[0:N_FFN_WEIGHTS]
    x1_ref, nk_ref, nv_ref, nh_ref, nc_ref = rest[N_FFN_WEIGHTS:N_FFN_WEIGHTS + 5]
    ffn_bf16 = rest[N_FFN_WEIGHTS + 5:2 * N_FFN_WEIGHTS + 5]
    (bias_scr, bias_t, kband, vband, cs, a_scr, b_scr, hst, y_scr, o_scr,
     proj_scr, q_scr) = rest[2 * N_FFN_WEIGHTS + 5:]
    tm = nb * tt
    t = pl.program_id(0)
    _convert_blocks(ffn_f32, ffn_bf16)

    @pl.when(t == 0)
    def _init():
        bucket = bucket_ref[...]
        for hh in range(N_HEADS):
            bias_scr[hh] = _gather_bias(bucket, t5_ref, hh) * LOG2E
        kband[...] = jnp.zeros((nb, 2 * WINDOW, KV_WIDTH), F32)
        vband[...] = jnp.zeros((nb, 2 * WINDOW, KV_WIDTH), F32)
        cs[:, 0:SUBLANES, :] = jnp.zeros((nb, SUBLANES, LRU_WIDTH), F32)
        hst[...] = jnp.zeros((nb, LRU_WIDTH), F32)

    x = x_ref[...].reshape(tm, D_MODEL)
    h = _rms(x, vec_ref[R_LN1:R_LN1 + 1, :]).astype(BF16)

    cs[:, SUBLANES:SUBLANES + tt, :] = _dot(h, win_ref[:, XR0:XG0]).reshape(nb, tt, LRU_WIDTH)
    c_row = -LRU_C * _softplus(-vec_ref[R_LAM:R_LAM + 1, :])
    n_chunks = (IN_WIDTH - XG0) // PROJ_CHUNK
    qkv = None
    for hd in range(LRU_HEADS):
        lc = slice(hd * LRU_BLOCK, (hd + 1) * LRU_BLOCK)
        xc = vec_ref[R_CB:R_CB + 1, lc]
        for j in range(CONV_WIDTH):
            off = SUBLANES - (CONV_WIDTH - 1) + j
            xc = xc + vec_ref[R_CW0 + j:R_CW0 + j + 1, lc] * cs[:, off:off + tt, lc]
        a, bb = _lru_gates(xc.reshape(tm, LRU_BLOCK), wax_ref[hd], vec_ref[R_BA:R_BA + 1, lc],
                           vec_ref[R_BX:R_BX + 1, lc], c_row[:, lc])
        for b in range(nb):
            rows = pl.ds(b, tt, stride=nb)
            a_scr[hd, rows, :] = a[b * tt:(b + 1) * tt]
            b_scr[hd, rows, :] = bb[b * tt:(b + 1) * tt]
        if hd < n_chunks:
            pc = slice(hd * PROJ_CHUNK, (hd + 1) * PROJ_CHUNK)
            proj_scr[:, pc] = _dot(h, win_ref[:, XG0 + hd * PROJ_CHUNK:XG0 + (hd + 1) * PROJ_CHUNK])
        elif hd == n_chunks:
            qkv = _dot(h, wqkv_ref[...])
    tail = cs[:, tt:tt + SUBLANES, :]
    nc_ref[...] = tail
    cs[:, 0:SUBLANES, :] = tail

    seg = seg_ref[...]
    q_scr[...] = _seg_rms(qkv[:, Q0:K0], seg,
                          vec_ref[R_QG:R_QG + 1, 0:Q_WIDTH] * (LOG2E * HEAD_DIM ** -0.5))
    kn = _seg_rms(qkv[:, K0:V0], seg[0:KV_WIDTH, 0:KV_WIDTH], vec_ref[R_KG:R_KG + 1, 0:KV_WIDTH])
    kband[:, WINDOW:WINDOW + tt, :] = kn.reshape(nb, tt, KV_WIDTH)
    vband[:, WINDOW:WINDOW + tt, :] = qkv[:, V0:QKV_WIDTH].reshape(nb, tt, KV_WIDTH)
    nk_ref[...] = kband[:, tt:tt + WINDOW, :]
    nv_ref[...] = vband[:, tt:tt + WINDOW, :]

    @pl.when(t * tt <= WINDOW)
    def _mask_bias():
        col = lax.broadcasted_iota(jnp.int32, (1, 2 * WINDOW), 1)
        colmask = jnp.where(col >= WINDOW - t * tt, 0.0, NEG_INF)
        for hh in range(N_HEADS):
            bias_t[hh] = bias_scr[hh, 0:tt, :] + colmask

    lane = lax.broadcasted_iota(jnp.int32, (1, LANES), 1)
    low = lane < HEAD_DIM
    for b in range(nb):
        kb = kband[b].astype(BF16)
        vb = vband[b].astype(BF16)
        o_kv = []
        for kh in range(N_KV_HEADS):
            keep = low if kh == 0 else jnp.logical_not(low)
            q4 = jnp.concatenate(
                [jnp.where(keep, q_scr[b * tt:(b + 1) * tt, g * LANES:(g + 1) * LANES], 0.0)
                 for g in range(GROUP)], axis=0).astype(BF16)
            s = _dot_nt(q4, kb)
            ps, invs = [], []
            for g in range(GROUP):
                hh = kh * GROUP + g
                sg = s[g * tt:(g + 1) * tt] + bias_t[hh]
                sink = sinks_ref[hh] * LOG2E
                m = jnp.maximum(jnp.max(sg, axis=-1, keepdims=True), sink)
                e = jnp.exp2(sg - m)
                invs.append(1.0 / (jnp.sum(e, axis=-1, keepdims=True) + jnp.exp2(sink - m)))
                ps.append(e.astype(BF16))
            o4 = _dot(jnp.concatenate(ps, axis=0), vb)
            o_kv.append([o4[g * tt:(g + 1) * tt] * invs[g] for g in range(GROUP)])
        for g in range(GROUP):
            og = jnp.where(low, o_kv[0][g], o_kv[1][g])
            o_scr[b * tt:(b + 1) * tt, g * LANES:(g + 1) * LANES] = og.astype(BF16)

    kband[:, 0:WINDOW, :] = kband[:, tt:tt + WINDOW, :]
    vband[:, 0:WINDOW, :] = vband[:, tt:tt + WINDOW, :]

    m_att = _sigmoid(proj_scr[:, GA0 - XG0:GL0 - XG0]) * _dot(o_scr[...], woa_ref[...])

    hs = [hst[:, hd * LRU_BLOCK:(hd + 1) * LRU_BLOCK] for hd in range(LRU_HEADS)]
    for ts in range(tt):
        rows = slice(ts * nb, (ts + 1) * nb)
        for hd in range(LRU_HEADS):
            hs[hd] = a_scr[hd, rows, :] * hs[hd] + b_scr[hd, rows, :]
            b_scr[hd, rows, :] = hs[hd]
    hfin = jnp.concatenate(hs, axis=1)
    hst[...] = hfin
    nh_ref[...] = hfin

    for hd in range(LRU_HEADS):
        lc = slice(hd * LRU_BLOCK, (hd + 1) * LRU_BLOCK)
        hseq = jnp.concatenate([b_scr[hd, pl.ds(b, tt, stride=nb), :] for b in range(nb)], axis=0)
        y_scr[:, lc] = (hseq * jax.nn.gelu(proj_scr[:, lc])).astype(BF16)

    m_all = m_att + _sigmoid(proj_scr[:, GL0 - XG0:IN_WIDTH - XG0]) * _dot(y_scr[...], wol_ref[...])
    x1 = x + _dot(m_all.astype(BF16), wout_ref[...])
    x1_ref[...] = x1.reshape(nb, tt, D_MODEL)


def _const_spec(shape, index):
    return pl.BlockSpec(shape, index, pipeline_mode=pl.Buffered(1))


def _smem_spec():
    return pl.BlockSpec(memory_space=pltpu.SMEM)


def _mixer_prompt(layer, x, w_in, prm):
    nb, seq, _ = x.shape
    tt = PROMPT_TT
    assert nb == SUBLANES and seq % tt == 0 and WINDOW % tt == 0
    tm = nb * tt
    wl = lambda *z: lambda t: (layer,) + z
    kern = functools.partial(_mixer_prompt_kernel, nb=nb, tt=tt)
    conv = [_convert_specs(prm[name], layer, seq // tt) for name in FFN_WEIGHT_NAMES]
    return pl.pallas_call(
        kern,
        grid=(seq // tt,),
        in_specs=[
            _smem_spec(), _smem_spec(),
            pl.BlockSpec((nb, tt, D_MODEL), lambda t: (0, t, 0)),
            _const_spec((None, VEC_ROWS, D_MODEL), wl(0, 0)),
            _const_spec((WINDOW, 2 * WINDOW), lambda t: (0, 0)),
            _const_spec((Q_WIDTH, Q_WIDTH), lambda t: (0, 0)),
            _const_spec((None, D_MODEL, QKV_WIDTH), wl(0, 0)),
            _const_spec((D_MODEL, IN_WIDTH), lambda t: (0, 0)),
            _const_spec((None, Q_WIDTH, D_MODEL), wl(0, 0)),
            _const_spec((None, LRU_HEADS, LRU_BLOCK, 2 * LRU_BLOCK), wl(0, 0, 0)),
            _const_spec((None, LRU_WIDTH, D_MODEL), wl(0, 0)),
            _const_spec((None, D_MODEL, D_MODEL), wl(0, 0)),
        ] + [c[0] for c in conv],
        out_specs=[
            pl.BlockSpec((nb, tt, D_MODEL), lambda t: (0, t, 0)),
            pl.BlockSpec((nb, WINDOW, KV_WIDTH), lambda t: (0, 0, 0)),
            pl.BlockSpec((nb, WINDOW, KV_WIDTH), lambda t: (0, 0, 0)),
            pl.BlockSpec((nb, LRU_WIDTH), lambda t: (0, 0)),
            pl.BlockSpec((nb, SUBLANES, LRU_WIDTH), lambda t: (0, 0, 0)),
        ] + [c[1] for c in conv],
        out_shape=[
            jax.ShapeDtypeStruct((nb, seq, D_MODEL), F32),
            jax.ShapeDtypeStruct((nb, WINDOW, KV_WIDTH), F32),
            jax.ShapeDtypeStruct((nb, WINDOW, KV_WIDTH), F32),
            jax.ShapeDtypeStruct((nb, LRU_WIDTH), F32),
            jax.ShapeDtypeStruct((nb, SUBLANES, LRU_WIDTH), F32),
        ] + [c[2] for c in conv],
        scratch_shapes=[
            pltpu.VMEM((N_HEADS, WINDOW, 2 * WINDOW), F32),
            pltpu.VMEM((N_HEADS, tt, 2 * WINDOW), F32),
            pltpu.VMEM((nb, 2 * WINDOW, KV_WIDTH), F32),
            pltpu.VMEM((nb, 2 * WINDOW, KV_WIDTH), F32),
            pltpu.VMEM((nb, tt + SUBLANES, LRU_WIDTH), F32),
            pltpu.VMEM((LRU_HEADS, tm, LRU_BLOCK), F32),
            pltpu.VMEM((LRU_HEADS, tm, LRU_BLOCK), F32),
            pltpu.VMEM((nb, LRU_WIDTH), F32),
            pltpu.VMEM((tm, LRU_WIDTH), BF16),
            pltpu.VMEM((tm, Q_WIDTH), BF16),
            pltpu.VMEM((tm, IN_WIDTH - XG0), F32),
            pltpu.VMEM((tm, Q_WIDTH), F32),
        ],
        compiler_params=pltpu.CompilerParams(
            dimension_semantics=("arbitrary",), vmem_limit_bytes=VMEM_LIMIT_BYTES),
        name=f"mixer_prompt_l{layer}",
    )(prm["t5"], prm["sinks"][layer], x, prm["vecs"], prm["bucket_p"], prm["seg"], prm["w_qkv"],
      w_in, prm["w_o_attn"], prm["w_ax"], prm["w_o_lru"], prm["w_out"],
      *[prm[name] for name in FFN_WEIGHT_NAMES])


def _mixer_sample_kernel(t5_ref, sinks_ref, x_ref, vec_ref, bucket_ref, seg_ref, wqkv_ref, win_ref,
                         woa_ref, wax_ref, wol_ref, wout_ref, ck_ref, cv_ref, h0_ref, sc_ref,
                         nk_all_ref, nv_all_ref,
                         x1_ref, nh_ref, xr_ref, nk_ref, nv_ref,
                         h_scr, q_scr, o_scr, mb_scr, kn_scr, vn_scr,
                         *, nseq, bb):
    del nk_all_ref, nv_all_ref
    i = pl.program_id(0)
    nkeys = bb * WINDOW

    @pl.when(i == 0)
    def _project():
        x = x_ref[...]
        h = _rms(x, vec_ref[R_LN1:R_LN1 + 1, :]).astype(BF16)
        h_scr[...] = h
        qkv = _dot(h, wqkv_ref[...])
        seg = seg_ref[...]
        q_scr[...] = _seg_rms(qkv[:, Q0:K0], seg,
                              vec_ref[R_QG:R_QG + 1, 0:Q_WIDTH] * (HEAD_DIM ** -0.5))
        kn_scr[...] = _seg_rms(qkv[:, K0:V0], seg[0:KV_WIDTH, 0:KV_WIDTH],
                               vec_ref[R_KG:R_KG + 1, 0:KV_WIDTH])
        vn_scr[...] = qkv[:, V0:QKV_WIDTH]
        bucket = bucket_ref[...]
        rowb = lax.broadcasted_iota(jnp.int32, (bb, nkeys), 0)
        colb = lax.broadcasted_iota(jnp.int32, (bb, nkeys), 1) // WINDOW
        for hh in range(N_HEADS):
            brow = _gather_bias(bucket, t5_ref, hh)
            brow = jnp.concatenate([brow] * bb, axis=1)
            mb_scr[hh * bb:(hh + 1) * bb, :] = jnp.where(rowb == colb, brow, NEG_INF)

    rows = pl.ds(pl.multiple_of(i * bb, bb), bb)
    lane = lax.broadcasted_iota(jnp.int32, (1, LANES), 1)
    low = lane < HEAD_DIM
    qblk = q_scr[rows, :]
    qz = []
    for kh in range(N_KV_HEADS):
        keep = low if kh == 0 else jnp.logical_not(low)
        for g in range(GROUP):
            qz.append(jnp.where(keep, qblk[:, g * LANES:(g + 1) * LANES], 0.0))
    qz = jnp.concatenate(qz, axis=0).astype(BF16)
    kt = jnp.concatenate([ck_ref[b] for b in range(bb)], axis=1).astype(BF16)
    vt = jnp.concatenate([cv_ref[b] for b in range(bb)], axis=1).astype(BF16)
    s = _dot(qz, kt) + mb_scr[...]
    kn_blk = kn_scr[rows, :]
    vn_blk = vn_scr[rows, :]
    knew = jnp.concatenate([kn_blk.astype(BF16).astype(F32)] * N_HEADS, axis=0)
    vnew = jnp.concatenate([vn_blk.astype(BF16).astype(F32)] * N_HEADS, axis=0)
    self_bias = jnp.concatenate(
        [jnp.full((bb, 1), t5_ref[hh], F32) for hh in range(N_HEADS)], axis=0)
    sink = jnp.concatenate(
        [jnp.full((bb, 1), sinks_ref[hh], F32) for hh in range(N_HEADS)], axis=0)
    s_self = jnp.sum(qz.astype(F32) * knew, axis=-1, keepdims=True) + self_bias
    m = jnp.maximum(jnp.maximum(jnp.max(s, axis=-1, keepdims=True), s_self), sink)
    e = jnp.exp(s - m)
    e_self = jnp.exp(s_self - m)
    den = jnp.sum(e, axis=-1, keepdims=True) + e_self + jnp.exp(sink - m)
    inv = 1.0 / den
    o = (_dot_nt((e * inv).astype(BF16), vt)
         + (e_self * inv).astype(BF16).astype(F32) * vnew)
    half = GROUP * bb
    for g in range(GROUP):
        og = jnp.where(low, o[g * bb:(g + 1) * bb], o[half + g * bb:half + (g + 1) * bb])
        o_scr[rows, g * LANES:(g + 1) * LANES] = og

    knt = kn_blk.T
    vnt = vn_blk.T
    last = lax.broadcasted_iota(jnp.int32, (1, WINDOW), 1) == WINDOW - 1
    for b in range(bb):
        nk_ref[b] = jnp.where(last, knt[:, b:b + 1], pltpu.roll(ck_ref[b], WINDOW - 1, 1))
        nv_ref[b] = jnp.where(last, vnt[:, b:b + 1], pltpu.roll(cv_ref[b], WINDOW - 1, 1))

    @pl.when(i == pl.num_programs(0) - 1)
    def _finish():
        x = x_ref[...]
        h = h_scr[...]
        m_att = _sigmoid(_dot(h, win_ref[:, GA0:GL0])) * _dot(o_scr[...].astype(BF16), woa_ref[...])
        xr = _dot(h, win_ref[:, XR0:XG0])
        xr_ref[...] = xr
        xc = vec_ref[R_CB:R_CB + 1, :] + vec_ref[R_CW0 + CONV_WIDTH - 1:R_CW0 + CONV_WIDTH, :] * xr
        for j in range(CONV_WIDTH - 1):
            xc = xc + vec_ref[R_CW0 + j:R_CW0 + j + 1, :] * sc_ref[j]
        c_row = -LRU_C * _softplus(-vec_ref[R_LAM:R_LAM + 1, :])
        hn = []
        for hd in range(LRU_HEADS):
            lc = slice(hd * LRU_BLOCK, (hd + 1) * LRU_BLOCK)
            a, bb_ = _lru_gates(xc[:, lc], wax_ref[hd], vec_ref[R_BA:R_BA + 1, lc],
                                vec_ref[R_BX:R_BX + 1, lc], c_row[:, lc])
            hn.append(a * h0_ref[:, lc] + bb_)
        hn = jnp.concatenate(hn, axis=1)
        nh_ref[...] = hn
        y = (hn * jax.nn.gelu(_dot(h, win_ref[:, XG0:GA0]))).astype(BF16)
        m_all = m_att + _sigmoid(_dot(h, win_ref[:, GL0:IN_WIDTH])) * _dot(y, wol_ref[...])
        x1_ref[...] = x + _dot(m_all.astype(BF16), wout_ref[...])


def _mixer_sample(layer, x, w_in, ckt, cvt, h0, sc, prm, nk_buf, nv_buf):
    depth, nseq = ckt.shape[0], ckt.shape[1]
    bb = SAMPLE_BB
    assert nseq % bb == 0
    wl = lambda *z: lambda i: (layer,) + z
    full2 = lambda i: (0, 0)
    cache_spec = pl.BlockSpec((None, bb, KV_WIDTH, WINDOW), lambda i: (layer, i, 0, 0))
    n_in = 16
    kern = functools.partial(_mixer_sample_kernel, nseq=nseq, bb=bb)
    return pl.pallas_call(
        kern,
        grid=(nseq // bb,),
        in_specs=[
            _smem_spec(), _smem_spec(),
            _const_spec((nseq, D_MODEL), full2),
            _const_spec((None, VEC_ROWS, D_MODEL), wl(0, 0)),
            _const_spec((1, WINDOW), full2),
            _const_spec((Q_WIDTH, Q_WIDTH), full2),
            _const_spec((None, D_MODEL, QKV_WIDTH), wl(0, 0)),
            _const_spec((D_MODEL, IN_WIDTH), full2),
            _const_spec((None, Q_WIDTH, D_MODEL), wl(0, 0)),
            _const_spec((None, LRU_HEADS, LRU_BLOCK, 2 * LRU_BLOCK), wl(0, 0, 0)),
            _const_spec((None, LRU_WIDTH, D_MODEL), wl(0, 0)),
            _const_spec((None, D_MODEL, D_MODEL), wl(0, 0)),
            cache_spec, cache_spec,
            _const_spec((None, nseq, LRU_WIDTH), wl(0, 0)),
            _const_spec((None, CONV_WIDTH - 1, nseq, LRU_WIDTH), wl(0, 0, 0)),
            pl.BlockSpec(memory_space=pl.ANY), pl.BlockSpec(memory_space=pl.ANY),
        ],
        out_specs=[
            pl.BlockSpec((nseq, D_MODEL), full2),
            pl.BlockSpec((nseq, LRU_WIDTH), full2),
            pl.BlockSpec((nseq, LRU_WIDTH), full2),
            cache_spec, cache_spec,
        ],
        out_shape=[
            jax.ShapeDtypeStruct((nseq, D_MODEL), F32),
            jax.ShapeDtypeStruct((nseq, LRU_WIDTH), F32),
            jax.ShapeDtypeStruct((nseq, LRU_WIDTH), F32),
            jax.ShapeDtypeStruct((depth, nseq, KV_WIDTH, WINDOW), F32),
            jax.ShapeDtypeStruct((depth, nseq, KV_WIDTH, WINDOW), F32),
        ],
        input_output_aliases={n_in: 3, n_in + 1: 4},
        scratch_shapes=[
            pltpu.VMEM((nseq, D_MODEL), BF16),
            pltpu.VMEM((nseq, Q_WIDTH), F32),
            pltpu.VMEM((nseq, Q_WIDTH), F32),
            pltpu.VMEM((N_HEADS * bb, bb * WINDOW), F32),
            pltpu.VMEM((nseq, KV_WIDTH), F32),
            pltpu.VMEM((nseq, KV_WIDTH), F32),
        ],
        compiler_params=pltpu.CompilerParams(
            dimension_semantics=("arbitrary",), vmem_limit_bytes=VMEM_LIMIT_BYTES),
        name=f"mixer_sample_l{layer}",
    )(prm["t5"], prm["sinks"][layer], x, prm["vecs"], prm["bucket_s"], prm["seg"], prm["w_qkv"],
      w_in, prm["w_o_attn"], prm["w_ax"], prm["w_o_lru"], prm["w_out"], ckt, cvt, h0, sc,
      nk_buf, nv_buf)


FF_CHUNKS = ((0, 1024), (1024, 2048), (2048, D_FF))
FFN_WEIGHT_NAMES = ("w_gate", "w_up", "w_down", "w_ple_gate", "w_ple")


def _ffn_rows(x, p, vec_ref, wg_ref, wu_ref, wd_ref, wpg_ref, wp_ref, act_scr):
    h2 = _rms(x, vec_ref[R_LN2:R_LN2 + 1, :]).astype(BF16)
    for lo, hi in FF_CHUNKS:
        g = _dot(h2, wg_ref[:, lo:hi])
        act = g * _sigmoid(g) * _dot(h2, wu_ref[:, lo:hi])
        act_scr[:, lo:hi] = act.astype(BF16)
    x = x + _dot(act_scr[...], wd_ref[...])
    h3 = _rms(x, vec_ref[R_LN3:R_LN3 + 1, :]).astype(BF16)
    gate = _sigmoid(_dot(h3, wpg_ref[...]))
    return x + gate * _dot(p.astype(BF16), wp_ref[...])


def _ffn_kernel(x_ref, p_ref, xs_ref, ps_ref, vec_ref, wg_ref, wu_ref, wd_ref, wpg_ref, wp_ref,
                *rest, convert):
    if convert:
        win_f32, o_ref, os_ref, win_bf16, act_scr = rest
        _convert_blocks([win_f32], [win_bf16])
    else:
        o_ref, os_ref, act_scr = rest
    weights = (vec_ref, wg_ref, wu_ref, wd_ref, wpg_ref, wp_ref)
    o_ref[...] = _ffn_rows(x_ref[...], p_ref[...], *weights, act_scr)

    @pl.when(pl.program_id(0) == pl.num_programs(0) - 1)
    def _sample():
        ns = xs_ref.shape[0]
        os_ref[...] = _ffn_rows(xs_ref[...], ps_ref[...], *weights, act_scr.at[0:ns, :])


def _ffn(layer, x, p, xs, ps, ffn_w, prm):
    rows, ns = x.shape[0], xs.shape[0]
    tm = min(FFN_TM, rows)
    assert rows % tm == 0 and ns <= tm
    depth = prm["w_in_f32"].shape[0]
    convert = layer + 1 < depth
    conv = [_convert_specs(prm["w_in_f32"], layer + 1, rows // tm)] if convert else []
    wl = lambda *z: lambda r: (layer,) + z
    whole = lambda r: (0, 0)
    outs = pl.pallas_call(
        functools.partial(_ffn_kernel, convert=convert),
        grid=(rows // tm,),
        in_specs=[
            pl.BlockSpec((tm, D_MODEL), lambda r: (r, 0)),
            pl.BlockSpec((None, tm, PLE_DIM), lambda r: (layer, r, 0)),
            _const_spec((ns, D_MODEL), whole),
            _const_spec((None, ns, PLE_DIM), wl(0, 0)),
            _const_spec((None, VEC_ROWS, D_MODEL), wl(0, 0)),
            _const_spec((D_MODEL, D_FF), whole),
            _const_spec((D_MODEL, D_FF), whole),
            _const_spec((D_FF, D_MODEL), whole),
            _const_spec((D_MODEL, D_MODEL), whole),
            _const_spec((PLE_DIM, D_MODEL), whole),
        ] + [c[0] for c in conv],
        out_specs=[
            pl.BlockSpec((tm, D_MODEL), lambda r: (r, 0)),
            pl.BlockSpec((ns, D_MODEL), whole),
        ] + [c[1] for c in conv],
        out_shape=[
            jax.ShapeDtypeStruct((rows, D_MODEL), F32),
            jax.ShapeDtypeStruct((ns, D_MODEL), F32),
        ] + [c[2] for c in conv],
        scratch_shapes=[pltpu.VMEM((tm, D_FF), BF16)],
        compiler_params=pltpu.CompilerParams(
            dimension_semantics=("arbitrary",), vmem_limit_bytes=FFN_VMEM_LIMIT_BYTES),
        name=f"ffn_l{layer}",
    )(x, p, xs, ps, prm["vecs"], *ffn_w, *([prm["w_in_f32"]] if convert else []))
    return outs[0], outs[1], (outs[2] if convert else None)


def _t5_bucket(dist):
    n = np.maximum(dist, 0)
    max_exact = N_BUCKETS // 2
    nf = np.maximum(n, 1).astype(np.float32)
    large = max_exact + (np.log(nf / max_exact) / math.log(MAX_DISTANCE / max_exact)
                         * (N_BUCKETS - max_exact)).astype(np.int32)
    large = np.minimum(large, N_BUCKETS - 1)
    return np.where(n < max_exact, n, large)


def _bucket_table(dist):
    return np.where((dist >= 0) & (dist < WINDOW), _t5_bucket(dist), -1).astype(np.int32)


def _regroup_heads(w, axis):
    shape = w.shape
    w = w.reshape(shape[:axis] + (N_KV_HEADS, GROUP, HEAD_DIM) + shape[axis + 1:])
    return jnp.swapaxes(w, axis, axis + 1).reshape(shape)


def _prepare(t5_table, ln1, w_in, q_gain, k_gain, sinks, w_o_attn, conv_w, conv_b, w_a, b_a,
             w_x, b_x, lam, w_o_lru, w_out, ln2, w_gate, w_up, w_down, ln3, w_ple, w_ple_gate):
    depth = w_in.shape[0]
    w_att = lax.optimization_barrier(w_in[:, :, 0:QKV_WIDTH])
    w_qkv = jnp.concatenate(
        [_regroup_heads(w_att[:, :, 0:Q_WIDTH], 2), w_att[:, :, Q_WIDTH:QKV_WIDTH]], axis=2).astype(BF16)

    def row(v):
        return jnp.pad(v, ((0, 0), (0, D_MODEL - v.shape[1])))[:, None, :]

    parts = [(R_LN1, row(ln1)), (R_CW0, conv_w), (R_CB, row(conv_b)), (R_BA, row(b_a)),
             (R_BX, row(b_x)), (R_LAM, row(lam)), (R_QG, row(jnp.tile(q_gain, (1, N_HEADS)))),
             (R_KG, row(jnp.tile(k_gain, (1, N_KV_HEADS)))), (R_LN2, row(ln2)), (R_LN3, row(ln3))]
    used = 0
    for first_row, part in parts:
        assert first_row == used
        used += part.shape[1]
    vecs = jnp.concatenate([part for _, part in parts]
                           + [jnp.zeros((depth, VEC_ROWS - used, D_MODEL), F32)], axis=1)
    head_id = np.arange(Q_WIDTH) // HEAD_DIM
    seg = jnp.asarray(head_id[:, None] == head_id[None, :], BF16)
    bucket_p = _bucket_table((WINDOW + np.arange(WINDOW))[:, None] - np.arange(2 * WINDOW)[None, :])
    bucket_s = _bucket_table((WINDOW - np.arange(WINDOW))[None, :])
    return {
        "t5": t5_table.reshape(-1),
        "sinks": sinks,
        "vecs": vecs,
        "seg": seg,
        "bucket_p": jnp.asarray(bucket_p),
        "bucket_s": jnp.asarray(bucket_s),
        "w_qkv": w_qkv,
        "w_in_f32": w_in,
        "w_in_first": lax.optimization_barrier(w_in[0]).astype(BF16),
        "w_o_attn": _regroup_heads(w_o_attn, 1).astype(BF16),
        "w_ax": jnp.concatenate([w_a, w_x], axis=-1).astype(BF16),
        "w_o_lru": w_o_lru.astype(BF16),
        "w_out": w_out.astype(BF16),
        "w_gate": w_gate,
        "w_up": w_up,
        "w_down": w_down,
        "w_ple_gate": w_ple_gate,
        "w_ple": w_ple,
    }


def kernel(x_prompt, x_sample, cache_k_win, cache_v_win, state_lru_h, state_conv, p_prompt,
           p_sample, t5_table, ln1, w_in, q_gain, k_gain, sinks, w_o_attn, conv_w, conv_b, w_a,
           b_a, w_x, b_x, lam, w_o_lru, w_out, ln2, w_gate, w_up, w_down, ln3, w_ple, w_ple_gate):
    depth = w_in.shape[0]
    nb, seq, _ = x_prompt.shape
    nseq = x_sample.shape[0]
    assert x_sample.shape[1] == 1 and cache_k_win.shape[2] == WINDOW
    prm = _prepare(t5_table, ln1, w_in, q_gain, k_gain, sinks, w_o_attn, conv_w, conv_b, w_a, b_a,
                   w_x, b_x, lam, w_o_lru, w_out, ln2, w_gate, w_up, w_down, ln3, w_ple, w_ple_gate)
    ckt = jnp.transpose(cache_k_win, (0, 1, 3, 4, 2)).reshape(depth, nseq, KV_WIDTH, WINDOW)
    cvt = jnp.transpose(cache_v_win, (0, 1, 3, 4, 2)).reshape(depth, nseq, KV_WIDTH, WINDOW)
    sc = jnp.transpose(state_conv, (0, 2, 1, 3))
    pp = p_prompt.reshape(depth, nb * seq, PLE_DIM)
    ps = p_sample.reshape(depth, nseq, PLE_DIM)

    yp = x_prompt
    ys = x_sample.reshape(nseq, D_MODEL)
    outs = [[] for _ in range(6)]
    nk_buf = jnp.zeros((depth, nseq, KV_WIDTH, WINDOW), F32)
    nv_buf = jnp.zeros((depth, nseq, KV_WIDTH, WINDOW), F32)
    w_in_l = prm["w_in_first"]
    for layer in range(depth):
        x1, nk, nv, nh, nc, *ffn_w = _mixer_prompt(layer, yp, w_in_l, prm)
        outs[0].append(nk.reshape(nb, WINDOW, N_KV_HEADS, HEAD_DIM))
        outs[1].append(nv.reshape(nb, WINDOW, N_KV_HEADS, HEAD_DIM))
        outs[2].append(nh)
        outs[3].append(nc[:, SUBLANES - (CONV_WIDTH - 1):, :])

        x1s, nhs, xr, nk_buf, nv_buf = _mixer_sample(layer, ys, w_in_l, ckt, cvt, state_lru_h, sc,
                                                     prm, nk_buf, nv_buf)
        yp, ys, w_in_l = _ffn(layer, x1.reshape(nb * seq, D_MODEL), pp, x1s, ps, ffn_w, prm)
        yp = yp.reshape(nb, seq, D_MODEL)
        outs[4].append(nhs)
        outs[5].append(jnp.concatenate([state_conv[layer, :, 1:], xr[:, None, :]], axis=1))

    def untranspose(buf):
        return jnp.transpose(buf.reshape(depth, nseq, N_KV_HEADS, HEAD_DIM, WINDOW), (0, 1, 4, 2, 3))

    stacked = [jnp.stack(o) for o in outs]
    return (yp, ys.reshape(nseq, 1, D_MODEL), stacked[0], stacked[1], stacked[2], stacked[3],
            untranspose(nk_buf), untranspose(nv_buf), stacked[4], stacked[5])
```

```python
import functools
import math

import numpy as np
import jax
import jax.numpy as jnp
from jax import lax
from jax.experimental import pallas as pl
from jax.experimental.pallas import tpu as pltpu

F32 = jnp.float32
BF16 = jnp.bfloat16

D_MODEL = 1024
HEAD_DIM = 64
N_HEADS = 8
N_KV_HEADS = 2
GROUP = N_HEADS // N_KV_HEADS
Q_WIDTH = N_HEADS * HEAD_DIM
KV_WIDTH = N_KV_HEADS * HEAD_DIM
WINDOW = 128
N_BUCKETS = 32
MAX_DISTANCE = 128
LRU_WIDTH = D_MODEL
LRU_HEADS = 8
LRU_BLOCK = LRU_WIDTH // LRU_HEADS
LRU_C = 8.0
CONV_WIDTH = 4
D_FF = 2816
PLE_DIM = 256
EPS = 1e-6
NEG_INF = -1e30
TINY = 1e-30
LOG2E = math.log2(math.e)

SUBLANES = 8
BF16_SUBLANES = 16
LANES = 128
VMEM_LIMIT_BYTES = 56 * 1024 * 1024
FFN_VMEM_LIMIT_BYTES = 58 * 1024 * 1024

Q0 = 0
K0 = Q0 + Q_WIDTH
V0 = K0 + KV_WIDTH
QKV_WIDTH = V0 + KV_WIDTH
XR0 = QKV_WIDTH
XG0 = XR0 + LRU_WIDTH
GA0 = XG0 + LRU_WIDTH
GL0 = GA0 + D_MODEL
IN_WIDTH = GL0 + D_MODEL

R_LN1, R_CW0, R_CB, R_BA, R_BX, R_LAM, R_QG, R_KG, R_LN2, R_LN3 = 0, 1, 5, 6, 7, 8, 9, 10, 11, 12
VEC_ROWS = 16

PROMPT_TT = 64
FFN_TM = 1024
SAMPLE_BB = 16
PROJ_CHUNK = 512
N_FFN_WEIGHTS = 5


def _dot(a, b):
    return jnp.dot(a, b, preferred_element_type=F32)


def _dot_nt(a, b):
    return lax.dot_general(a, b, (((1,), (1,)), ((), ())), preferred_element_type=F32)


def _rms(x, g):
    ms = jnp.mean(x * x, axis=-1, keepdims=True)
    return x * lax.rsqrt(ms + EPS) * g


def _seg_rms(x, seg, g):
    x2 = x * x
    ssq = _dot(x2.astype(BF16), seg)
    return x * lax.rsqrt(ssq * (1.0 / HEAD_DIM) + EPS) * g


def _sigmoid(x):
    return 0.5 * jnp.tanh(0.5 * x) + 0.5


def _softplus(z):
    return jnp.maximum(z, 0.0) + jnp.log1p(jnp.exp(-jnp.abs(z)))


def _gather_bias(bucket, t5_ref, head):
    acc = jnp.full(bucket.shape, NEG_INF, F32)
    for bkt in range(N_BUCKETS):
        acc = jnp.where(bucket == bkt, t5_ref[bkt * N_HEADS + head], acc)
    return acc


def _lru_gates(xch, wax, b_a, b_x, c_row):
    gts = _dot(xch.astype(BF16), wax)
    r = _sigmoid(gts[:, :LRU_BLOCK] + b_a)
    ig = _sigmoid(gts[:, LRU_BLOCK:] + b_x)
    log_a = c_row * r
    a = jnp.exp(log_a)
    y = 1.0 - a * a
    return a, y * lax.rsqrt(jnp.maximum(y, TINY)) * (ig * xch)


def _rows_per_step(rows, nsteps):
    need = -(-rows // nsteps)
    for per in range(BF16_SUBLANES, rows + 1, BF16_SUBLANES):
        if rows % per == 0 and per >= need:
            return per
    raise ValueError((rows, nsteps))


def _convert_specs(stacked, layer, nsteps):
    _, rows, cols = stacked.shape
    per = _rows_per_step(rows, nsteps)
    last = rows // per - 1
    return (pl.BlockSpec((None, per, cols), lambda t: (layer, jnp.minimum(t, last), 0)),
            pl.BlockSpec((per, cols), lambda t: (jnp.minimum(t, last), 0)),
            jax.ShapeDtypeStruct((rows, cols), BF16))


def _convert_blocks(src_refs, dst_refs):
    for src, dst in zip(src_refs, dst_refs):
        dst[...] = src[...].astype(BF16)


def _mixer_prompt_kernel(t5_ref, sinks_ref, x_ref, vec_ref, bucket_ref, seg_ref, wqkv_ref, win_ref,
                         woa_ref, wax_ref, wol_ref, wout_ref, *rest, nb, tt):
    ffn_f32 = rest[0:N_FFN_WEIGHTS]
    x1_ref, nk_ref, nv_ref, nh_ref, nc_ref = rest[N_FFN_WEIGHTS:N_FFN_WEIGHTS + 5]
    ffn_bf16 = rest[N_FFN_WEIGHTS + 5:2 * N_FFN_WEIGHTS + 5]
    (bias_scr, bias_t, kband, vband, cs, a_scr, b_scr, hst, y_scr, o_scr,
     proj_scr, q_scr) = rest[2 * N_FFN_WEIGHTS + 5:]
    tm = nb * tt
    t = pl.program_id(0)
    _convert_blocks(ffn_f32, ffn_bf16)

    @pl.when(t == 0)
    def _init():
        bucket = bucket_ref[...]
        for hh in range(N_HEADS):
            bias_scr[hh] = _gather_bias(bucket, t5_ref, hh) * LOG2E
        kband[...] = jnp.zeros((nb, 2 * WINDOW, KV_WIDTH), F32)
        vband[...] = jnp.zeros((nb, 2 * WINDOW, KV_WIDTH), F32)
        cs[:, 0:SUBLANES, :] = jnp.zeros((nb, SUBLANES, LRU_WIDTH), F32)
        hst[...] = jnp.zeros((nb, LRU_WIDTH), F32)

    x = x_ref[...].reshape(tm, D_MODEL)
    h = (x * vec_ref[R_LN1:R_LN1 + 1, :]).astype(BF16)
    rstd = lax.rsqrt(jnp.mean(x * x, axis=-1, keepdims=True) + EPS)

    cs[:, SUBLANES:SUBLANES + tt, :] = (_dot(h, win_ref[:, XR0:XG0]) * rstd).reshape(nb, tt, LRU_WIDTH)
    c_row = -LRU_C * _softplus(-vec_ref[R_LAM:R_LAM + 1, :])
    n_chunks = (IN_WIDTH - XG0) // PROJ_CHUNK
    qkv = None
    for hd in range(LRU_HEADS):
        lc = slice(hd * LRU_BLOCK, (hd + 1) * LRU_BLOCK)
        xc = vec_ref[R_CB:R_CB + 1, lc]
        for j in range(CONV_WIDTH):
            off = SUBLANES - (CONV_WIDTH - 1) + j
            xc = xc + vec_ref[R_CW0 + j:R_CW0 + j + 1, lc] * cs[:, off:off + tt, lc]
        a, bb = _lru_gates(xc.reshape(tm, LRU_BLOCK), wax_ref[hd], vec_ref[R_BA:R_BA + 1, lc],
                           vec_ref[R_BX:R_BX + 1, lc], c_row[:, lc])
        for b in range(nb):
            rows = pl.ds(b, tt, stride=nb)
            a_scr[hd, rows, :] = a[b * tt:(b + 1) * tt]
            b_scr[hd, rows, :] = bb[b * tt:(b + 1) * tt]
        if hd == 0:
            qkv = _dot(h, wqkv_ref[...]) * rstd
        elif hd == 1:
            seg = seg_ref[...]
            q_scr[...] = _seg_rms(qkv[:, Q0:K0], seg,
                                  vec_ref[R_QG:R_QG + 1, 0:Q_WIDTH] * (LOG2E * HEAD_DIM ** -0.5))
            kn = _seg_rms(qkv[:, K0:V0], seg[0:KV_WIDTH, 0:KV_WIDTH],
                          vec_ref[R_KG:R_KG + 1, 0:KV_WIDTH])
            kband[:, WINDOW:WINDOW + tt, :] = kn.reshape(nb, tt, KV_WIDTH)
            vband[:, WINDOW:WINDOW + tt, :] = qkv[:, V0:QKV_WIDTH].reshape(nb, tt, KV_WIDTH)
            nk_ref[...] = kband[:, tt:tt + WINDOW, :]
            nv_ref[...] = vband[:, tt:tt + WINDOW, :]
        if hd >= LRU_HEADS - n_chunks:
            c0 = (hd - (LRU_HEADS - n_chunks)) * PROJ_CHUNK
            proj_scr[:, c0:c0 + PROJ_CHUNK] = _dot(h, win_ref[:, XG0 + c0:XG0 + c0 + PROJ_CHUNK]) * rstd
    tail = cs[:, tt:tt + SUBLANES, :]
    nc_ref[...] = tail
    cs[:, 0:SUBLANES, :] = tail

    @pl.when(t * tt <= WINDOW)
    def _mask_bias():
        col = lax.broadcasted_iota(jnp.int32, (1, 2 * WINDOW), 1)
        colmask = jnp.where(col >= WINDOW - t * tt, 0.0, NEG_INF)
        for hh in range(N_HEADS):
            bias_t[hh] = bias_scr[hh, 0:tt, :] + colmask

    lane = lax.broadcasted_iota(jnp.int32, (1, LANES), 1)
    low = lane < HEAD_DIM
    for b in range(nb):
        kb = kband[b].astype(BF16)
        vb = vband[b].astype(BF16)
        o_kv = []
        for kh in range(N_KV_HEADS):
            keep = low if kh == 0 else jnp.logical_not(low)
            q4 = jnp.concatenate(
                [jnp.where(keep, q_scr[b * tt:(b + 1) * tt, g * LANES:(g + 1) * LANES], 0.0)
                 for g in range(GROUP)], axis=0).astype(BF16)
            s = _dot_nt(q4, kb)
            ps, invs = [], []
            for g in range(GROUP):
                hh = kh * GROUP + g
                sg = s[g * tt:(g + 1) * tt] + bias_t[hh]
                sink = sinks_ref[hh] * LOG2E
                m = jnp.maximum(jnp.max(sg, axis=-1, keepdims=True), sink)
                e = jnp.exp2(sg - m)
                invs.append(1.0 / (jnp.sum(e, axis=-1, keepdims=True) + jnp.exp2(sink - m)))
                ps.append(e.astype(BF16))
            o4 = _dot(jnp.concatenate(ps, axis=0), vb)
            o_kv.append([o4[g * tt:(g + 1) * tt] * invs[g] for g in range(GROUP)])
        for g in range(GROUP):
            og = jnp.where(low, o_kv[0][g], o_kv[1][g])
            o_scr[b * tt:(b + 1) * tt, g * LANES:(g + 1) * LANES] = og.astype(BF16)

    kband[:, 0:WINDOW, :] = kband[:, tt:tt + WINDOW, :]
    vband[:, 0:WINDOW, :] = vband[:, tt:tt + WINDOW, :]

    m_att = _sigmoid(proj_scr[:, GA0 - XG0:GL0 - XG0]) * _dot(o_scr[...], woa_ref[...])

    hs = [hst[:, hd * LRU_BLOCK:(hd + 1) * LRU_BLOCK] for hd in range(LRU_HEADS)]
    for ts in range(tt):
        rows = slice(ts * nb, (ts + 1) * nb)
        for hd in range(LRU_HEADS):
            hs[hd] = a_scr[hd, rows, :] * hs[hd] + b_scr[hd, rows, :]
            b_scr[hd, rows, :] = hs[hd]
    hfin = jnp.concatenate(hs, axis=1)
    hst[...] = hfin
    nh_ref[...] = hfin

    for hd in range(LRU_HEADS):
        lc = slice(hd * LRU_BLOCK, (hd + 1) * LRU_BLOCK)
        hseq = jnp.concatenate([b_scr[hd, pl.ds(b, tt, stride=nb), :] for b in range(nb)], axis=0)
        y_scr[:, lc] = (hseq * jax.nn.gelu(proj_scr[:, lc])).astype(BF16)

    m_all = m_att + _sigmoid(proj_scr[:, GL0 - XG0:IN_WIDTH - XG0]) * _dot(y_scr[...], wol_ref[...])
    x1 = x + _dot(m_all.astype(BF16), wout_ref[...])
    x1_ref[...] = x1.reshape(nb, tt, D_MODEL)


def _const_spec(shape, index):
    return pl.BlockSpec(shape, index, pipeline_mode=pl.Buffered(1))


def _smem_spec():
    return pl.BlockSpec(memory_space=pltpu.SMEM)


def _mixer_prompt(layer, x, w_in, prm):
    nb, seq, _ = x.shape
    tt = PROMPT_TT
    assert nb == SUBLANES and seq % tt == 0 and WINDOW % tt == 0
    tm = nb * tt
    wl = lambda *z: lambda t: (layer,) + z
    kern = functools.partial(_mixer_prompt_kernel, nb=nb, tt=tt)
    conv = [_convert_specs(prm[name], layer, seq // tt) for name in FFN_WEIGHT_NAMES]
    return pl.pallas_call(
        kern,
        grid=(seq // tt,),
        in_specs=[
            _smem_spec(), _smem_spec(),
            pl.BlockSpec((nb, tt, D_MODEL), lambda t: (0, t, 0)),
            _const_spec((None, VEC_ROWS, D_MODEL), wl(0, 0)),
            _const_spec((WINDOW, 2 * WINDOW), lambda t: (0, 0)),
            _const_spec((Q_WIDTH, Q_WIDTH), lambda t: (0, 0)),
            _const_spec((None, D_MODEL, QKV_WIDTH), wl(0, 0)),
            _const_spec((D_MODEL, IN_WIDTH), lambda t: (0, 0)),
            _const_spec((None, Q_WIDTH, D_MODEL), wl(0, 0)),
            _const_spec((None, LRU_HEADS, LRU_BLOCK, 2 * LRU_BLOCK), wl(0, 0, 0)),
            _const_spec((None, LRU_WIDTH, D_MODEL), wl(0, 0)),
            _const_spec((None, D_MODEL, D_MODEL), wl(0, 0)),
        ] + [c[0] for c in conv],
        out_specs=[
            pl.BlockSpec((nb, tt, D_MODEL), lambda t: (0, t, 0)),
            pl.BlockSpec((nb, WINDOW, KV_WIDTH), lambda t: (0, 0, 0)),
            pl.BlockSpec((nb, WINDOW, KV_WIDTH), lambda t: (0, 0, 0)),
            pl.BlockSpec((nb, LRU_WIDTH), lambda t: (0, 0)),
            pl.BlockSpec((nb, SUBLANES, LRU_WIDTH), lambda t: (0, 0, 0)),
        ] + [c[1] for c in conv],
        out_shape=[
            jax.ShapeDtypeStruct((nb, seq, D_MODEL), F32),
            jax.ShapeDtypeStruct((nb, WINDOW, KV_WIDTH), F32),
            jax.ShapeDtypeStruct((nb, WINDOW, KV_WIDTH), F32),
            jax.ShapeDtypeStruct((nb, LRU_WIDTH), F32),
            jax.ShapeDtypeStruct((nb, SUBLANES, LRU_WIDTH), F32),
        ] + [c[2] for c in conv],
        scratch_shapes=[
            pltpu.VMEM((N_HEADS, WINDOW, 2 * WINDOW), F32),
            pltpu.VMEM((N_HEADS, tt, 2 * WINDOW), F32),
            pltpu.VMEM((nb, 2 * WINDOW, KV_WIDTH), F32),
            pltpu.VMEM((nb, 2 * WINDOW, KV_WIDTH), F32),
            pltpu.VMEM((nb, tt + SUBLANES, LRU_WIDTH), F32),
            pltpu.VMEM((LRU_HEADS, tm, LRU_BLOCK), F32),
            pltpu.VMEM((LRU_HEADS, tm, LRU_BLOCK), F32),
            pltpu.VMEM((nb, LRU_WIDTH), F32),
            pltpu.VMEM((tm, LRU_WIDTH), BF16),
            pltpu.VMEM((tm, Q_WIDTH), BF16),
            pltpu.VMEM((tm, IN_WIDTH - XG0), F32),
            pltpu.VMEM((tm, Q_WIDTH), F32),
        ],
        compiler_params=pltpu.CompilerParams(
            dimension_semantics=("arbitrary",), vmem_limit_bytes=VMEM_LIMIT_BYTES),
        name=f"mixer_prompt_l{layer}",
    )(prm["t5"], prm["sinks"][layer], x, prm["vecs"], prm["bucket_p"], prm["seg"], prm["w_qkv"],
      w_in, prm["w_o_attn"], prm["w_ax"], prm["w_o_lru"], prm["w_out"],
      *[prm[name] for name in FFN_WEIGHT_NAMES])


def _mixer_sample_kernel(t5_ref, sinks_ref, x_ref, vec_ref, bucket_ref, seg_ref, wqkv_ref, win_ref,
                         woa_ref, wax_ref, wol_ref, wout_ref, ck_ref, cv_ref, h0_ref, sc_ref,
                         nk_all_ref, nv_all_ref,
                         x1_ref, nh_ref, xr_ref, nk_ref, nv_ref,
                         h_scr, q_scr, o_scr, mb_scr, kn_scr, vn_scr,
                         *, bb):
    del nk_all_ref, nv_all_ref
    i = pl.program_id(0)
    nkeys = bb * WINDOW

    @pl.when(i == 0)
    def _project():
        x = x_ref[...]
        h = _rms(x, vec_ref[R_LN1:R_LN1 + 1, :]).astype(BF16)
        h_scr[...] = h
        qkv = _dot(h, wqkv_ref[...])
        seg = seg_ref[...]
        q_scr[...] = _seg_rms(qkv[:, Q0:K0], seg,
                              vec_ref[R_QG:R_QG + 1, 0:Q_WIDTH] * (HEAD_DIM ** -0.5))
        kn_scr[...] = _seg_rms(qkv[:, K0:V0], seg[0:KV_WIDTH, 0:KV_WIDTH],
                               vec_ref[R_KG:R_KG + 1, 0:KV_WIDTH])
        vn_scr[...] = qkv[:, V0:QKV_WIDTH]
        bucket = bucket_ref[...]
        rowb = lax.broadcasted_iota(jnp.int32, (bb, nkeys), 0)
        colb = lax.broadcasted_iota(jnp.int32, (bb, nkeys), 1) // WINDOW
        for hh in range(N_HEADS):
            brow = _gather_bias(bucket, t5_ref, hh)
            brow = jnp.concatenate([brow] * bb, axis=1)
            mb_scr[hh * bb:(hh + 1) * bb, :] = jnp.where(rowb == colb, brow, NEG_INF)

    rows = pl.ds(pl.multiple_of(i * bb, bb), bb)
    lane = lax.broadcasted_iota(jnp.int32, (1, LANES), 1)
    low = lane < HEAD_DIM
    qblk = q_scr[rows, :]
    qz = []
    for kh in range(N_KV_HEADS):
        keep = low if kh == 0 else jnp.logical_not(low)
        for g in range(GROUP):
            qz.append(jnp.where(keep, qblk[:, g * LANES:(g + 1) * LANES], 0.0))
    qz = jnp.concatenate(qz, axis=0).astype(BF16)
    kt = jnp.concatenate([ck_ref[b] for b in range(bb)], axis=1).astype(BF16)
    vt = jnp.concatenate([cv_ref[b] for b in range(bb)], axis=1).astype(BF16)
    s = _dot(qz, kt) + mb_scr[...]
    kn_blk = kn_scr[rows, :]
    vn_blk = vn_scr[rows, :]
    knew = jnp.concatenate([kn_blk.astype(BF16).astype(F32)] * N_HEADS, axis=0)
    vnew = jnp.concatenate([vn_blk.astype(BF16).astype(F32)] * N_HEADS, axis=0)
    self_bias = jnp.concatenate(
        [jnp.full((bb, 1), t5_ref[hh], F32) for hh in range(N_HEADS)], axis=0)
    sink = jnp.concatenate(
        [jnp.full((bb, 1), sinks_ref[hh], F32) for hh in range(N_HEADS)], axis=0)
    s_self = jnp.sum(qz.astype(F32) * knew, axis=-1, keepdims=True) + self_bias
    m = jnp.maximum(jnp.maximum(jnp.max(s, axis=-1, keepdims=True), s_self), sink)
    e = jnp.exp(s - m)
    e_self = jnp.exp(s_self - m)
    den = jnp.sum(e, axis=-1, keepdims=True) + e_self + jnp.exp(sink - m)
    inv = 1.0 / den
    o = (_dot_nt((e * inv).astype(BF16), vt)
         + (e_self * inv).astype(BF16).astype(F32) * vnew)
    half = GROUP * bb
    for g in range(GROUP):
        og = jnp.where(low, o[g * bb:(g + 1) * bb], o[half + g * bb:half + (g + 1) * bb])
        o_scr[rows, g * LANES:(g + 1) * LANES] = og

    knt = kn_blk.T
    vnt = vn_blk.T
    last = lax.broadcasted_iota(jnp.int32, (1, WINDOW), 1) == WINDOW - 1
    for b in range(bb):
        nk_ref[b] = jnp.where(last, knt[:, b:b + 1], pltpu.roll(ck_ref[b], WINDOW - 1, 1))
        nv_ref[b] = jnp.where(last, vnt[:, b:b + 1], pltpu.roll(cv_ref[b], WINDOW - 1, 1))

    @pl.when(i == pl.num_programs(0) - 1)
    def _finish():
        x = x_ref[...]
        h = h_scr[...]
        m_att = _sigmoid(_dot(h, win_ref[:, GA0:GL0])) * _dot(o_scr[...].astype(BF16), woa_ref[...])
        xr = _dot(h, win_ref[:, XR0:XG0])
        xr_ref[...] = xr
        xc = vec_ref[R_CB:R_CB + 1, :] + vec_ref[R_CW0 + CONV_WIDTH - 1:R_CW0 + CONV_WIDTH, :] * xr
        for j in range(CONV_WIDTH - 1):
            xc = xc + vec_ref[R_CW0 + j:R_CW0 + j + 1, :] * sc_ref[j]
        c_row = -LRU_C * _softplus(-vec_ref[R_LAM:R_LAM + 1, :])
        hn = []
        for hd in range(LRU_HEADS):
            lc = slice(hd * LRU_BLOCK, (hd + 1) * LRU_BLOCK)
            a, bb_ = _lru_gates(xc[:, lc], wax_ref[hd], vec_ref[R_BA:R_BA + 1, lc],
                                vec_ref[R_BX:R_BX + 1, lc], c_row[:, lc])
            hn.append(a * h0_ref[:, lc] + bb_)
        hn = jnp.concatenate(hn, axis=1)
        nh_ref[...] = hn
        y = (hn * jax.nn.gelu(_dot(h, win_ref[:, XG0:GA0]))).astype(BF16)
        m_all = m_att + _sigmoid(_dot(h, win_ref[:, GL0:IN_WIDTH])) * _dot(y, wol_ref[...])
        x1_ref[...] = x + _dot(m_all.astype(BF16), wout_ref[...])


def _mixer_sample(layer, x, w_in, ckt, cvt, h0, sc, prm, nk_buf, nv_buf):
    depth, nseq = ckt.shape[0], ckt.shape[1]
    bb = SAMPLE_BB
    assert nseq % bb == 0
    wl = lambda *z: lambda i: (layer,) + z
    full2 = lambda i: (0, 0)
    cache_spec = pl.BlockSpec((None, bb, KV_WIDTH, WINDOW), lambda i: (layer, i, 0, 0))
    n_in = 16
    kern = functools.partial(_mixer_sample_kernel, bb=bb)
    return pl.pallas_call(
        kern,
        grid=(nseq // bb,),
        in_specs=[
            _smem_spec(), _smem_spec(),
            _const_spec((nseq, D_MODEL), full2),
            _const_spec((None, VEC_ROWS, D_MODEL), wl(0, 0)),
            _const_spec((1, WINDOW), full2),
            _const_spec((Q_WIDTH, Q_WIDTH), full2),
            _const_spec((None, D_MODEL, QKV_WIDTH), wl(0, 0)),
            _const_spec((D_MODEL, IN_WIDTH), full2),
            _const_spec((None, Q_WIDTH, D_MODEL), wl(0, 0)),
            _const_spec((None, LRU_HEADS, LRU_BLOCK, 2 * LRU_BLOCK), wl(0, 0, 0)),
            _const_spec((None, LRU_WIDTH, D_MODEL), wl(0, 0)),
            _const_spec((None, D_MODEL, D_MODEL), wl(0, 0)),
            cache_spec, cache_spec,
            _const_spec((None, nseq, LRU_WIDTH), wl(0, 0)),
            _const_spec((None, CONV_WIDTH - 1, nseq, LRU_WIDTH), wl(0, 0, 0)),
            pl.BlockSpec(memory_space=pl.ANY), pl.BlockSpec(memory_space=pl.ANY),
        ],
        out_specs=[
            pl.BlockSpec((nseq, D_MODEL), full2),
            pl.BlockSpec((nseq, LRU_WIDTH), full2),
            pl.BlockSpec((nseq, LRU_WIDTH), full2),
            cache_spec, cache_spec,
        ],
        out_shape=[
            jax.ShapeDtypeStruct((nseq, D_MODEL), F32),
            jax.ShapeDtypeStruct((nseq, LRU_WIDTH), F32),
            jax.ShapeDtypeStruct((nseq, LRU_WIDTH), F32),
            jax.ShapeDtypeStruct((depth, nseq, KV_WIDTH, WINDOW), F32),
            jax.ShapeDtypeStruct((depth, nseq, KV_WIDTH, WINDOW), F32),
        ],
        input_output_aliases={n_in: 3, n_in + 1: 4},
        scratch_shapes=[
            pltpu.VMEM((nseq, D_MODEL), BF16),
            pltpu.VMEM((nseq, Q_WIDTH), F32),
            pltpu.VMEM((nseq, Q_WIDTH), F32),
            pltpu.VMEM((N_HEADS * bb, bb * WINDOW), F32),
            pltpu.VMEM((nseq, KV_WIDTH), F32),
            pltpu.VMEM((nseq, KV_WIDTH), F32),
        ],
        compiler_params=pltpu.CompilerParams(
            dimension_semantics=("arbitrary",), vmem_limit_bytes=VMEM_LIMIT_BYTES),
        name=f"mixer_sample_l{layer}",
    )(prm["t5"], prm["sinks"][layer], x, prm["vecs"], prm["bucket_s"], prm["seg"], prm["w_qkv"],
      w_in, prm["w_o_attn"], prm["w_ax"], prm["w_o_lru"], prm["w_out"], ckt, cvt, h0, sc,
      nk_buf, nv_buf)


FF_CHUNKS = ((0, 1024), (1024, 2048), (2048, D_FF))
FFN_WEIGHT_NAMES = ("w_gate", "w_up", "w_down", "w_ple_gate", "w_ple")


def _ffn_rows(x, p, vec_ref, wg_ref, wu_ref, wd_ref, wpg_ref, wp_ref, act_scr):
    h2 = _rms(x, vec_ref[R_LN2:R_LN2 + 1, :]).astype(BF16)
    for lo, hi in FF_CHUNKS:
        g = _dot(h2, wg_ref[:, lo:hi])
        act = g * _sigmoid(g) * _dot(h2, wu_ref[:, lo:hi])
        act_scr[:, lo:hi] = act.astype(BF16)
    x = x + _dot(act_scr[...], wd_ref[...])
    h3 = _rms(x, vec_ref[R_LN3:R_LN3 + 1, :]).astype(BF16)
    gate = _sigmoid(_dot(h3, wpg_ref[...]))
    return x + gate * _dot(p.astype(BF16), wp_ref[...])


def _ffn_kernel(x_ref, p_ref, xs_ref, ps_ref, vec_ref, wg_ref, wu_ref, wd_ref, wpg_ref, wp_ref,
                *rest, convert):
    if convert:
        win_f32, o_ref, os_ref, win_bf16, act_scr = rest
        _convert_blocks([win_f32], [win_bf16])
    else:
        o_ref, os_ref, act_scr = rest
    weights = (vec_ref, wg_ref, wu_ref, wd_ref, wpg_ref, wp_ref)
    o_ref[...] = _ffn_rows(x_ref[...], p_ref[...], *weights, act_scr)

    @pl.when(pl.program_id(0) == pl.num_programs(0) - 1)
    def _sample():
        ns = xs_ref.shape[0]
        os_ref[...] = _ffn_rows(xs_ref[...], ps_ref[...], *weights, act_scr.at[0:ns, :])


def _ffn(layer, x, p, xs, ps, ffn_w, prm):
    rows, ns = x.shape[0], xs.shape[0]
    tm = min(FFN_TM, rows)
    assert rows % tm == 0 and ns <= tm
    depth = prm["w_in_f32"].shape[0]
    convert = layer + 1 < depth
    conv = [_convert_specs(prm["w_in_f32"], layer + 1, rows // tm)] if convert else []
    wl = lambda *z: lambda r: (layer,) + z
    whole = lambda r: (0, 0)
    outs = pl.pallas_call(
        functools.partial(_ffn_kernel, convert=convert),
        grid=(rows // tm,),
        in_specs=[
            pl.BlockSpec((tm, D_MODEL), lambda r: (r, 0)),
            pl.BlockSpec((None, tm, PLE_DIM), lambda r: (layer, r, 0)),
            _const_spec((ns, D_MODEL), whole),
            _const_spec((None, ns, PLE_DIM), wl(0, 0)),
            _const_spec((None, VEC_ROWS, D_MODEL), wl(0, 0)),
            _const_spec((D_MODEL, D_FF), whole),
            _const_spec((D_MODEL, D_FF), whole),
            _const_spec((D_FF, D_MODEL), whole),
            _const_spec((D_MODEL, D_MODEL), whole),
            _const_spec((PLE_DIM, D_MODEL), whole),
        ] + [c[0] for c in conv],
        out_specs=[
            pl.BlockSpec((tm, D_MODEL), lambda r: (r, 0)),
            pl.BlockSpec((ns, D_MODEL), whole),
        ] + [c[1] for c in conv],
        out_shape=[
            jax.ShapeDtypeStruct((rows, D_MODEL), F32),
            jax.ShapeDtypeStruct((ns, D_MODEL), F32),
        ] + [c[2] for c in conv],
        scratch_shapes=[pltpu.VMEM((tm, D_FF), BF16)],
        compiler_params=pltpu.CompilerParams(
            dimension_semantics=("arbitrary",), vmem_limit_bytes=FFN_VMEM_LIMIT_BYTES),
        name=f"ffn_l{layer}",
    )(x, p, xs, ps, prm["vecs"], *ffn_w, *([prm["w_in_f32"]] if convert else []))
    return outs[0], outs[1], (outs[2] if convert else None)


def _t5_bucket(dist):
    n = np.maximum(dist, 0)
    max_exact = N_BUCKETS // 2
    nf = np.maximum(n, 1).astype(np.float32)
    large = max_exact + (np.log(nf / max_exact) / math.log(MAX_DISTANCE / max_exact)
                         * (N_BUCKETS - max_exact)).astype(np.int32)
    large = np.minimum(large, N_BUCKETS - 1)
    return np.where(n < max_exact, n, large)


def _bucket_table(dist):
    return np.where((dist >= 0) & (dist < WINDOW), _t5_bucket(dist), -1).astype(np.int32)


def _regroup_heads(w, axis):
    shape = w.shape
    w = w.reshape(shape[:axis] + (N_KV_HEADS, GROUP, HEAD_DIM) + shape[axis + 1:])
    return jnp.swapaxes(w, axis, axis + 1).reshape(shape)


def _prepare(t5_table, ln1, w_in, q_gain, k_gain, sinks, w_o_attn, conv_w, conv_b, w_a, b_a,
             w_x, b_x, lam, w_o_lru, w_out, ln2, w_gate, w_up, w_down, ln3, w_ple, w_ple_gate):
    depth = w_in.shape[0]
    w_att = lax.optimization_barrier(w_in[:, :, 0:QKV_WIDTH])
    w_qkv = jnp.concatenate(
        [_regroup_heads(w_att[:, :, 0:Q_WIDTH], 2), w_att[:, :, Q_WIDTH:QKV_WIDTH]], axis=2).astype(BF16)

    def row(v):
        return jnp.pad(v, ((0, 0), (0, D_MODEL - v.shape[1])))[:, None, :]

    parts = [(R_LN1, row(ln1)), (R_CW0, conv_w), (R_CB, row(conv_b)), (R_BA, row(b_a)),
             (R_BX, row(b_x)), (R_LAM, row(lam)), (R_QG, row(jnp.tile(q_gain, (1, N_HEADS)))),
             (R_KG, row(jnp.tile(k_gain, (1, N_KV_HEADS)))), (R_LN2, row(ln2)), (R_LN3, row(ln3))]
    used = 0
    for first_row, part in parts:
        assert first_row == used
        used += part.shape[1]
    vecs = jnp.concatenate([part for _, part in parts]
                           + [jnp.zeros((depth, VEC_ROWS - used, D_MODEL), F32)], axis=1)
    head_id = np.arange(Q_WIDTH) // HEAD_DIM
    seg = jnp.asarray(head_id[:, None] == head_id[None, :], BF16)
    bucket_p = _bucket_table((WINDOW + np.arange(WINDOW))[:, None] - np.arange(2 * WINDOW)[None, :])
    bucket_s = _bucket_table((WINDOW - np.arange(WINDOW))[None, :])
    return {
        "t5": t5_table.reshape(-1),
        "sinks": sinks,
        "vecs": vecs,
        "seg": seg,
        "bucket_p": jnp.asarray(bucket_p),
        "bucket_s": jnp.asarray(bucket_s),
        "w_qkv": w_qkv,
        "w_in_f32": w_in,
        "w_in_first": lax.optimization_barrier(w_in[0]).astype(BF16),
        "w_o_attn": _regroup_heads(w_o_attn, 1).astype(BF16),
        "w_ax": jnp.concatenate([w_a, w_x], axis=-1).astype(BF16),
        "w_o_lru": w_o_lru.astype(BF16),
        "w_out": w_out.astype(BF16),
        "w_gate": w_gate,
        "w_up": w_up,
        "w_down": w_down,
        "w_ple_gate": w_ple_gate,
        "w_ple": w_ple,
    }


def kernel(x_prompt, x_sample, cache_k_win, cache_v_win, state_lru_h, state_conv, p_prompt,
           p_sample, t5_table, ln1, w_in, q_gain, k_gain, sinks, w_o_attn, conv_w, conv_b, w_a,
           b_a, w_x, b_x, lam, w_o_lru, w_out, ln2, w_gate, w_up, w_down, ln3, w_ple, w_ple_gate):
    depth = w_in.shape[0]
    nb, seq, _ = x_prompt.shape
    nseq = x_sample.shape[0]
    assert x_sample.shape[1] == 1 and cache_k_win.shape[2] == WINDOW
    prm = _prepare(t5_table, ln1, w_in, q_gain, k_gain, sinks, w_o_attn, conv_w, conv_b, w_a, b_a,
                   w_x, b_x, lam, w_o_lru, w_out, ln2, w_gate, w_up, w_down, ln3, w_ple, w_ple_gate)
    ckt = jnp.transpose(cache_k_win, (0, 1, 3, 4, 2)).reshape(depth, nseq, KV_WIDTH, WINDOW)
    cvt = jnp.transpose(cache_v_win, (0, 1, 3, 4, 2)).reshape(depth, nseq, KV_WIDTH, WINDOW)
    sc = jnp.transpose(state_conv, (0, 2, 1, 3))
    pp = p_prompt.reshape(depth, nb * seq, PLE_DIM)
    ps = p_sample.reshape(depth, nseq, PLE_DIM)

    yp = x_prompt
    ys = x_sample.reshape(nseq, D_MODEL)
    outs = [[] for _ in range(6)]
    nk_buf = jnp.zeros((depth, nseq, KV_WIDTH, WINDOW), F32)
    nv_buf = jnp.zeros((depth, nseq, KV_WIDTH, WINDOW), F32)
    w_in_l = prm["w_in_first"]
    for layer in range(depth):
        x1, nk, nv, nh, nc, *ffn_w = _mixer_prompt(layer, yp, w_in_l, prm)
        outs[0].append(nk.reshape(nb, WINDOW, N_KV_HEADS, HEAD_DIM))
        outs[1].append(nv.reshape(nb, WINDOW, N_KV_HEADS, HEAD_DIM))
        outs[2].append(nh)
        outs[3].append(nc[:, SUBLANES - (CONV_WIDTH - 1):, :])

        x1s, nhs, xr, nk_buf, nv_buf = _mixer_sample(layer, ys, w_in_l, ckt, cvt, state_lru_h, sc,
                                                     prm, nk_buf, nv_buf)
        yp, ys, w_in_l = _ffn(layer, x1.reshape(nb * seq, D_MODEL), pp, x1s, ps, ffn_w, prm)
        yp = yp.reshape(nb, seq, D_MODEL)
        outs[4].append(nhs)
        outs[5].append(jnp.concatenate([state_conv[layer, :, 1:], xr[:, None, :]], axis=1))

    def untranspose(buf):
        return jnp.transpose(buf.reshape(depth, nseq, N_KV_HEADS, HEAD_DIM, WINDOW), (0, 1, 4, 2, 3))

    stacked = [jnp.stack(o) for o in outs]
    return (yp, ys.reshape(nseq, 1, D_MODEL), stacked[0], stacked[1], stacked[2], stacked[3],
            untranspose(nk_buf), untranspose(nv_buf), stacked[4], stacked[5])
```

```python
import functools
import math

import numpy as np
import jax
import jax.numpy as jnp
from jax import lax
from jax.experimental import pallas as pl
from jax.experimental.pallas import tpu as pltpu

F32 = jnp.float32
BF16 = jnp.bfloat16

D_MODEL = 1024
HEAD_DIM = 64
N_HEADS = 8
N_KV_HEADS = 2
GROUP = N_HEADS // N_KV_HEADS
Q_WIDTH = N_HEADS * HEAD_DIM
KV_WIDTH = N_KV_HEADS * HEAD_DIM
WINDOW = 128
N_BUCKETS = 32
MAX_DISTANCE = 128
LRU_WIDTH = D_MODEL
LRU_HEADS = 8
LRU_BLOCK = LRU_WIDTH // LRU_HEADS
LRU_C = 8.0
CONV_WIDTH = 4
D_FF = 2816
PLE_DIM = 256
EPS = 1e-6
NEG_INF = -1e30
TINY = 1e-30
LOG2E = math.log2(math.e)
GELU_C = math.sqrt(2.0 / math.pi)

SUBLANES = 8
BF16_SUBLANES = 16
LANES = 128
VMEM_LIMIT_BYTES = 56 * 1024 * 1024
FFN_VMEM_LIMIT_BYTES = 58 * 1024 * 1024

Q0 = 0
K0 = Q0 + Q_WIDTH
V0 = K0 + KV_WIDTH
QKV_WIDTH = V0 + KV_WIDTH
XR0 = QKV_WIDTH
XG0 = XR0 + LRU_WIDTH
GA0 = XG0 + LRU_WIDTH
GL0 = GA0 + D_MODEL
IN_WIDTH = GL0 + D_MODEL

R_LN1, R_CW0, R_CB, R_BA, R_BX, R_LAM, R_QG, R_KG, R_LN2, R_LN3 = 0, 1, 5, 6, 7, 8, 9, 10, 11, 12
VEC_ROWS = 16

PROMPT_TT = 64
FFN_TM = 1024
SAMPLE_BB = 16
PROJ_CHUNK = 512
N_FFN_WEIGHTS = 5


def _dot(a, b):
    return jnp.dot(a, b, preferred_element_type=F32)


def _dot_nt(a, b):
    return lax.dot_general(a, b, (((1,), (1,)), ((), ())), preferred_element_type=F32)


def _rms(x, g):
    ms = jnp.mean(x * x, axis=-1, keepdims=True)
    return x * lax.rsqrt(ms + EPS) * g


def _seg_rms(x, seg, g):
    x2 = x * x
    ssq = _dot(x2.astype(BF16), seg)
    return x * lax.rsqrt(ssq * (1.0 / HEAD_DIM) + EPS) * g


def _sigmoid(x):
    return 0.5 * jnp.tanh(0.5 * x) + 0.5


def _gelu_times(x, y):
    t = jnp.tanh(x * (GELU_C + (GELU_C * 0.044715) * (x * x)))
    hxy = (0.5 * x) * y
    return hxy + hxy * t


def _softplus(z):
    return jnp.maximum(z, 0.0) + jnp.log1p(jnp.exp(-jnp.abs(z)))


def _gather_bias(bucket, t5_ref, head):
    acc = jnp.full(bucket.shape, NEG_INF, F32)
    for bkt in range(N_BUCKETS):
        acc = jnp.where(bucket == bkt, t5_ref[bkt * N_HEADS + head], acc)
    return acc


def _lru_gates(xch, wax, b_a, b_x, c_row):
    gts = _dot(xch.astype(BF16), wax)
    r = _sigmoid(gts[:, :LRU_BLOCK] + b_a)
    ig = _sigmoid(gts[:, LRU_BLOCK:] + b_x)
    log_a = c_row * r
    a = jnp.exp(log_a)
    y = 1.0 - a * a
    return a, y * lax.rsqrt(jnp.maximum(y, TINY)) * (ig * xch)


def _rows_per_step(rows, nsteps):
    need = -(-rows // nsteps)
    for per in range(BF16_SUBLANES, rows + 1, BF16_SUBLANES):
        if rows % per == 0 and per >= need:
            return per
    raise ValueError((rows, nsteps))


def _convert_specs(stacked, layer, nsteps):
    _, rows, cols = stacked.shape
    per = _rows_per_step(rows, nsteps)
    last = rows // per - 1
    return (pl.BlockSpec((None, per, cols), lambda t: (layer, jnp.minimum(t, last), 0)),
            pl.BlockSpec((per, cols), lambda t: (jnp.minimum(t, last), 0)),
            jax.ShapeDtypeStruct((rows, cols), BF16))


def _convert_blocks(src_refs, dst_refs):
    for src, dst in zip(src_refs, dst_refs):
        dst[...] = src[...].astype(BF16)


def _mixer_prompt_kernel(t5_ref, sinks_ref, x_ref, vec_ref, bucket_ref, seg_ref, wqkv_ref, win_ref,
                         woa_ref, wax_ref, wol_ref, wout_ref, *rest, nb, tt):
    ffn_f32 = rest[0:N_FFN_WEIGHTS]
    x1_ref, nk_ref, nv_ref, nh_ref, nc_ref = rest[N_FFN_WEIGHTS:N_FFN_WEIGHTS + 5]
    ffn_bf16 = rest[N_FFN_WEIGHTS + 5:2 * N_FFN_WEIGHTS + 5]
    (bias_scr, bias_t, kband, vband, cs, a_scr, b_scr, hst, y_scr, o_scr,
     proj_scr, q_scr) = rest[2 * N_FFN_WEIGHTS + 5:]
    tm = nb * tt
    t = pl.program_id(0)
    _convert_blocks(ffn_f32, ffn_bf16)

    @pl.when(t == 0)
    def _init():
        bucket = bucket_ref[...]
        for hh in range(N_HEADS):
            bias_scr[hh] = _gather_bias(bucket, t5_ref, hh) * LOG2E
        kband[...] = jnp.zeros((nb, 2 * WINDOW, KV_WIDTH), F32)
        vband[...] = jnp.zeros((nb, 2 * WINDOW, KV_WIDTH), F32)
        cs[:, 0:SUBLANES, :] = jnp.zeros((nb, SUBLANES, LRU_WIDTH), F32)
        hst[...] = jnp.zeros((nb, LRU_WIDTH), F32)

    x = x_ref[...].reshape(tm, D_MODEL)
    h = (x * vec_ref[R_LN1:R_LN1 + 1, :]).astype(BF16)
    rstd = lax.rsqrt(jnp.mean(x * x, axis=-1, keepdims=True) + EPS)

    cs[:, SUBLANES:SUBLANES + tt, :] = (_dot(h, win_ref[:, XR0:XG0]) * rstd).reshape(nb, tt, LRU_WIDTH)
    c_row = -LRU_C * _softplus(-vec_ref[R_LAM:R_LAM + 1, :])
    n_chunks = (IN_WIDTH - XG0) // PROJ_CHUNK
    qkv = None
    for hd in range(LRU_HEADS):
        lc = slice(hd * LRU_BLOCK, (hd + 1) * LRU_BLOCK)
        xc = vec_ref[R_CB:R_CB + 1, lc]
        for j in range(CONV_WIDTH):
            off = SUBLANES - (CONV_WIDTH - 1) + j
            xc = xc + vec_ref[R_CW0 + j:R_CW0 + j + 1, lc] * cs[:, off:off + tt, lc]
        a, bb = _lru_gates(xc.reshape(tm, LRU_BLOCK), wax_ref[hd], vec_ref[R_BA:R_BA + 1, lc],
                           vec_ref[R_BX:R_BX + 1, lc], c_row[:, lc])
        for b in range(nb):
            rows = pl.ds(b, tt, stride=nb)
            a_scr[hd, rows, :] = a[b * tt:(b + 1) * tt]
            b_scr[hd, rows, :] = bb[b * tt:(b + 1) * tt]
        if hd == 0:
            qkv = _dot(h, wqkv_ref[...]) * rstd
        elif hd == 1:
            seg = seg_ref[...]
            q_scr[...] = _seg_rms(qkv[:, Q0:K0], seg,
                                  vec_ref[R_QG:R_QG + 1, 0:Q_WIDTH] * (LOG2E * HEAD_DIM ** -0.5))
            kn = _seg_rms(qkv[:, K0:V0], seg[0:KV_WIDTH, 0:KV_WIDTH],
                          vec_ref[R_KG:R_KG + 1, 0:KV_WIDTH])
            kband[:, WINDOW:WINDOW + tt, :] = kn.reshape(nb, tt, KV_WIDTH)
            vband[:, WINDOW:WINDOW + tt, :] = qkv[:, V0:QKV_WIDTH].reshape(nb, tt, KV_WIDTH)
            nk_ref[...] = kband[:, tt:tt + WINDOW, :]
            nv_ref[...] = vband[:, tt:tt + WINDOW, :]
        if hd >= LRU_HEADS - n_chunks:
            c0 = (hd - (LRU_HEADS - n_chunks)) * PROJ_CHUNK
            proj_scr[:, c0:c0 + PROJ_CHUNK] = _dot(h, win_ref[:, XG0 + c0:XG0 + c0 + PROJ_CHUNK]) * rstd
    tail = cs[:, tt:tt + SUBLANES, :]
    nc_ref[...] = tail
    cs[:, 0:SUBLANES, :] = tail

    @pl.when(t * tt <= WINDOW)
    def _mask_bias():
        col = lax.broadcasted_iota(jnp.int32, (1, 2 * WINDOW), 1)
        colmask = jnp.where(col >= WINDOW - t * tt, 0.0, NEG_INF)
        for hh in range(N_HEADS):
            bias_t[hh] = bias_scr[hh, 0:tt, :] + colmask

    lane = lax.broadcasted_iota(jnp.int32, (1, LANES), 1)
    low = lane < HEAD_DIM
    for b in range(nb):
        kb = kband[b].astype(BF16)
        vb = vband[b].astype(BF16)
        o_kv = []
        for kh in range(N_KV_HEADS):
            keep = low if kh == 0 else jnp.logical_not(low)
            q4 = jnp.concatenate(
                [jnp.where(keep, q_scr[b * tt:(b + 1) * tt, g * LANES:(g + 1) * LANES], 0.0)
                 for g in range(GROUP)], axis=0).astype(BF16)
            s = _dot_nt(q4, kb)
            ps, invs = [], []
            for g in range(GROUP):
                hh = kh * GROUP + g
                sg = s[g * tt:(g + 1) * tt] + bias_t[hh]
                sink = sinks_ref[hh] * LOG2E
                m = jnp.maximum(jnp.max(sg, axis=-1, keepdims=True), sink)
                e = jnp.exp2(sg - m)
                invs.append(1.0 / (jnp.sum(e, axis=-1, keepdims=True) + jnp.exp2(sink - m)))
                ps.append(e.astype(BF16))
            o4 = _dot(jnp.concatenate(ps, axis=0), vb)
            o_kv.append([o4[g * tt:(g + 1) * tt] * invs[g] for g in range(GROUP)])
        for g in range(GROUP):
            og = jnp.where(low, o_kv[0][g], o_kv[1][g])
            o_scr[b * tt:(b + 1) * tt, g * LANES:(g + 1) * LANES] = og.astype(BF16)

    kband[:, 0:WINDOW, :] = kband[:, tt:tt + WINDOW, :]
    vband[:, 0:WINDOW, :] = vband[:, tt:tt + WINDOW, :]

    m_att = _sigmoid(proj_scr[:, GA0 - XG0:GL0 - XG0]) * _dot(o_scr[...], woa_ref[...])

    hs = [hst[:, hd * LRU_BLOCK:(hd + 1) * LRU_BLOCK] for hd in range(LRU_HEADS)]
    for ts in range(tt):
        rows = slice(ts * nb, (ts + 1) * nb)
        for hd in range(LRU_HEADS):
            hs[hd] = a_scr[hd, rows, :] * hs[hd] + b_scr[hd, rows, :]
            b_scr[hd, rows, :] = hs[hd]
    hfin = jnp.concatenate(hs, axis=1)
    hst[...] = hfin
    nh_ref[...] = hfin

    for hd in range(LRU_HEADS):
        lc = slice(hd * LRU_BLOCK, (hd + 1) * LRU_BLOCK)
        hseq = jnp.concatenate([b_scr[hd, pl.ds(b, tt, stride=nb), :] for b in range(nb)], axis=0)
        y_scr[:, lc] = _gelu_times(proj_scr[:, lc], hseq).astype(BF16)

    m_all = m_att + _sigmoid(proj_scr[:, GL0 - XG0:IN_WIDTH - XG0]) * _dot(y_scr[...], wol_ref[...])
    x1 = x + _dot(m_all.astype(BF16), wout_ref[...])
    x1_ref[...] = x1.reshape(nb, tt, D_MODEL)


def _const_spec(shape, index):
    return pl.BlockSpec(shape, index, pipeline_mode=pl.Buffered(1))


def _smem_spec():
    return pl.BlockSpec(memory_space=pltpu.SMEM)


def _mixer_prompt(layer, x, w_in, prm):
    nb, seq, _ = x.shape
    tt = PROMPT_TT
    assert nb == SUBLANES and seq % tt == 0 and WINDOW % tt == 0
    tm = nb * tt
    wl = lambda *z: lambda t: (layer,) + z
    kern = functools.partial(_mixer_prompt_kernel, nb=nb, tt=tt)
    conv = [_convert_specs(prm[name], layer, seq // tt) for name in FFN_WEIGHT_NAMES]
    return pl.pallas_call(
        kern,
        grid=(seq // tt,),
        in_specs=[
            _smem_spec(), _smem_spec(),
            pl.BlockSpec((nb, tt, D_MODEL), lambda t: (0, t, 0)),
            _const_spec((None, VEC_ROWS, D_MODEL), wl(0, 0)),
            _const_spec((WINDOW, 2 * WINDOW), lambda t: (0, 0)),
            _const_spec((Q_WIDTH, Q_WIDTH), lambda t: (0, 0)),
            _const_spec((None, D_MODEL, QKV_WIDTH), wl(0, 0)),
            _const_spec((D_MODEL, IN_WIDTH), lambda t: (0, 0)),
            _const_spec((None, Q_WIDTH, D_MODEL), wl(0, 0)),
            _const_spec((None, LRU_HEADS, LRU_BLOCK, 2 * LRU_BLOCK), wl(0, 0, 0)),
            _const_spec((None, LRU_WIDTH, D_MODEL), wl(0, 0)),
            _const_spec((None, D_MODEL, D_MODEL), wl(0, 0)),
        ] + [c[0] for c in conv],
        out_specs=[
            pl.BlockSpec((nb, tt, D_MODEL), lambda t: (0, t, 0)),
            pl.BlockSpec((nb, WINDOW, KV_WIDTH), lambda t: (0, 0, 0)),
            pl.BlockSpec((nb, WINDOW, KV_WIDTH), lambda t: (0, 0, 0)),
            pl.BlockSpec((nb, LRU_WIDTH), lambda t: (0, 0)),
            pl.BlockSpec((nb, SUBLANES, LRU_WIDTH), lambda t: (0, 0, 0)),
        ] + [c[1] for c in conv],
        out_shape=[
            jax.ShapeDtypeStruct((nb, seq, D_MODEL), F32),
            jax.ShapeDtypeStruct((nb, WINDOW, KV_WIDTH), F32),
            jax.ShapeDtypeStruct((nb, WINDOW, KV_WIDTH), F32),
            jax.ShapeDtypeStruct((nb, LRU_WIDTH), F32),
            jax.ShapeDtypeStruct((nb, SUBLANES, LRU_WIDTH), F32),
        ] + [c[2] for c in conv],
        scratch_shapes=[
            pltpu.VMEM((N_HEADS, WINDOW, 2 * WINDOW), F32),
            pltpu.VMEM((N_HEADS, tt, 2 * WINDOW), F32),
            pltpu.VMEM((nb, 2 * WINDOW, KV_WIDTH), F32),
            pltpu.VMEM((nb, 2 * WINDOW, KV_WIDTH), F32),
            pltpu.VMEM((nb, tt + SUBLANES, LRU_WIDTH), F32),
            pltpu.VMEM((LRU_HEADS, tm, LRU_BLOCK), F32),
            pltpu.VMEM((LRU_HEADS, tm, LRU_BLOCK), F32),
            pltpu.VMEM((nb, LRU_WIDTH), F32),
            pltpu.VMEM((tm, LRU_WIDTH), BF16),
            pltpu.VMEM((tm, Q_WIDTH), BF16),
            pltpu.VMEM((tm, IN_WIDTH - XG0), F32),
            pltpu.VMEM((tm, Q_WIDTH), F32),
        ],
        compiler_params=pltpu.CompilerParams(
            dimension_semantics=("arbitrary",), vmem_limit_bytes=VMEM_LIMIT_BYTES),
        name=f"mixer_prompt_l{layer}",
    )(prm["t5"], prm["sinks"][layer], x, prm["vecs"], prm["bucket_p"], prm["seg"], prm["w_qkv"],
      w_in, prm["w_o_attn"], prm["w_ax"], prm["w_o_lru"], prm["w_out"],
      *[prm[name] for name in FFN_WEIGHT_NAMES])


def _mixer_sample_kernel(t5_ref, sinks_ref, x_ref, vec_ref, bucket_ref, seg_ref, wqkv_ref, win_ref,
                         woa_ref, wax_ref, wol_ref, wout_ref, ck_ref, cv_ref, h0_ref, sc_ref,
                         nk_all_ref, nv_all_ref,
                         x1_ref, nh_ref, xr_ref, nk_ref, nv_ref,
                         h_scr, q_scr, o_scr, mb_scr, kn_scr, vn_scr,
                         *, bb):
    del nk_all_ref, nv_all_ref
    i = pl.program_id(0)
    nkeys = bb * WINDOW

    @pl.when(i == 0)
    def _project():
        x = x_ref[...]
        h = _rms(x, vec_ref[R_LN1:R_LN1 + 1, :]).astype(BF16)
        h_scr[...] = h
        qkv = _dot(h, wqkv_ref[...])
        seg = seg_ref[...]
        q_scr[...] = _seg_rms(qkv[:, Q0:K0], seg,
                              vec_ref[R_QG:R_QG + 1, 0:Q_WIDTH] * (HEAD_DIM ** -0.5))
        kn_scr[...] = _seg_rms(qkv[:, K0:V0], seg[0:KV_WIDTH, 0:KV_WIDTH],
                               vec_ref[R_KG:R_KG + 1, 0:KV_WIDTH])
        vn_scr[...] = qkv[:, V0:QKV_WIDTH]
        bucket = bucket_ref[...]
        rowb = lax.broadcasted_iota(jnp.int32, (bb, nkeys), 0)
        colb = lax.broadcasted_iota(jnp.int32, (bb, nkeys), 1) // WINDOW
        for hh in range(N_HEADS):
            brow = _gather_bias(bucket, t5_ref, hh)
            brow = jnp.concatenate([brow] * bb, axis=1)
            mb_scr[hh * bb:(hh + 1) * bb, :] = jnp.where(rowb == colb, brow, NEG_INF)

    rows = pl.ds(pl.multiple_of(i * bb, bb), bb)
    lane = lax.broadcasted_iota(jnp.int32, (1, LANES), 1)
    low = lane < HEAD_DIM
    qblk = q_scr[rows, :]
    qz = []
    for kh in range(N_KV_HEADS):
        keep = low if kh == 0 else jnp.logical_not(low)
        for g in range(GROUP):
            qz.append(jnp.where(keep, qblk[:, g * LANES:(g + 1) * LANES], 0.0))
    qz = jnp.concatenate(qz, axis=0).astype(BF16)
    kt = jnp.concatenate([ck_ref[b] for b in range(bb)], axis=1).astype(BF16)
    vt = jnp.concatenate([cv_ref[b] for b in range(bb)], axis=1).astype(BF16)
    s = _dot(qz, kt) + mb_scr[...]
    kn_blk = kn_scr[rows, :]
    vn_blk = vn_scr[rows, :]
    knew = jnp.concatenate([kn_blk.astype(BF16).astype(F32)] * N_HEADS, axis=0)
    vnew = jnp.concatenate([vn_blk.astype(BF16).astype(F32)] * N_HEADS, axis=0)
    self_bias = jnp.concatenate(
        [jnp.full((bb, 1), t5_ref[hh], F32) for hh in range(N_HEADS)], axis=0)
    sink = jnp.concatenate(
        [jnp.full((bb, 1), sinks_ref[hh], F32) for hh in range(N_HEADS)], axis=0)
    s_self = jnp.sum(qz.astype(F32) * knew, axis=-1, keepdims=True) + self_bias
    m = jnp.maximum(jnp.maximum(jnp.max(s, axis=-1, keepdims=True), s_self), sink)
    e = jnp.exp(s - m)
    e_self = jnp.exp(s_self - m)
    den = jnp.sum(e, axis=-1, keepdims=True) + e_self + jnp.exp(sink - m)
    inv = 1.0 / den
    o = (_dot_nt((e * inv).astype(BF16), vt)
         + (e_self * inv).astype(BF16).astype(F32) * vnew)
    half = GROUP * bb
    for g in range(GROUP):
        og = jnp.where(low, o[g * bb:(g + 1) * bb], o[half + g * bb:half + (g + 1) * bb])
        o_scr[rows, g * LANES:(g + 1) * LANES] = og

    knt = kn_blk.T
    vnt = vn_blk.T
    last = lax.broadcasted_iota(jnp.int32, (1, WINDOW), 1) == WINDOW - 1
    for b in range(bb):
        nk_ref[b] = jnp.where(last, knt[:, b:b + 1], pltpu.roll(ck_ref[b], WINDOW - 1, 1))
        nv_ref[b] = jnp.where(last, vnt[:, b:b + 1], pltpu.roll(cv_ref[b], WINDOW - 1, 1))

    @pl.when(i == pl.num_programs(0) - 1)
    def _finish():
        x = x_ref[...]
        h = h_scr[...]
        m_att = _sigmoid(_dot(h, win_ref[:, GA0:GL0])) * _dot(o_scr[...].astype(BF16), woa_ref[...])
        xr = _dot(h, win_ref[:, XR0:XG0])
        xr_ref[...] = xr
        xc = vec_ref[R_CB:R_CB + 1, :] + vec_ref[R_CW0 + CONV_WIDTH - 1:R_CW0 + CONV_WIDTH, :] * xr
        for j in range(CONV_WIDTH - 1):
            xc = xc + vec_ref[R_CW0 + j:R_CW0 + j + 1, :] * sc_ref[j]
        c_row = -LRU_C * _softplus(-vec_ref[R_LAM:R_LAM + 1, :])
        hn = []
        for hd in range(LRU_HEADS):
            lc = slice(hd * LRU_BLOCK, (hd + 1) * LRU_BLOCK)
            a, bb_ = _lru_gates(xc[:, lc], wax_ref[hd], vec_ref[R_BA:R_BA + 1, lc],
                                vec_ref[R_BX:R_BX + 1, lc], c_row[:, lc])
            hn.append(a * h0_ref[:, lc] + bb_)
        hn = jnp.concatenate(hn, axis=1)
        nh_ref[...] = hn
        y = _gelu_times(_dot(h, win_ref[:, XG0:GA0]), hn).astype(BF16)
        m_all = m_att + _sigmoid(_dot(h, win_ref[:, GL0:IN_WIDTH])) * _dot(y, wol_ref[...])
        x1_ref[...] = x + _dot(m_all.astype(BF16), wout_ref[...])


def _mixer_sample(layer, x, w_in, ckt, cvt, h0, sc, prm, nk_buf, nv_buf):
    depth, nseq = ckt.shape[0], ckt.shape[1]
    bb = SAMPLE_BB
    assert nseq % bb == 0
    wl = lambda *z: lambda i: (layer,) + z
    full2 = lambda i: (0, 0)
    cache_spec = pl.BlockSpec((None, bb, KV_WIDTH, WINDOW), lambda i: (layer, i, 0, 0))
    n_in = 16
    kern = functools.partial(_mixer_sample_kernel, bb=bb)
    return pl.pallas_call(
        kern,
        grid=(nseq // bb,),
        in_specs=[
            _smem_spec(), _smem_spec(),
            _const_spec((nseq, D_MODEL), full2),
            _const_spec((None, VEC_ROWS, D_MODEL), wl(0, 0)),
            _const_spec((1, WINDOW), full2),
            _const_spec((Q_WIDTH, Q_WIDTH), full2),
            _const_spec((None, D_MODEL, QKV_WIDTH), wl(0, 0)),
            _const_spec((D_MODEL, IN_WIDTH), full2),
            _const_spec((None, Q_WIDTH, D_MODEL), wl(0, 0)),
            _const_spec((None, LRU_HEADS, LRU_BLOCK, 2 * LRU_BLOCK), wl(0, 0, 0)),
            _const_spec((None, LRU_WIDTH, D_MODEL), wl(0, 0)),
            _const_spec((None, D_MODEL, D_MODEL), wl(0, 0)),
            cache_spec, cache_spec,
            _const_spec((None, nseq, LRU_WIDTH), wl(0, 0)),
            _const_spec((None, CONV_WIDTH - 1, nseq, LRU_WIDTH), wl(0, 0, 0)),
            pl.BlockSpec(memory_space=pl.ANY), pl.BlockSpec(memory_space=pl.ANY),
        ],
        out_specs=[
            pl.BlockSpec((nseq, D_MODEL), full2),
            pl.BlockSpec((nseq, LRU_WIDTH), full2),
            pl.BlockSpec((nseq, LRU_WIDTH), full2),
            cache_spec, cache_spec,
        ],
        out_shape=[
            jax.ShapeDtypeStruct((nseq, D_MODEL), F32),
            jax.ShapeDtypeStruct((nseq, LRU_WIDTH), F32),
            jax.ShapeDtypeStruct((nseq, LRU_WIDTH), F32),
            jax.ShapeDtypeStruct((depth, nseq, KV_WIDTH, WINDOW), F32),
            jax.ShapeDtypeStruct((depth, nseq, KV_WIDTH, WINDOW), F32),
        ],
        input_output_aliases={n_in: 3, n_in + 1: 4},
        scratch_shapes=[
            pltpu.VMEM((nseq, D_MODEL), BF16),
            pltpu.VMEM((nseq, Q_WIDTH), F32),
            pltpu.VMEM((nseq, Q_WIDTH), F32),
            pltpu.VMEM((N_HEADS * bb, bb * WINDOW), F32),
            pltpu.VMEM((nseq, KV_WIDTH), F32),
            pltpu.VMEM((nseq, KV_WIDTH), F32),
        ],
        compiler_params=pltpu.CompilerParams(
            dimension_semantics=("arbitrary",), vmem_limit_bytes=VMEM_LIMIT_BYTES),
        name=f"mixer_sample_l{layer}",
    )(prm["t5"], prm["sinks"][layer], x, prm["vecs"], prm["bucket_s"], prm["seg"], prm["w_qkv"],
      w_in, prm["w_o_attn"], prm["w_ax"], prm["w_o_lru"], prm["w_out"], ckt, cvt, h0, sc,
      nk_buf, nv_buf)


FF_CHUNKS = ((0, 1024), (1024, 2048), (2048, D_FF))
FFN_WEIGHT_NAMES = ("w_gate", "w_up", "w_down", "w_ple_gate", "w_ple")


def _ffn_rows(x, p, vec_ref, wg_ref, wu_ref, wd_ref, wpg_ref, wp_ref, act_scr):
    h2 = (x * vec_ref[R_LN2:R_LN2 + 1, :]).astype(BF16)
    r2 = lax.rsqrt(jnp.mean(x * x, axis=-1, keepdims=True) + EPS)
    for lo, hi in FF_CHUNKS:
        g = _dot(h2, wg_ref[:, lo:hi]) * r2
        act = g * _sigmoid(g) * (_dot(h2, wu_ref[:, lo:hi]) * r2)
        act_scr[:, lo:hi] = act.astype(BF16)
    x = x + _dot(act_scr[...], wd_ref[...])
    h3 = (x * vec_ref[R_LN3:R_LN3 + 1, :]).astype(BF16)
    r3 = lax.rsqrt(jnp.mean(x * x, axis=-1, keepdims=True) + EPS)
    gate = _sigmoid(_dot(h3, wpg_ref[...]) * r3)
    return x + gate * _dot(p.astype(BF16), wp_ref[...])


def _ffn_kernel(x_ref, p_ref, xs_ref, ps_ref, vec_ref, wg_ref, wu_ref, wd_ref, wpg_ref, wp_ref,
                *rest, convert):
    if convert:
        win_f32, o_ref, os_ref, win_bf16, act_scr = rest
        _convert_blocks([win_f32], [win_bf16])
    else:
        o_ref, os_ref, act_scr = rest
    weights = (vec_ref, wg_ref, wu_ref, wd_ref, wpg_ref, wp_ref)
    o_ref[...] = _ffn_rows(x_ref[...], p_ref[...], *weights, act_scr)

    @pl.when(pl.program_id(0) == pl.num_programs(0) - 1)
    def _sample():
        ns = xs_ref.shape[0]
        os_ref[...] = _ffn_rows(xs_ref[...], ps_ref[...], *weights, act_scr.at[0:ns, :])


def _ffn(layer, x, p, xs, ps, ffn_w, prm):
    rows, ns = x.shape[0], xs.shape[0]
    tm = min(FFN_TM, rows)
    assert rows % tm == 0 and ns <= tm
    depth = prm["w_in_f32"].shape[0]
    convert = layer + 1 < depth
    conv = [_convert_specs(prm["w_in_f32"], layer + 1, rows // tm)] if convert else []
    wl = lambda *z: lambda r: (layer,) + z
    whole = lambda r: (0, 0)
    outs = pl.pallas_call(
        functools.partial(_ffn_kernel, convert=convert),
        grid=(rows // tm,),
        in_specs=[
            pl.BlockSpec((tm, D_MODEL), lambda r: (r, 0)),
            pl.BlockSpec((None, tm, PLE_DIM), lambda r: (layer, r, 0)),
            _const_spec((ns, D_MODEL), whole),
            _const_spec((None, ns, PLE_DIM), wl(0, 0)),
            _const_spec((None, VEC_ROWS, D_MODEL), wl(0, 0)),
            _const_spec((D_MODEL, D_FF), whole),
            _const_spec((D_MODEL, D_FF), whole),
            _const_spec((D_FF, D_MODEL), whole),
            _const_spec((D_MODEL, D_MODEL), whole),
            _const_spec((PLE_DIM, D_MODEL), whole),
        ] + [c[0] for c in conv],
        out_specs=[
            pl.BlockSpec((tm, D_MODEL), lambda r: (r, 0)),
            pl.BlockSpec((ns, D_MODEL), whole),
        ] + [c[1] for c in conv],
        out_shape=[
            jax.ShapeDtypeStruct((rows, D_MODEL), F32),
            jax.ShapeDtypeStruct((ns, D_MODEL), F32),
        ] + [c[2] for c in conv],
        scratch_shapes=[pltpu.VMEM((tm, D_FF), BF16)],
        compiler_params=pltpu.CompilerParams(
            dimension_semantics=("arbitrary",), vmem_limit_bytes=FFN_VMEM_LIMIT_BYTES),
        name=f"ffn_l{layer}",
    )(x, p, xs, ps, prm["vecs"], *ffn_w, *([prm["w_in_f32"]] if convert else []))
    return outs[0], outs[1], (outs[2] if convert else None)


def _t5_bucket(dist):
    n = np.maximum(dist, 0)
    max_exact = N_BUCKETS // 2
    nf = np.maximum(n, 1).astype(np.float32)
    large = max_exact + (np.log(nf / max_exact) / math.log(MAX_DISTANCE / max_exact)
                         * (N_BUCKETS - max_exact)).astype(np.int32)
    large = np.minimum(large, N_BUCKETS - 1)
    return np.where(n < max_exact, n, large)


def _bucket_table(dist):
    return np.where((dist >= 0) & (dist < WINDOW), _t5_bucket(dist), -1).astype(np.int32)


def _regroup_heads(w, axis):
    shape = w.shape
    w = w.reshape(shape[:axis] + (N_KV_HEADS, GROUP, HEAD_DIM) + shape[axis + 1:])
    return jnp.swapaxes(w, axis, axis + 1).reshape(shape)


def _prepare(t5_table, ln1, w_in, q_gain, k_gain, sinks, w_o_attn, conv_w, conv_b, w_a, b_a,
             w_x, b_x, lam, w_o_lru, w_out, ln2, w_gate, w_up, w_down, ln3, w_ple, w_ple_gate):
    depth = w_in.shape[0]
    w_att = lax.optimization_barrier(w_in[:, :, 0:QKV_WIDTH])
    w_qkv = jnp.concatenate(
        [_regroup_heads(w_att[:, :, 0:Q_WIDTH], 2), w_att[:, :, Q_WIDTH:QKV_WIDTH]], axis=2).astype(BF16)

    def row(v):
        return jnp.pad(v, ((0, 0), (0, D_MODEL - v.shape[1])))[:, None, :]

    parts = [(R_LN1, row(ln1)), (R_CW0, conv_w), (R_CB, row(conv_b)), (R_BA, row(b_a)),
             (R_BX, row(b_x)), (R_LAM, row(lam)), (R_QG, row(jnp.tile(q_gain, (1, N_HEADS)))),
             (R_KG, row(jnp.tile(k_gain, (1, N_KV_HEADS)))), (R_LN2, row(ln2)), (R_LN3, row(ln3))]
    used = 0
    for first_row, part in parts:
        assert first_row == used
        used += part.shape[1]
    vecs = jnp.concatenate([part for _, part in parts]
                           + [jnp.zeros((depth, VEC_ROWS - used, D_MODEL), F32)], axis=1)
    head_id = np.arange(Q_WIDTH) // HEAD_DIM
    seg = jnp.asarray(head_id[:, None] == head_id[None, :], BF16)
    bucket_p = _bucket_table((WINDOW + np.arange(WINDOW))[:, None] - np.arange(2 * WINDOW)[None, :])
    bucket_s = _bucket_table((WINDOW - np.arange(WINDOW))[None, :])
    return {
        "t5": t5_table.reshape(-1),
        "sinks": sinks,
        "vecs": vecs,
        "seg": seg,
        "bucket_p": jnp.asarray(bucket_p),
        "bucket_s": jnp.asarray(bucket_s),
        "w_qkv": w_qkv,
        "w_in_f32": w_in,
        "w_in_first": lax.optimization_barrier(w_in[0]).astype(BF16),
        "w_o_attn": _regroup_heads(w_o_attn, 1).astype(BF16),
        "w_ax": jnp.concatenate([w_a, w_x], axis=-1).astype(BF16),
        "w_o_lru": w_o_lru.astype(BF16),
        "w_out": w_out.astype(BF16),
        "w_gate": w_gate,
        "w_up": w_up,
        "w_down": w_down,
        "w_ple_gate": w_ple_gate,
        "w_ple": w_ple,
    }


def kernel(x_prompt, x_sample, cache_k_win, cache_v_win, state_lru_h, state_conv, p_prompt,
           p_sample, t5_table, ln1, w_in, q_gain, k_gain, sinks, w_o_attn, conv_w, conv_b, w_a,
           b_a, w_x, b_x, lam, w_o_lru, w_out, ln2, w_gate, w_up, w_down, ln3, w_ple, w_ple_gate):
    depth = w_in.shape[0]
    nb, seq, _ = x_prompt.shape
    nseq = x_sample.shape[0]
    assert x_sample.shape[1] == 1 and cache_k_win.shape[2] == WINDOW
    prm = _prepare(t5_table, ln1, w_in, q_gain, k_gain, sinks, w_o_attn, conv_w, conv_b, w_a, b_a,
                   w_x, b_x, lam, w_o_lru, w_out, ln2, w_gate, w_up, w_down, ln3, w_ple, w_ple_gate)
    ckt = jnp.transpose(cache_k_win, (0, 1, 3, 4, 2)).reshape(depth, nseq, KV_WIDTH, WINDOW)
    cvt = jnp.transpose(cache_v_win, (0, 1, 3, 4, 2)).reshape(depth, nseq, KV_WIDTH, WINDOW)
    sc = jnp.transpose(state_conv, (0, 2, 1, 3))
    pp = p_prompt.reshape(depth, nb * seq, PLE_DIM)
    ps = p_sample.reshape(depth, nseq, PLE_DIM)

    yp = x_prompt
    ys = x_sample.reshape(nseq, D_MODEL)
    outs = [[] for _ in range(6)]
    nk_buf = jnp.zeros((depth, nseq, KV_WIDTH, WINDOW), F32)
    nv_buf = jnp.zeros((depth, nseq, KV_WIDTH, WINDOW), F32)
    w_in_l = prm["w_in_first"]
    for layer in range(depth):
        x1, nk, nv, nh, nc, *ffn_w = _mixer_prompt(layer, yp, w_in_l, prm)
        outs[0].append(nk.reshape(nb, WINDOW, N_KV_HEADS, HEAD_DIM))
        outs[1].append(nv.reshape(nb, WINDOW, N_KV_HEADS, HEAD_DIM))
        outs[2].append(nh)
        outs[3].append(nc[:, SUBLANES - (CONV_WIDTH - 1):, :])

        x1s, nhs, xr, nk_buf, nv_buf = _mixer_sample(layer, ys, w_in_l, ckt, cvt, state_lru_h, sc,
                                                     prm, nk_buf, nv_buf)
        yp, ys, w_in_l = _ffn(layer, x1.reshape(nb * seq, D_MODEL), pp, x1s, ps, ffn_w, prm)
        yp = yp.reshape(nb, seq, D_MODEL)
        outs[4].append(nhs)
        outs[5].append(jnp.concatenate([state_conv[layer, :, 1:], xr[:, None, :]], axis=1))

    def untranspose(buf):
        return jnp.transpose(buf.reshape(depth, nseq, N_KV_HEADS, HEAD_DIM, WINDOW), (0, 1, 4, 2, 3))

    stacked = [jnp.stack(o) for o in outs]
    return (yp, ys.reshape(nseq, 1, D_MODEL), stacked[0], stacked[1], stacked[2], stacked[3],
            untranspose(nk_buf), untranspose(nv_buf), stacked[4], stacked[5])
```

```python
import functools
import math

import numpy as np
import jax
import jax.numpy as jnp
from jax import lax
from jax.experimental import pallas as pl
from jax.experimental.pallas import tpu as pltpu

F32 = jnp.float32
BF16 = jnp.bfloat16

D_MODEL = 1024
HEAD_DIM = 64
N_HEADS = 8
N_KV_HEADS = 2
GROUP = N_HEADS // N_KV_HEADS
Q_WIDTH = N_HEADS * HEAD_DIM
KV_WIDTH = N_KV_HEADS * HEAD_DIM
WINDOW = 128
N_BUCKETS = 32
MAX_DISTANCE = 128
LRU_WIDTH = D_MODEL
LRU_HEADS = 8
LRU_BLOCK = LRU_WIDTH // LRU_HEADS
LRU_C = 8.0
CONV_WIDTH = 4
D_FF = 2816
PLE_DIM = 256
EPS = 1e-6
NEG_INF = -1e30
TINY = 1e-30
LOG2E = math.log2(math.e)
GELU_C = math.sqrt(2.0 / math.pi)

SUBLANES = 8
BF16_SUBLANES = 16
LANES = 128
VMEM_LIMIT_BYTES = 56 * 1024 * 1024
FFN_VMEM_LIMIT_BYTES = 58 * 1024 * 1024

Q0 = 0
K0 = Q0 + Q_WIDTH
V0 = K0 + KV_WIDTH
QKV_WIDTH = V0 + KV_WIDTH
XR0 = QKV_WIDTH
XG0 = XR0 + LRU_WIDTH
GA0 = XG0 + LRU_WIDTH
GL0 = GA0 + D_MODEL
IN_WIDTH = GL0 + D_MODEL

R_LN1, R_CW0, R_CB, R_BA, R_BX, R_LAM, R_QG, R_KG, R_LN2, R_LN3 = 0, 1, 5, 6, 7, 8, 9, 10, 11, 12
VEC_ROWS = 16

PROMPT_TT = 64
FFN_TM = 1024
SAMPLE_BB = 16
PROJ_CHUNK = 512
N_FFN_WEIGHTS = 5


def _dot(a, b):
    return jnp.dot(a, b, preferred_element_type=F32)


def _dot_nt(a, b):
    return lax.dot_general(a, b, (((1,), (1,)), ((), ())), preferred_element_type=F32)


def _rms(x, g):
    ms = jnp.mean(x * x, axis=-1, keepdims=True)
    return x * lax.rsqrt(ms + EPS) * g


def _seg_rms(x, seg, g):
    x2 = x * x
    ssq = _dot(x2.astype(BF16), seg)
    return x * lax.rsqrt(ssq * (1.0 / HEAD_DIM) + EPS) * g


def _sigmoid(x, half_scale=0.5):
    return 0.5 * jnp.tanh(x * half_scale) + 0.5


def _gelu_times(x, y):
    t = jnp.tanh(x * (GELU_C + (GELU_C * 0.044715) * (x * x)))
    hxy = (0.5 * x) * y
    return hxy + hxy * t


def _softplus(z):
    return jnp.maximum(z, 0.0) + jnp.log1p(jnp.exp(-jnp.abs(z)))


def _gather_bias(bucket, t5_ref, head):
    acc = jnp.full(bucket.shape, NEG_INF, F32)
    for bkt in range(N_BUCKETS):
        acc = jnp.where(bucket == bkt, t5_ref[bkt * N_HEADS + head], acc)
    return acc


def _lru_gates(xch, wax, b_a, b_x, c_row):
    gts = _dot(xch.astype(BF16), wax)
    r = _sigmoid(gts[:, :LRU_BLOCK] + b_a)
    ig = _sigmoid(gts[:, LRU_BLOCK:] + b_x)
    log_a = c_row * r
    a = jnp.exp(log_a)
    y = 1.0 - a * a
    return a, y * lax.rsqrt(jnp.maximum(y, TINY)) * (ig * xch)


def _rows_per_step(rows, nsteps):
    need = -(-rows // nsteps)
    for per in range(BF16_SUBLANES, rows + 1, BF16_SUBLANES):
        if rows % per == 0 and per >= need:
            return per
    raise ValueError((rows, nsteps))


def _convert_specs(stacked, layer, nsteps):
    _, rows, cols = stacked.shape
    per = _rows_per_step(rows, nsteps)
    last = rows // per - 1
    return (pl.BlockSpec((None, per, cols), lambda t: (layer, jnp.minimum(t, last), 0)),
            pl.BlockSpec((per, cols), lambda t: (jnp.minimum(t, last), 0)),
            jax.ShapeDtypeStruct((rows, cols), BF16))


def _convert_blocks(src_refs, dst_refs):
    for src, dst in zip(src_refs, dst_refs):
        dst[...] = src[...].astype(BF16)


def _mixer_prompt_kernel(t5_ref, sinks_ref, x_ref, vec_ref, bucket_ref, seg_ref, wqkv_ref, win_ref,
                         woa_ref, wax_ref, wol_ref, wout_ref, *rest, nb, tt):
    ffn_f32 = rest[0:N_FFN_WEIGHTS]
    x1_ref, nk_ref, nv_ref, nh_ref, nc_ref = rest[N_FFN_WEIGHTS:N_FFN_WEIGHTS + 5]
    ffn_bf16 = rest[N_FFN_WEIGHTS + 5:2 * N_FFN_WEIGHTS + 5]
    (bias_scr, bias_t, kband, vband, cs, a_scr, b_scr, hst, y_scr, o_scr,
     proj_scr, q_scr) = rest[2 * N_FFN_WEIGHTS + 5:]
    tm = nb * tt
    t = pl.program_id(0)
    _convert_blocks(ffn_f32, ffn_bf16)

    @pl.when(t == 0)
    def _init():
        bucket = bucket_ref[...]
        for hh in range(N_HEADS):
            bias_scr[hh] = _gather_bias(bucket, t5_ref, hh) * LOG2E
        kband[...] = jnp.zeros((nb, 2 * WINDOW, KV_WIDTH), F32)
        vband[...] = jnp.zeros((nb, 2 * WINDOW, KV_WIDTH), F32)
        cs[:, 0:SUBLANES, :] = jnp.zeros((nb, SUBLANES, LRU_WIDTH), F32)
        hst[...] = jnp.zeros((nb, LRU_WIDTH), F32)

    x = x_ref[...].reshape(tm, D_MODEL)
    h = (x * vec_ref[R_LN1:R_LN1 + 1, :]).astype(BF16)
    rstd = lax.rsqrt(jnp.mean(x * x, axis=-1, keepdims=True) + EPS)

    cs[:, SUBLANES:SUBLANES + tt, :] = (_dot(h, win_ref[:, XR0:XG0]) * rstd).reshape(nb, tt, LRU_WIDTH)
    c_row = -LRU_C * _softplus(-vec_ref[R_LAM:R_LAM + 1, :])
    n_chunks = (IN_WIDTH - XG0) // PROJ_CHUNK
    qkv = None
    for hd in range(LRU_HEADS):
        lc = slice(hd * LRU_BLOCK, (hd + 1) * LRU_BLOCK)
        xc = vec_ref[R_CB:R_CB + 1, lc]
        for j in range(CONV_WIDTH):
            off = SUBLANES - (CONV_WIDTH - 1) + j
            xc = xc + vec_ref[R_CW0 + j:R_CW0 + j + 1, lc] * cs[:, off:off + tt, lc]
        a, bb = _lru_gates(xc.reshape(tm, LRU_BLOCK), wax_ref[hd], vec_ref[R_BA:R_BA + 1, lc],
                           vec_ref[R_BX:R_BX + 1, lc], c_row[:, lc])
        for b in range(nb):
            rows = pl.ds(b, tt, stride=nb)
            a_scr[hd, rows, :] = a[b * tt:(b + 1) * tt]
            b_scr[hd, rows, :] = bb[b * tt:(b + 1) * tt]
        if hd == 0:
            qkv = _dot(h, wqkv_ref[...]) * rstd
        elif hd == 1:
            seg = seg_ref[...]
            q_scr[...] = _seg_rms(qkv[:, Q0:K0], seg,
                                  vec_ref[R_QG:R_QG + 1, 0:Q_WIDTH] * (LOG2E * HEAD_DIM ** -0.5))
            kn = _seg_rms(qkv[:, K0:V0], seg[0:KV_WIDTH, 0:KV_WIDTH],
                          vec_ref[R_KG:R_KG + 1, 0:KV_WIDTH])
            kband[:, WINDOW:WINDOW + tt, :] = kn.reshape(nb, tt, KV_WIDTH)
            vband[:, WINDOW:WINDOW + tt, :] = qkv[:, V0:QKV_WIDTH].reshape(nb, tt, KV_WIDTH)
            nk_ref[...] = kband[:, tt:tt + WINDOW, :]
            nv_ref[...] = vband[:, tt:tt + WINDOW, :]
        if hd >= LRU_HEADS - n_chunks:
            c0 = (hd - (LRU_HEADS - n_chunks)) * PROJ_CHUNK
            proj_scr[:, c0:c0 + PROJ_CHUNK] = _dot(h, win_ref[:, XG0 + c0:XG0 + c0 + PROJ_CHUNK])
    tail = cs[:, tt:tt + SUBLANES, :]
    nc_ref[...] = tail
    cs[:, 0:SUBLANES, :] = tail

    @pl.when(t * tt <= WINDOW)
    def _mask_bias():
        col = lax.broadcasted_iota(jnp.int32, (1, 2 * WINDOW), 1)
        colmask = jnp.where(col >= WINDOW - t * tt, 0.0, NEG_INF)
        for hh in range(N_HEADS):
            bias_t[hh] = bias_scr[hh, 0:tt, :] + colmask

    lane = lax.broadcasted_iota(jnp.int32, (1, LANES), 1)
    low = lane < HEAD_DIM
    for b in range(nb):
        kb = kband[b].astype(BF16)
        vb = vband[b].astype(BF16)
        o_kv = []
        for kh in range(N_KV_HEADS):
            keep = low if kh == 0 else jnp.logical_not(low)
            q4 = jnp.concatenate(
                [jnp.where(keep, q_scr[b * tt:(b + 1) * tt, g * LANES:(g + 1) * LANES], 0.0)
                 for g in range(GROUP)], axis=0).astype(BF16)
            s = _dot_nt(q4, kb)
            ps, invs = [], []
            for g in range(GROUP):
                hh = kh * GROUP + g
                sg = s[g * tt:(g + 1) * tt] + bias_t[hh]
                sink = sinks_ref[hh] * LOG2E
                m = jnp.maximum(jnp.max(sg, axis=-1, keepdims=True), sink)
                e = jnp.exp2(sg - m)
                invs.append(1.0 / (jnp.sum(e, axis=-1, keepdims=True) + jnp.exp2(sink - m)))
                ps.append(e.astype(BF16))
            o4 = _dot(jnp.concatenate(ps, axis=0), vb)
            o_kv.append([o4[g * tt:(g + 1) * tt] * invs[g] for g in range(GROUP)])
        for g in range(GROUP):
            og = jnp.where(low, o_kv[0][g], o_kv[1][g])
            o_scr[b * tt:(b + 1) * tt, g * LANES:(g + 1) * LANES] = og.astype(BF16)

    kband[:, 0:WINDOW, :] = kband[:, tt:tt + WINDOW, :]
    vband[:, 0:WINDOW, :] = vband[:, tt:tt + WINDOW, :]

    half_rstd = 0.5 * rstd
    m_att = _sigmoid(proj_scr[:, GA0 - XG0:GL0 - XG0], half_rstd) * _dot(o_scr[...], woa_ref[...])

    hs = [hst[:, hd * LRU_BLOCK:(hd + 1) * LRU_BLOCK] for hd in range(LRU_HEADS)]
    for ts in range(tt):
        rows = slice(ts * nb, (ts + 1) * nb)
        for hd in range(LRU_HEADS):
            hs[hd] = a_scr[hd, rows, :] * hs[hd] + b_scr[hd, rows, :]
            b_scr[hd, rows, :] = hs[hd]
    hfin = jnp.concatenate(hs, axis=1)
    hst[...] = hfin
    nh_ref[...] = hfin

    for hd in range(LRU_HEADS):
        lc = slice(hd * LRU_BLOCK, (hd + 1) * LRU_BLOCK)
        hseq = jnp.concatenate([b_scr[hd, pl.ds(b, tt, stride=nb), :] for b in range(nb)], axis=0)
        y_scr[:, lc] = _gelu_times(proj_scr[:, lc] * rstd, hseq).astype(BF16)

    m_all = (m_att + _sigmoid(proj_scr[:, GL0 - XG0:IN_WIDTH - XG0], half_rstd)
             * _dot(y_scr[...], wol_ref[...]))
    x1 = x + _dot(m_all.astype(BF16), wout_ref[...])
    x1_ref[...] = x1.reshape(nb, tt, D_MODEL)


def _const_spec(shape, index):
    return pl.BlockSpec(shape, index, pipeline_mode=pl.Buffered(1))


def _smem_spec():
    return pl.BlockSpec(memory_space=pltpu.SMEM)


def _mixer_prompt(layer, x, w_in, prm):
    nb, seq, _ = x.shape
    tt = PROMPT_TT
    assert nb == SUBLANES and seq % tt == 0 and WINDOW % tt == 0
    tm = nb * tt
    wl = lambda *z: lambda t: (layer,) + z
    kern = functools.partial(_mixer_prompt_kernel, nb=nb, tt=tt)
    conv = [_convert_specs(prm[name], layer, seq // tt) for name in FFN_WEIGHT_NAMES]
    return pl.pallas_call(
        kern,
        grid=(seq // tt,),
        in_specs=[
            _smem_spec(), _smem_spec(),
            pl.BlockSpec((nb, tt, D_MODEL), lambda t: (0, t, 0)),
            _const_spec((None, VEC_ROWS, D_MODEL), wl(0, 0)),
            _const_spec((WINDOW, 2 * WINDOW), lambda t: (0, 0)),
            _const_spec((Q_WIDTH, Q_WIDTH), lambda t: (0, 0)),
            _const_spec((None, D_MODEL, QKV_WIDTH), wl(0, 0)),
            _const_spec((D_MODEL, IN_WIDTH), lambda t: (0, 0)),
            _const_spec((None, Q_WIDTH, D_MODEL), wl(0, 0)),
            _const_spec((None, LRU_HEADS, LRU_BLOCK, 2 * LRU_BLOCK), wl(0, 0, 0)),
            _const_spec((None, LRU_WIDTH, D_MODEL), wl(0, 0)),
            _const_spec((None, D_MODEL, D_MODEL), wl(0, 0)),
        ] + [c[0] for c in conv],
        out_specs=[
            pl.BlockSpec((nb, tt, D_MODEL), lambda t: (0, t, 0)),
            pl.BlockSpec((nb, WINDOW, KV_WIDTH), lambda t: (0, 0, 0)),
            pl.BlockSpec((nb, WINDOW, KV_WIDTH), lambda t: (0, 0, 0)),
            pl.BlockSpec((nb, LRU_WIDTH), lambda t: (0, 0)),
            pl.BlockSpec((nb, SUBLANES, LRU_WIDTH), lambda t: (0, 0, 0)),
        ] + [c[1] for c in conv],
        out_shape=[
            jax.ShapeDtypeStruct((nb, seq, D_MODEL), F32),
            jax.ShapeDtypeStruct((nb, WINDOW, KV_WIDTH), F32),
            jax.ShapeDtypeStruct((nb, WINDOW, KV_WIDTH), F32),
            jax.ShapeDtypeStruct((nb, LRU_WIDTH), F32),
            jax.ShapeDtypeStruct((nb, SUBLANES, LRU_WIDTH), F32),
        ] + [c[2] for c in conv],
        scratch_shapes=[
            pltpu.VMEM((N_HEADS, WINDOW, 2 * WINDOW), F32),
            pltpu.VMEM((N_HEADS, tt, 2 * WINDOW), F32),
            pltpu.VMEM((nb, 2 * WINDOW, KV_WIDTH), F32),
            pltpu.VMEM((nb, 2 * WINDOW, KV_WIDTH), F32),
            pltpu.VMEM((nb, tt + SUBLANES, LRU_WIDTH), F32),
            pltpu.VMEM((LRU_HEADS, tm, LRU_BLOCK), F32),
            pltpu.VMEM((LRU_HEADS, tm, LRU_BLOCK), F32),
            pltpu.VMEM((nb, LRU_WIDTH), F32),
            pltpu.VMEM((tm, LRU_WIDTH), BF16),
            pltpu.VMEM((tm, Q_WIDTH), BF16),
            pltpu.VMEM((tm, IN_WIDTH - XG0), F32),
            pltpu.VMEM((tm, Q_WIDTH), F32),
        ],
        compiler_params=pltpu.CompilerParams(
            dimension_semantics=("arbitrary",), vmem_limit_bytes=VMEM_LIMIT_BYTES),
        name=f"mixer_prompt_l{layer}",
    )(prm["t5"], prm["sinks"][layer], x, prm["vecs"], prm["bucket_p"], prm["seg"], prm["w_qkv"],
      w_in, prm["w_o_attn"], prm["w_ax"], prm["w_o_lru"], prm["w_out"],
      *[prm[name] for name in FFN_WEIGHT_NAMES])


def _mixer_sample_kernel(t5_ref, sinks_ref, x_ref, vec_ref, bucket_ref, seg_ref, wqkv_ref, win_ref,
                         woa_ref, wax_ref, wol_ref, wout_ref, ck_ref, cv_ref, h0_ref, sc_ref,
                         nk_all_ref, nv_all_ref,
                         x1_ref, nh_ref, xr_ref, nk_ref, nv_ref,
                         h_scr, q_scr, o_scr, mb_scr, kn_scr, vn_scr,
                         *, bb):
    del nk_all_ref, nv_all_ref
    i = pl.program_id(0)
    nkeys = bb * WINDOW

    @pl.when(i == 0)
    def _project():
        x = x_ref[...]
        h = _rms(x, vec_ref[R_LN1:R_LN1 + 1, :]).astype(BF16)
        h_scr[...] = h
        qkv = _dot(h, wqkv_ref[...])
        seg = seg_ref[...]
        q_scr[...] = _seg_rms(qkv[:, Q0:K0], seg,
                              vec_ref[R_QG:R_QG + 1, 0:Q_WIDTH] * (HEAD_DIM ** -0.5))
        kn_scr[...] = _seg_rms(qkv[:, K0:V0], seg[0:KV_WIDTH, 0:KV_WIDTH],
                               vec_ref[R_KG:R_KG + 1, 0:KV_WIDTH])
        vn_scr[...] = qkv[:, V0:QKV_WIDTH]
        bucket = bucket_ref[...]
        rowb = lax.broadcasted_iota(jnp.int32, (bb, nkeys), 0)
        colb = lax.broadcasted_iota(jnp.int32, (bb, nkeys), 1) // WINDOW
        for hh in range(N_HEADS):
            brow = _gather_bias(bucket, t5_ref, hh)
            brow = jnp.concatenate([brow] * bb, axis=1)
            mb_scr[hh * bb:(hh + 1) * bb, :] = jnp.where(rowb == colb, brow, NEG_INF)

    rows = pl.ds(pl.multiple_of(i * bb, bb), bb)
    lane = lax.broadcasted_iota(jnp.int32, (1, LANES), 1)
    low = lane < HEAD_DIM
    qblk = q_scr[rows, :]
    qz = []
    for kh in range(N_KV_HEADS):
        keep = low if kh == 0 else jnp.logical_not(low)
        for g in range(GROUP):
            qz.append(jnp.where(keep, qblk[:, g * LANES:(g + 1) * LANES], 0.0))
    qz = jnp.concatenate(qz, axis=0).astype(BF16)
    kt = jnp.concatenate([ck_ref[b] for b in range(bb)], axis=1).astype(BF16)
    vt = jnp.concatenate([cv_ref[b] for b in range(bb)], axis=1).astype(BF16)
    s = _dot(qz, kt) + mb_scr[...]
    kn_blk = kn_scr[rows, :]
    vn_blk = vn_scr[rows, :]
    knew = jnp.concatenate([kn_blk.astype(BF16).astype(F32)] * N_HEADS, axis=0)
    vnew = jnp.concatenate([vn_blk.astype(BF16).astype(F32)] * N_HEADS, axis=0)
    self_bias = jnp.concatenate(
        [jnp.full((bb, 1), t5_ref[hh], F32) for hh in range(N_HEADS)], axis=0)
    sink = jnp.concatenate(
        [jnp.full((bb, 1), sinks_ref[hh], F32) for hh in range(N_HEADS)], axis=0)
    s_self = jnp.sum(qz.astype(F32) * knew, axis=-1, keepdims=True) + self_bias
    m = jnp.maximum(jnp.maximum(jnp.max(s, axis=-1, keepdims=True), s_self), sink)
    e = jnp.exp(s - m)
    e_self = jnp.exp(s_self - m)
    den = jnp.sum(e, axis=-1, keepdims=True) + e_self + jnp.exp(sink - m)
    inv = 1.0 / den
    o = (_dot_nt((e * inv).astype(BF16), vt)
         + (e_self * inv).astype(BF16).astype(F32) * vnew)
    half = GROUP * bb
    for g in range(GROUP):
        og = jnp.where(low, o[g * bb:(g + 1) * bb], o[half + g * bb:half + (g + 1) * bb])
        o_scr[rows, g * LANES:(g + 1) * LANES] = og

    knt = kn_blk.T
    vnt = vn_blk.T
    last = lax.broadcasted_iota(jnp.int32, (1, WINDOW), 1) == WINDOW - 1
    for b in range(bb):
        nk_ref[b] = jnp.where(last, knt[:, b:b + 1], pltpu.roll(ck_ref[b], WINDOW - 1, 1))
        nv_ref[b] = jnp.where(last, vnt[:, b:b + 1], pltpu.roll(cv_ref[b], WINDOW - 1, 1))

    @pl.when(i == pl.num_programs(0) - 1)
    def _finish():
        x = x_ref[...]
        h = h_scr[...]
        m_att = _sigmoid(_dot(h, win_ref[:, GA0:GL0])) * _dot(o_scr[...].astype(BF16), woa_ref[...])
        xr = _dot(h, win_ref[:, XR0:XG0])
        xr_ref[...] = xr
        xc = vec_ref[R_CB:R_CB + 1, :] + vec_ref[R_CW0 + CONV_WIDTH - 1:R_CW0 + CONV_WIDTH, :] * xr
        for j in range(CONV_WIDTH - 1):
            xc = xc + vec_ref[R_CW0 + j:R_CW0 + j + 1, :] * sc_ref[j]
        c_row = -LRU_C * _softplus(-vec_ref[R_LAM:R_LAM + 1, :])
        hn = []
        for hd in range(LRU_HEADS):
            lc = slice(hd * LRU_BLOCK, (hd + 1) * LRU_BLOCK)
            a, bb_ = _lru_gates(xc[:, lc], wax_ref[hd], vec_ref[R_BA:R_BA + 1, lc],
                                vec_ref[R_BX:R_BX + 1, lc], c_row[:, lc])
            hn.append(a * h0_ref[:, lc] + bb_)
        hn = jnp.concatenate(hn, axis=1)
        nh_ref[...] = hn
        y = _gelu_times(_dot(h, win_ref[:, XG0:GA0]), hn).astype(BF16)
        m_all = m_att + _sigmoid(_dot(h, win_ref[:, GL0:IN_WIDTH])) * _dot(y, wol_ref[...])
        x1_ref[...] = x + _dot(m_all.astype(BF16), wout_ref[...])


def _mixer_sample(layer, x, w_in, ckt, cvt, h0, sc, prm, nk_buf, nv_buf):
    depth, nseq = ckt.shape[0], ckt.shape[1]
    bb = SAMPLE_BB
    assert nseq % bb == 0
    wl = lambda *z: lambda i: (layer,) + z
    full2 = lambda i: (0, 0)
    cache_spec = pl.BlockSpec((None, bb, KV_WIDTH, WINDOW), lambda i: (layer, i, 0, 0))
    n_in = 16
    kern = functools.partial(_mixer_sample_kernel, bb=bb)
    return pl.pallas_call(
        kern,
        grid=(nseq // bb,),
        in_specs=[
            _smem_spec(), _smem_spec(),
            _const_spec((nseq, D_MODEL), full2),
            _const_spec((None, VEC_ROWS, D_MODEL), wl(0, 0)),
            _const_spec((1, WINDOW), full2),
            _const_spec((Q_WIDTH, Q_WIDTH), full2),
            _const_spec((None, D_MODEL, QKV_WIDTH), wl(0, 0)),
            _const_spec((D_MODEL, IN_WIDTH), full2),
            _const_spec((None, Q_WIDTH, D_MODEL), wl(0, 0)),
            _const_spec((None, LRU_HEADS, LRU_BLOCK, 2 * LRU_BLOCK), wl(0, 0, 0)),
            _const_spec((None, LRU_WIDTH, D_MODEL), wl(0, 0)),
            _const_spec((None, D_MODEL, D_MODEL), wl(0, 0)),
            cache_spec, cache_spec,
            _const_spec((None, nseq, LRU_WIDTH), wl(0, 0)),
            _const_spec((None, CONV_WIDTH - 1, nseq, LRU_WIDTH), wl(0, 0, 0)),
            pl.BlockSpec(memory_space=pl.ANY), pl.BlockSpec(memory_space=pl.ANY),
        ],
        out_specs=[
            pl.BlockSpec((nseq, D_MODEL), full2),
            pl.BlockSpec((nseq, LRU_WIDTH), full2),
            pl.BlockSpec((nseq, LRU_WIDTH), full2),
            cache_spec, cache_spec,
        ],
        out_shape=[
            jax.ShapeDtypeStruct((nseq, D_MODEL), F32),
            jax.ShapeDtypeStruct((nseq, LRU_WIDTH), F32),
            jax.ShapeDtypeStruct((nseq, LRU_WIDTH), F32),
            jax.ShapeDtypeStruct((depth, nseq, KV_WIDTH, WINDOW), F32),
            jax.ShapeDtypeStruct((depth, nseq, KV_WIDTH, WINDOW), F32),
        ],
        input_output_aliases={n_in: 3, n_in + 1: 4},
        scratch_shapes=[
            pltpu.VMEM((nseq, D_MODEL), BF16),
            pltpu.VMEM((nseq, Q_WIDTH), F32),
            pltpu.VMEM((nseq, Q_WIDTH), F32),
            pltpu.VMEM((N_HEADS * bb, bb * WINDOW), F32),
            pltpu.VMEM((nseq, KV_WIDTH), F32),
            pltpu.VMEM((nseq, KV_WIDTH), F32),
        ],
        compiler_params=pltpu.CompilerParams(
            dimension_semantics=("arbitrary",), vmem_limit_bytes=VMEM_LIMIT_BYTES),
        name=f"mixer_sample_l{layer}",
    )(prm["t5"], prm["sinks"][layer], x, prm["vecs"], prm["bucket_s"], prm["seg"], prm["w_qkv"],
      w_in, prm["w_o_attn"], prm["w_ax"], prm["w_o_lru"], prm["w_out"], ckt, cvt, h0, sc,
      nk_buf, nv_buf)


FF_CHUNKS = ((0, 1024), (1024, 2048), (2048, D_FF))
FFN_WEIGHT_NAMES = ("w_gate", "w_up", "w_down", "w_ple_gate", "w_ple")


def _ffn_rows(x, p, vec_ref, wg_ref, wu_ref, wd_ref, wpg_ref, wp_ref, act_scr):
    h2 = (x * vec_ref[R_LN2:R_LN2 + 1, :]).astype(BF16)
    r2 = lax.rsqrt(jnp.mean(x * x, axis=-1, keepdims=True) + EPS)
    for lo, hi in FF_CHUNKS:
        g = _dot(h2, wg_ref[:, lo:hi]) * r2
        act = g * _sigmoid(g) * (_dot(h2, wu_ref[:, lo:hi]) * r2)
        act_scr[:, lo:hi] = act.astype(BF16)
    x = x + _dot(act_scr[...], wd_ref[...])
    h3 = (x * vec_ref[R_LN3:R_LN3 + 1, :]).astype(BF16)
    r3 = lax.rsqrt(jnp.mean(x * x, axis=-1, keepdims=True) + EPS)
    gate = _sigmoid(_dot(h3, wpg_ref[...]) * r3)
    return x + gate * _dot(p.astype(BF16), wp_ref[...])


def _ffn_kernel(x_ref, p_ref, xs_ref, ps_ref, vec_ref, wg_ref, wu_ref, wd_ref, wpg_ref, wp_ref,
                *rest, convert):
    if convert:
        win_f32, o_ref, os_ref, win_bf16, act_scr = rest
        _convert_blocks([win_f32], [win_bf16])
    else:
        o_ref, os_ref, act_scr = rest
    weights = (vec_ref, wg_ref, wu_ref, wd_ref, wpg_ref, wp_ref)
    o_ref[...] = _ffn_rows(x_ref[...], p_ref[...], *weights, act_scr)

    @pl.when(pl.program_id(0) == pl.num_programs(0) - 1)
    def _sample():
        ns = xs_ref.shape[0]
        os_ref[...] = _ffn_rows(xs_ref[...], ps_ref[...], *weights, act_scr.at[0:ns, :])


def _ffn(layer, x, p, xs, ps, ffn_w, prm):
    rows, ns = x.shape[0], xs.shape[0]
    tm = min(FFN_TM, rows)
    assert rows % tm == 0 and ns <= tm
    depth = prm["w_in_f32"].shape[0]
    convert = layer + 1 < depth
    conv = [_convert_specs(prm["w_in_f32"], layer + 1, rows // tm)] if convert else []
    wl = lambda *z: lambda r: (layer,) + z
    whole = lambda r: (0, 0)
    outs = pl.pallas_call(
        functools.partial(_ffn_kernel, convert=convert),
        grid=(rows // tm,),
        in_specs=[
            pl.BlockSpec((tm, D_MODEL), lambda r: (r, 0)),
            pl.BlockSpec((None, tm, PLE_DIM), lambda r: (layer, r, 0)),
            _const_spec((ns, D_MODEL), whole),
            _const_spec((None, ns, PLE_DIM), wl(0, 0)),
            _const_spec((None, VEC_ROWS, D_MODEL), wl(0, 0)),
            _const_spec((D_MODEL, D_FF), whole),
            _const_spec((D_MODEL, D_FF), whole),
            _const_spec((D_FF, D_MODEL), whole),
            _const_spec((D_MODEL, D_MODEL), whole),
            _const_spec((PLE_DIM, D_MODEL), whole),
        ] + [c[0] for c in conv],
        out_specs=[
            pl.BlockSpec((tm, D_MODEL), lambda r: (r, 0)),
            pl.BlockSpec((ns, D_MODEL), whole),
        ] + [c[1] for c in conv],
        out_shape=[
            jax.ShapeDtypeStruct((rows, D_MODEL), F32),
            jax.ShapeDtypeStruct((ns, D_MODEL), F32),
        ] + [c[2] for c in conv],
        scratch_shapes=[pltpu.VMEM((tm, D_FF), BF16)],
        compiler_params=pltpu.CompilerParams(
            dimension_semantics=("arbitrary",), vmem_limit_bytes=FFN_VMEM_LIMIT_BYTES),
        name=f"ffn_l{layer}",
    )(x, p, xs, ps, prm["vecs"], *ffn_w, *([prm["w_in_f32"]] if convert else []))
    return outs[0], outs[1], (outs[2] if convert else None)


def _t5_bucket(dist):
    n = np.maximum(dist, 0)
    max_exact = N_BUCKETS // 2
    nf = np.maximum(n, 1).astype(np.float32)
    large = max_exact + (np.log(nf / max_exact) / math.log(MAX_DISTANCE / max_exact)
                         * (N_BUCKETS - max_exact)).astype(np.int32)
    large = np.minimum(large, N_BUCKETS - 1)
    return np.where(n < max_exact, n, large)


def _bucket_table(dist):
    return np.where((dist >= 0) & (dist < WINDOW), _t5_bucket(dist), -1).astype(np.int32)


def _regroup_heads(w, axis):
    shape = w.shape
    w = w.reshape(shape[:axis] + (N_KV_HEADS, GROUP, HEAD_DIM) + shape[axis + 1:])
    return jnp.swapaxes(w, axis, axis + 1).reshape(shape)


def _prepare(t5_table, ln1, w_in, q_gain, k_gain, sinks, w_o_attn, conv_w, conv_b, w_a, b_a,
             w_x, b_x, lam, w_o_lru, w_out, ln2, w_gate, w_up, w_down, ln3, w_ple, w_ple_gate):
    depth = w_in.shape[0]
    w_att = lax.optimization_barrier(w_in[:, :, 0:QKV_WIDTH])
    w_qkv = jnp.concatenate(
        [_regroup_heads(w_att[:, :, 0:Q_WIDTH], 2), w_att[:, :, Q_WIDTH:QKV_WIDTH]], axis=2).astype(BF16)

    def row(v):
        return jnp.pad(v, ((0, 0), (0, D_MODEL - v.shape[1])))[:, None, :]

    parts = [(R_LN1, row(ln1)), (R_CW0, conv_w), (R_CB, row(conv_b)), (R_BA, row(b_a)),
             (R_BX, row(b_x)), (R_LAM, row(lam)), (R_QG, row(jnp.tile(q_gain, (1, N_HEADS)))),
             (R_KG, row(jnp.tile(k_gain, (1, N_KV_HEADS)))), (R_LN2, row(ln2)), (R_LN3, row(ln3))]
    used = 0
    for first_row, part in parts:
        assert first_row == used
        used += part.shape[1]
    vecs = jnp.concatenate([part for _, part in parts]
                           + [jnp.zeros((depth, VEC_ROWS - used, D_MODEL), F32)], axis=1)
    head_id = np.arange(Q_WIDTH) // HEAD_DIM
    seg = jnp.asarray(head_id[:, None] == head_id[None, :], BF16)
    bucket_p = _bucket_table((WINDOW + np.arange(WINDOW))[:, None] - np.arange(2 * WINDOW)[None, :])
    bucket_s = _bucket_table((WINDOW - np.arange(WINDOW))[None, :])
    return {
        "t5": t5_table.reshape(-1),
        "sinks": sinks,
        "vecs": vecs,
        "seg": seg,
        "bucket_p": jnp.asarray(bucket_p),
        "bucket_s": jnp.asarray(bucket_s),
        "w_qkv": w_qkv,
        "w_in_f32": w_in,
        "w_in_first": lax.optimization_barrier(w_in[0]).astype(BF16),
        "w_o_attn": _regroup_heads(w_o_attn, 1).astype(BF16),
        "w_ax": jnp.concatenate([w_a, w_x], axis=-1).astype(BF16),
        "w_o_lru": w_o_lru.astype(BF16),
        "w_out": w_out.astype(BF16),
        "w_gate": w_gate,
        "w_up": w_up,
        "w_down": w_down,
        "w_ple_gate": w_ple_gate,
        "w_ple": w_ple,
    }


def kernel(x_prompt, x_sample, cache_k_win, cache_v_win, state_lru_h, state_conv, p_prompt,
           p_sample, t5_table, ln1, w_in, q_gain, k_gain, sinks, w_o_attn, conv_w, conv_b, w_a,
           b_a, w_x, b_x, lam, w_o_lru, w_out, ln2, w_gate, w_up, w_down, ln3, w_ple, w_ple_gate):
    depth = w_in.shape[0]
    nb, seq, _ = x_prompt.shape
    nseq = x_sample.shape[0]
    assert x_sample.shape[1] == 1 and cache_k_win.shape[2] == WINDOW
    prm = _prepare(t5_table, ln1, w_in, q_gain, k_gain, sinks, w_o_attn, conv_w, conv_b, w_a, b_a,
                   w_x, b_x, lam, w_o_lru, w_out, ln2, w_gate, w_up, w_down, ln3, w_ple, w_ple_gate)
    ckt = jnp.transpose(cache_k_win, (0, 1, 3, 4, 2)).reshape(depth, nseq, KV_WIDTH, WINDOW)
    cvt = jnp.transpose(cache_v_win, (0, 1, 3, 4, 2)).reshape(depth, nseq, KV_WIDTH, WINDOW)
    sc = jnp.transpose(state_conv, (0, 2, 1, 3))
    pp = p_prompt.reshape(depth, nb * seq, PLE_DIM)
    ps = p_sample.reshape(depth, nseq, PLE_DIM)

    yp = x_prompt
    ys = x_sample.reshape(nseq, D_MODEL)
    outs = [[] for _ in range(6)]
    nk_buf = jnp.zeros((depth, nseq, KV_WIDTH, WINDOW), F32)
    nv_buf = jnp.zeros((depth, nseq, KV_WIDTH, WINDOW), F32)
    w_in_l = prm["w_in_first"]
    for layer in range(depth):
        x1, nk, nv, nh, nc, *ffn_w = _mixer_prompt(layer, yp, w_in_l, prm)
        outs[0].append(nk.reshape(nb, WINDOW, N_KV_HEADS, HEAD_DIM))
        outs[1].append(nv.reshape(nb, WINDOW, N_KV_HEADS, HEAD_DIM))
        outs[2].append(nh)
        outs[3].append(nc[:, SUBLANES - (CONV_WIDTH - 1):, :])

        x1s, nhs, xr, nk_buf, nv_buf = _mixer_sample(layer, ys, w_in_l, ckt, cvt, state_lru_h, sc,
                                                     prm, nk_buf, nv_buf)
        yp, ys, w_in_l = _ffn(layer, x1.reshape(nb * seq, D_MODEL), pp, x1s, ps, ffn_w, prm)
        yp = yp.reshape(nb, seq, D_MODEL)
        outs[4].append(nhs)
        outs[5].append(jnp.concatenate([state_conv[layer, :, 1:], xr[:, None, :]], axis=1))

    def untranspose(buf):
        return jnp.transpose(buf.reshape(depth, nseq, N_KV_HEADS, HEAD_DIM, WINDOW), (0, 1, 4, 2, 3))

    stacked = [jnp.stack(o) for o in outs]
    return (yp, ys.reshape(nseq, 1, D_MODEL), stacked[0], stacked[1], stacked[2], stacked[3],
            untranspose(nk_buf), untranspose(nv_buf), stacked[4], stacked[5])
```

```python
import functools
import math

import numpy as np
import jax
import jax.numpy as jnp
from jax import lax
from jax.experimental import pallas as pl
from jax.experimental.pallas import tpu as pltpu

F32 = jnp.float32
BF16 = jnp.bfloat16

D_MODEL = 1024
HEAD_DIM = 64
N_HEADS = 8
N_KV_HEADS = 2
GROUP = N_HEADS // N_KV_HEADS
Q_WIDTH = N_HEADS * HEAD_DIM
KV_WIDTH = N_KV_HEADS * HEAD_DIM
WINDOW = 128
N_BUCKETS = 32
MAX_DISTANCE = 128
LRU_WIDTH = D_MODEL
LRU_HEADS = 8
LRU_BLOCK = LRU_WIDTH // LRU_HEADS
LRU_C = 8.0
CONV_WIDTH = 4
D_FF = 2816
PLE_DIM = 256
EPS = 1e-6
NEG_INF = -1e30
TINY = 1e-30
LOG2E = math.log2(math.e)
GELU_C = math.sqrt(2.0 / math.pi)

SUBLANES = 8
BF16_SUBLANES = 16
LANES = 128
VMEM_LIMIT_BYTES = 56 * 1024 * 1024
FFN_VMEM_LIMIT_BYTES = 58 * 1024 * 1024

Q0 = 0
K0 = Q0 + Q_WIDTH
V0 = K0 + KV_WIDTH
QKV_WIDTH = V0 + KV_WIDTH
XR0 = QKV_WIDTH
XG0 = XR0 + LRU_WIDTH
GA0 = XG0 + LRU_WIDTH
GL0 = GA0 + D_MODEL
IN_WIDTH = GL0 + D_MODEL

R_LN1, R_CW0, R_CB, R_BA, R_BX, R_LAM, R_QG, R_KG, R_LN2, R_LN3 = 0, 1, 5, 6, 7, 8, 9, 10, 11, 12
VEC_ROWS = 16

PROMPT_TT = 64
FFN_TM = 1024
SAMPLE_BB = 16
PROJ_CHUNK = 512
N_FFN_WEIGHTS = 5


def _dot(a, b):
    return jnp.dot(a, b, preferred_element_type=F32)


def _dot_nt(a, b):
    return lax.dot_general(a, b, (((1,), (1,)), ((), ())), preferred_element_type=F32)


def _rms(x, g):
    ms = jnp.mean(x * x, axis=-1, keepdims=True)
    return x * lax.rsqrt(ms + EPS) * g


def _seg_rms(x, seg, g):
    x2 = x * x
    ssq = _dot(x2.astype(BF16), seg)
    return x * lax.rsqrt(ssq * (1.0 / HEAD_DIM) + EPS) * g


def _sigmoid(x):
    return 0.5 * jnp.tanh(0.5 * x) + 0.5


def _gelu_times(x, y):
    t = jnp.tanh(x * (GELU_C + (GELU_C * 0.044715) * (x * x)))
    hxy = (0.5 * x) * y
    return hxy + hxy * t


def _softplus(z):
    return jnp.maximum(z, 0.0) + jnp.log1p(jnp.exp(-jnp.abs(z)))


def _gather_bias(bucket, t5_ref, head):
    acc = jnp.full(bucket.shape, NEG_INF, F32)
    for bkt in range(N_BUCKETS):
        acc = jnp.where(bucket == bkt, t5_ref[bkt * N_HEADS + head], acc)
    return acc


def _lru_gates(xch, wax, b_a, b_x, c_row):
    gts = _dot(xch.astype(BF16), wax)
    r = _sigmoid(gts[:, :LRU_BLOCK] + b_a)
    ig = _sigmoid(gts[:, LRU_BLOCK:] + b_x)
    log_a = c_row * r
    a = jnp.exp(log_a)
    y = 1.0 - a * a
    return a, y * lax.rsqrt(jnp.maximum(y, TINY)) * (ig * xch)


def _rows_per_step(rows, nsteps):
    need = -(-rows // nsteps)
    for per in range(BF16_SUBLANES, rows + 1, BF16_SUBLANES):
        if rows % per == 0 and per >= need:
            return per
    raise ValueError((rows, nsteps))


def _convert_specs(stacked, layer, nsteps):
    _, rows, cols = stacked.shape
    per = _rows_per_step(rows, nsteps)
    last = rows // per - 1
    return (pl.BlockSpec((None, per, cols), lambda t: (layer, jnp.minimum(t, last), 0)),
            pl.BlockSpec((per, cols), lambda t: (jnp.minimum(t, last), 0)),
            jax.ShapeDtypeStruct((rows, cols), BF16))


def _convert_blocks(src_refs, dst_refs):
    for src, dst in zip(src_refs, dst_refs):
        dst[...] = src[...].astype(BF16)


def _mixer_prompt_kernel(t5_ref, sinks_ref, x_ref, vec_ref, bucket_ref, seg_ref, wqkv_ref, win_ref,
                         woa_ref, wax_ref, wol_ref, wout_ref, *rest, nb, tt):
    ffn_f32 = rest[0:N_FFN_WEIGHTS]
    x1_ref, nk_ref, nv_ref, nh_ref, nc_ref = rest[N_FFN_WEIGHTS:N_FFN_WEIGHTS + 5]
    ffn_bf16 = rest[N_FFN_WEIGHTS + 5:2 * N_FFN_WEIGHTS + 5]
    (bias_scr, bias_t, kband, vband, cs, a_scr, b_scr, hst, y_scr, o_scr,
     proj_scr, q_scr) = rest[2 * N_FFN_WEIGHTS + 5:]
    tm = nb * tt
    t = pl.program_id(0)
    _convert_blocks(ffn_f32, ffn_bf16)

    @pl.when(t == 0)
    def _init():
        bucket = bucket_ref[...]
        for hh in range(N_HEADS):
            bias_scr[hh] = _gather_bias(bucket, t5_ref, hh) * LOG2E
        kband[...] = jnp.zeros((nb, 2 * WINDOW, KV_WIDTH), F32)
        vband[...] = jnp.zeros((nb, 2 * WINDOW, KV_WIDTH), F32)
        cs[:, 0:SUBLANES, :] = jnp.zeros((nb, SUBLANES, LRU_WIDTH), F32)
        hst[...] = jnp.zeros((nb, LRU_WIDTH), F32)

    x = x_ref[...].reshape(tm, D_MODEL)
    h = (x * vec_ref[R_LN1:R_LN1 + 1, :]).astype(BF16)
    rstd = lax.rsqrt(jnp.mean(x * x, axis=-1, keepdims=True) + EPS)

    cs[:, SUBLANES:SUBLANES + tt, :] = (_dot(h, win_ref[:, XR0:XG0]) * rstd).reshape(nb, tt, LRU_WIDTH)
    c_row = -LRU_C * _softplus(-vec_ref[R_LAM:R_LAM + 1, :])
    n_chunks = (IN_WIDTH - XG0) // PROJ_CHUNK
    qkv = None
    for hd in range(LRU_HEADS):
        lc = slice(hd * LRU_BLOCK, (hd + 1) * LRU_BLOCK)
        xc = vec_ref[R_CB:R_CB + 1, lc]
        for j in range(CONV_WIDTH):
            off = SUBLANES - (CONV_WIDTH - 1) + j
            xc = xc + vec_ref[R_CW0 + j:R_CW0 + j + 1, lc] * cs[:, off:off + tt, lc]
        a, bb = _lru_gates(xc.reshape(tm, LRU_BLOCK), wax_ref[hd], vec_ref[R_BA:R_BA + 1, lc],
                           vec_ref[R_BX:R_BX + 1, lc], c_row[:, lc])
        for b in range(nb):
            rows = pl.ds(b, tt, stride=nb)
            a_scr[hd, rows, :] = a[b * tt:(b + 1) * tt]
            b_scr[hd, rows, :] = bb[b * tt:(b + 1) * tt]
        if hd == 0:
            qkv = _dot(h, wqkv_ref[...]) * rstd
        elif hd == 1:
            seg = seg_ref[...]
            q_scr[...] = _seg_rms(qkv[:, Q0:K0], seg,
                                  vec_ref[R_QG:R_QG + 1, 0:Q_WIDTH] * (LOG2E * HEAD_DIM ** -0.5))
            kn = _seg_rms(qkv[:, K0:V0], seg[0:KV_WIDTH, 0:KV_WIDTH],
                          vec_ref[R_KG:R_KG + 1, 0:KV_WIDTH])
            kband[:, WINDOW:WINDOW + tt, :] = kn.reshape(nb, tt, KV_WIDTH)
            vband[:, WINDOW:WINDOW + tt, :] = qkv[:, V0:QKV_WIDTH].reshape(nb, tt, KV_WIDTH)
            nk_ref[...] = kband[:, tt:tt + WINDOW, :]
            nv_ref[...] = vband[:, tt:tt + WINDOW, :]
        if hd >= LRU_HEADS - n_chunks:
            c0 = (hd - (LRU_HEADS - n_chunks)) * PROJ_CHUNK
            proj_scr[:, c0:c0 + PROJ_CHUNK] = _dot(h, win_ref[:, XG0 + c0:XG0 + c0 + PROJ_CHUNK]) * rstd
    tail = cs[:, tt:tt + SUBLANES, :]
    nc_ref[...] = tail
    cs[:, 0:SUBLANES, :] = tail

    col = lax.broadcasted_iota(jnp.int32, (1, 2 * WINDOW), 1)
    colmask = jnp.where(col >= WINDOW - t * tt, 0.0, NEG_INF)
    for hh in range(N_HEADS):
        bias_t[hh] = bias_scr[hh, 0:tt, :] + colmask

    lane = lax.broadcasted_iota(jnp.int32, (1, LANES), 1)
    low = lane < HEAD_DIM
    for b in range(nb):
        kb = kband[b].astype(BF16)
        vb = vband[b].astype(BF16)
        o_kv = []
        for kh in range(N_KV_HEADS):
            keep = low if kh == 0 else jnp.logical_not(low)
            q4 = jnp.concatenate(
                [jnp.where(keep, q_scr[b * tt:(b + 1) * tt, g * LANES:(g + 1) * LANES], 0.0)
                 for g in range(GROUP)], axis=0).astype(BF16)
            s = _dot_nt(q4, kb)
            ps, invs = [], []
            for g in range(GROUP):
                hh = kh * GROUP + g
                sg = s[g * tt:(g + 1) * tt] + bias_t[hh]
                sink = sinks_ref[hh] * LOG2E
                m = jnp.maximum(jnp.max(sg, axis=-1, keepdims=True), sink)
                e = jnp.exp2(sg - m)
                invs.append(1.0 / (jnp.sum(e, axis=-1, keepdims=True) + jnp.exp2(sink - m)))
                ps.append(e.astype(BF16))
            o4 = _dot(jnp.concatenate(ps, axis=0), vb)
            o_kv.append([o4[g * tt:(g + 1) * tt] * invs[g] for g in range(GROUP)])
        for g in range(GROUP):
            og = jnp.where(low, o_kv[0][g], o_kv[1][g])
            o_scr[b * tt:(b + 1) * tt, g * LANES:(g + 1) * LANES] = og.astype(BF16)

    kband[:, 0:WINDOW, :] = kband[:, tt:tt + WINDOW, :]
    vband[:, 0:WINDOW, :] = vband[:, tt:tt + WINDOW, :]

    m_att = _sigmoid(proj_scr[:, GA0 - XG0:GL0 - XG0]) * _dot(o_scr[...], woa_ref[...])

    hs = [hst[:, hd * LRU_BLOCK:(hd + 1) * LRU_BLOCK] for hd in range(LRU_HEADS)]
    for ts in range(tt):
        rows = slice(ts * nb, (ts + 1) * nb)
        for hd in range(LRU_HEADS):
            hs[hd] = a_scr[hd, rows, :] * hs[hd] + b_scr[hd, rows, :]
            b_scr[hd, rows, :] = hs[hd]
    hfin = jnp.concatenate(hs, axis=1)
    hst[...] = hfin
    nh_ref[...] = hfin

    for hd in range(LRU_HEADS):
        lc = slice(hd * LRU_BLOCK, (hd + 1) * LRU_BLOCK)
        hseq = jnp.concatenate([b_scr[hd, pl.ds(b, tt, stride=nb), :] for b in range(nb)], axis=0)
        y_scr[:, lc] = _gelu_times(proj_scr[:, lc], hseq).astype(BF16)

    m_all = m_att + _sigmoid(proj_scr[:, GL0 - XG0:IN_WIDTH - XG0]) * _dot(y_scr[...], wol_ref[...])
    x1 = x + _dot(m_all.astype(BF16), wout_ref[...])
    x1_ref[...] = x1.reshape(nb, tt, D_MODEL)


def _const_spec(shape, index):
    return pl.BlockSpec(shape, index, pipeline_mode=pl.Buffered(1))


def _smem_spec():
    return pl.BlockSpec(memory_space=pltpu.SMEM)


def _mixer_prompt(layer, x, w_in, prm):
    nb, seq, _ = x.shape
    tt = PROMPT_TT
    assert nb == SUBLANES and seq % tt == 0 and WINDOW % tt == 0
    tm = nb * tt
    wl = lambda *z: lambda t: (layer,) + z
    kern = functools.partial(_mixer_prompt_kernel, nb=nb, tt=tt)
    conv = [_convert_specs(prm[name], layer, seq // tt) for name in FFN_WEIGHT_NAMES]
    return pl.pallas_call(
        kern,
        grid=(seq // tt,),
        in_specs=[
            _smem_spec(), _smem_spec(),
            pl.BlockSpec((nb, tt, D_MODEL), lambda t: (0, t, 0)),
            _const_spec((None, VEC_ROWS, D_MODEL), wl(0, 0)),
            _const_spec((WINDOW, 2 * WINDOW), lambda t: (0, 0)),
            _const_spec((Q_WIDTH, Q_WIDTH), lambda t: (0, 0)),
            _const_spec((None, D_MODEL, QKV_WIDTH), wl(0, 0)),
            _const_spec((D_MODEL, IN_WIDTH), lambda t: (0, 0)),
            _const_spec((None, Q_WIDTH, D_MODEL), wl(0, 0)),
            _const_spec((None, LRU_HEADS, LRU_BLOCK, 2 * LRU_BLOCK), wl(0, 0, 0)),
            _const_spec((None, LRU_WIDTH, D_MODEL), wl(0, 0)),
            _const_spec((None, D_MODEL, D_MODEL), wl(0, 0)),
        ] + [c[0] for c in conv],
        out_specs=[
            pl.BlockSpec((nb, tt, D_MODEL), lambda t: (0, t, 0)),
            pl.BlockSpec((nb, WINDOW, KV_WIDTH), lambda t: (0, 0, 0)),
            pl.BlockSpec((nb, WINDOW, KV_WIDTH), lambda t: (0, 0, 0)),
            pl.BlockSpec((nb, LRU_WIDTH), lambda t: (0, 0)),
            pl.BlockSpec((nb, SUBLANES, LRU_WIDTH), lambda t: (0, 0, 0)),
        ] + [c[1] for c in conv],
        out_shape=[
            jax.ShapeDtypeStruct((nb, seq, D_MODEL), F32),
            jax.ShapeDtypeStruct((nb, WINDOW, KV_WIDTH), F32),
            jax.ShapeDtypeStruct((nb, WINDOW, KV_WIDTH), F32),
            jax.ShapeDtypeStruct((nb, LRU_WIDTH), F32),
            jax.ShapeDtypeStruct((nb, SUBLANES, LRU_WIDTH), F32),
        ] + [c[2] for c in conv],
        scratch_shapes=[
            pltpu.VMEM((N_HEADS, WINDOW, 2 * WINDOW), F32),
            pltpu.VMEM((N_HEADS, tt, 2 * WINDOW), F32),
            pltpu.VMEM((nb, 2 * WINDOW, KV_WIDTH), F32),
            pltpu.VMEM((nb, 2 * WINDOW, KV_WIDTH), F32),
            pltpu.VMEM((nb, tt + SUBLANES, LRU_WIDTH), F32),
            pltpu.VMEM((LRU_HEADS, tm, LRU_BLOCK), F32),
            pltpu.VMEM((LRU_HEADS, tm, LRU_BLOCK), F32),
            pltpu.VMEM((nb, LRU_WIDTH), F32),
            pltpu.VMEM((tm, LRU_WIDTH), BF16),
            pltpu.VMEM((tm, Q_WIDTH), BF16),
            pltpu.VMEM((tm, IN_WIDTH - XG0), F32),
            pltpu.VMEM((tm, Q_WIDTH), F32),
        ],
        compiler_params=pltpu.CompilerParams(
            dimension_semantics=("arbitrary",), vmem_limit_bytes=VMEM_LIMIT_BYTES),
        name=f"mixer_prompt_l{layer}",
    )(prm["t5"], prm["sinks"][layer], x, prm["vecs"], prm["bucket_p"], prm["seg"], prm["w_qkv"],
      w_in, prm["w_o_attn"], prm["w_ax"], prm["w_o_lru"], prm["w_out"],
      *[prm[name] for name in FFN_WEIGHT_NAMES])


def _mixer_sample_kernel(t5_ref, sinks_ref, x_ref, vec_ref, bucket_ref, seg_ref, wqkv_ref, win_ref,
                         woa_ref, wax_ref, wol_ref, wout_ref, ck_ref, cv_ref, h0_ref, sc_ref,
                         nk_all_ref, nv_all_ref,
                         x1_ref, nh_ref, xr_ref, nk_ref, nv_ref,
                         h_scr, q_scr, o_scr, mb_scr, kn_scr, vn_scr,
                         *, bb):
    del nk_all_ref, nv_all_ref
    i = pl.program_id(0)
    nkeys = bb * WINDOW

    @pl.when(i == 0)
    def _project():
        x = x_ref[...]
        h = _rms(x, vec_ref[R_LN1:R_LN1 + 1, :]).astype(BF16)
        h_scr[...] = h
        qkv = _dot(h, wqkv_ref[...])
        seg = seg_ref[...]
        q_scr[...] = _seg_rms(qkv[:, Q0:K0], seg,
                              vec_ref[R_QG:R_QG + 1, 0:Q_WIDTH] * (HEAD_DIM ** -0.5))
        kn_scr[...] = _seg_rms(qkv[:, K0:V0], seg[0:KV_WIDTH, 0:KV_WIDTH],
                               vec_ref[R_KG:R_KG + 1, 0:KV_WIDTH])
        vn_scr[...] = qkv[:, V0:QKV_WIDTH]
        bucket = bucket_ref[...]
        rowb = lax.broadcasted_iota(jnp.int32, (bb, nkeys), 0)
        colb = lax.broadcasted_iota(jnp.int32, (bb, nkeys), 1) // WINDOW
        for hh in range(N_HEADS):
            brow = _gather_bias(bucket, t5_ref, hh)
            brow = jnp.concatenate([brow] * bb, axis=1)
            mb_scr[hh * bb:(hh + 1) * bb, :] = jnp.where(rowb == colb, brow, NEG_INF)

    rows = pl.ds(pl.multiple_of(i * bb, bb), bb)
    lane = lax.broadcasted_iota(jnp.int32, (1, LANES), 1)
    low = lane < HEAD_DIM
    qblk = q_scr[rows, :]
    qz = []
    for kh in range(N_KV_HEADS):
        keep = low if kh == 0 else jnp.logical_not(low)
        for g in range(GROUP):
            qz.append(jnp.where(keep, qblk[:, g * LANES:(g + 1) * LANES], 0.0))
    qz = jnp.concatenate(qz, axis=0).astype(BF16)
    kt = jnp.concatenate([ck_ref[b] for b in range(bb)], axis=1).astype(BF16)
    vt = jnp.concatenate([cv_ref[b] for b in range(bb)], axis=1).astype(BF16)
    s = _dot(qz, kt) + mb_scr[...]
    kn_blk = kn_scr[rows, :]
    vn_blk = vn_scr[rows, :]
    knew = jnp.concatenate([kn_blk.astype(BF16).astype(F32)] * N_HEADS, axis=0)
    vnew = jnp.concatenate([vn_blk.astype(BF16).astype(F32)] * N_HEADS, axis=0)
    self_bias = jnp.concatenate(
        [jnp.full((bb, 1), t5_ref[hh], F32) for hh in range(N_HEADS)], axis=0)
    sink = jnp.concatenate(
        [jnp.full((bb, 1), sinks_ref[hh], F32) for hh in range(N_HEADS)], axis=0)
    s_self = jnp.sum(qz.astype(F32) * knew, axis=-1, keepdims=True) + self_bias
    m = jnp.maximum(jnp.maximum(jnp.max(s, axis=-1, keepdims=True), s_self), sink)
    e = jnp.exp(s - m)
    e_self = jnp.exp(s_self - m)
    den = jnp.sum(e, axis=-1, keepdims=True) + e_self + jnp.exp(sink - m)
    inv = 1.0 / den
    o = (_dot_nt((e * inv).astype(BF16), vt)
         + (e_self * inv).astype(BF16).astype(F32) * vnew)
    half = GROUP * bb
    for g in range(GROUP):
        og = jnp.where(low, o[g * bb:(g + 1) * bb], o[half + g * bb:half + (g + 1) * bb])
        o_scr[rows, g * LANES:(g + 1) * LANES] = og

    knt = kn_blk.T
    vnt = vn_blk.T
    last = lax.broadcasted_iota(jnp.int32, (1, WINDOW), 1) == WINDOW - 1
    for b in range(bb):
        nk_ref[b] = jnp.where(last, knt[:, b:b + 1], pltpu.roll(ck_ref[b], WINDOW - 1, 1))
        nv_ref[b] = jnp.where(last, vnt[:, b:b + 1], pltpu.roll(cv_ref[b], WINDOW - 1, 1))

    @pl.when(i == pl.num_programs(0) - 1)
    def _finish():
        x = x_ref[...]
        h = h_scr[...]
        m_att = _sigmoid(_dot(h, win_ref[:, GA0:GL0])) * _dot(o_scr[...].astype(BF16), woa_ref[...])
        xr = _dot(h, win_ref[:, XR0:XG0])
        xr_ref[...] = xr
        xc = vec_ref[R_CB:R_CB + 1, :] + vec_ref[R_CW0 + CONV_WIDTH - 1:R_CW0 + CONV_WIDTH, :] * xr
        for j in range(CONV_WIDTH - 1):
            xc = xc + vec_ref[R_CW0 + j:R_CW0 + j + 1, :] * sc_ref[j]
        c_row = -LRU_C * _softplus(-vec_ref[R_LAM:R_LAM + 1, :])
        hn = []
        for hd in range(LRU_HEADS):
            lc = slice(hd * LRU_BLOCK, (hd + 1) * LRU_BLOCK)
            a, bb_ = _lru_gates(xc[:, lc], wax_ref[hd], vec_ref[R_BA:R_BA + 1, lc],
                                vec_ref[R_BX:R_BX + 1, lc], c_row[:, lc])
            hn.append(a * h0_ref[:, lc] + bb_)
        hn = jnp.concatenate(hn, axis=1)
        nh_ref[...] = hn
        y = _gelu_times(_dot(h, win_ref[:, XG0:GA0]), hn).astype(BF16)
        m_all = m_att + _sigmoid(_dot(h, win_ref[:, GL0:IN_WIDTH])) * _dot(y, wol_ref[...])
        x1_ref[...] = x + _dot(m_all.astype(BF16), wout_ref[...])


def _mixer_sample(layer, x, w_in, ckt, cvt, h0, sc, prm, nk_buf, nv_buf):
    depth, nseq = ckt.shape[0], ckt.shape[1]
    bb = SAMPLE_BB
    assert nseq % bb == 0
    wl = lambda *z: lambda i: (layer,) + z
    full2 = lambda i: (0, 0)
    cache_spec = pl.BlockSpec((None, bb, KV_WIDTH, WINDOW), lambda i: (layer, i, 0, 0))
    n_in = 16
    kern = functools.partial(_mixer_sample_kernel, bb=bb)
    return pl.pallas_call(
        kern,
        grid=(nseq // bb,),
        in_specs=[
            _smem_spec(), _smem_spec(),
            _const_spec((nseq, D_MODEL), full2),
            _const_spec((None, VEC_ROWS, D_MODEL), wl(0, 0)),
            _const_spec((1, WINDOW), full2),
            _const_spec((Q_WIDTH, Q_WIDTH), full2),
            _const_spec((None, D_MODEL, QKV_WIDTH), wl(0, 0)),
            _const_spec((D_MODEL, IN_WIDTH), full2),
            _const_spec((None, Q_WIDTH, D_MODEL), wl(0, 0)),
            _const_spec((None, LRU_HEADS, LRU_BLOCK, 2 * LRU_BLOCK), wl(0, 0, 0)),
            _const_spec((None, LRU_WIDTH, D_MODEL), wl(0, 0)),
            _const_spec((None, D_MODEL, D_MODEL), wl(0, 0)),
            cache_spec, cache_spec,
            _const_spec((None, nseq, LRU_WIDTH), wl(0, 0)),
            _const_spec((None, CONV_WIDTH - 1, nseq, LRU_WIDTH), wl(0, 0, 0)),
            pl.BlockSpec(memory_space=pl.ANY), pl.BlockSpec(memory_space=pl.ANY),
        ],
        out_specs=[
            pl.BlockSpec((nseq, D_MODEL), full2),
            pl.BlockSpec((nseq, LRU_WIDTH), full2),
            pl.BlockSpec((nseq, LRU_WIDTH), full2),
            cache_spec, cache_spec,
        ],
        out_shape=[
            jax.ShapeDtypeStruct((nseq, D_MODEL), F32),
            jax.ShapeDtypeStruct((nseq, LRU_WIDTH), F32),
            jax.ShapeDtypeStruct((nseq, LRU_WIDTH), F32),
            jax.ShapeDtypeStruct((depth, nseq, KV_WIDTH, WINDOW), F32),
            jax.ShapeDtypeStruct((depth, nseq, KV_WIDTH, WINDOW), F32),
        ],
        input_output_aliases={n_in: 3, n_in + 1: 4},
        scratch_shapes=[
            pltpu.VMEM((nseq, D_MODEL), BF16),
            pltpu.VMEM((nseq, Q_WIDTH), F32),
            pltpu.VMEM((nseq, Q_WIDTH), F32),
            pltpu.VMEM((N_HEADS * bb, bb * WINDOW), F32),
            pltpu.VMEM((nseq, KV_WIDTH), F32),
            pltpu.VMEM((nseq, KV_WIDTH), F32),
        ],
        compiler_params=pltpu.CompilerParams(
            dimension_semantics=("arbitrary",), vmem_limit_bytes=VMEM_LIMIT_BYTES),
        name=f"mixer_sample_l{layer}",
    )(prm["t5"], prm["sinks"][layer], x, prm["vecs"], prm["bucket_s"], prm["seg"], prm["w_qkv"],
      w_in, prm["w_o_attn"], prm["w_ax"], prm["w_o_lru"], prm["w_out"], ckt, cvt, h0, sc,
      nk_buf, nv_buf)


FF_CHUNKS = ((0, 1024), (1024, 2048), (2048, D_FF))
FFN_WEIGHT_NAMES = ("w_gate", "w_up", "w_down", "w_ple_gate", "w_ple")


def _ffn_rows(x, p, vec_ref, wg_ref, wu_ref, wd_ref, wpg_ref, wp_ref, act_scr):
    h2 = (x * vec_ref[R_LN2:R_LN2 + 1, :]).astype(BF16)
    r2 = lax.rsqrt(jnp.mean(x * x, axis=-1, keepdims=True) + EPS)
    for lo, hi in FF_CHUNKS:
        g = _dot(h2, wg_ref[:, lo:hi]) * r2
        act = g * _sigmoid(g) * (_dot(h2, wu_ref[:, lo:hi]) * r2)
        act_scr[:, lo:hi] = act.astype(BF16)
    x = x + _dot(act_scr[...], wd_ref[...])
    h3 = (x * vec_ref[R_LN3:R_LN3 + 1, :]).astype(BF16)
    r3 = lax.rsqrt(jnp.mean(x * x, axis=-1, keepdims=True) + EPS)
    gate = _sigmoid(_dot(h3, wpg_ref[...]) * r3)
    return x + gate * _dot(p.astype(BF16), wp_ref[...])


def _ffn_kernel(x_ref, p_ref, xs_ref, ps_ref, vec_ref, wg_ref, wu_ref, wd_ref, wpg_ref, wp_ref,
                *rest, convert):
    if convert:
        win_f32, o_ref, os_ref, win_bf16, act_scr = rest
        _convert_blocks([win_f32], [win_bf16])
    else:
        o_ref, os_ref, act_scr = rest
    weights = (vec_ref, wg_ref, wu_ref, wd_ref, wpg_ref, wp_ref)
    o_ref[...] = _ffn_rows(x_ref[...], p_ref[...], *weights, act_scr)

    @pl.when(pl.program_id(0) == pl.num_programs(0) - 1)
    def _sample():
        ns = xs_ref.shape[0]
        os_ref[...] = _ffn_rows(xs_ref[...], ps_ref[...], *weights, act_scr.at[0:ns, :])


def _ffn(layer, x, p, xs, ps, ffn_w, prm):
    rows, ns = x.shape[0], xs.shape[0]
    tm = min(FFN_TM, rows)
    assert rows % tm == 0 and ns <= tm
    depth = prm["w_in_f32"].shape[0]
    convert = layer + 1 < depth
    conv = [_convert_specs(prm["w_in_f32"], layer + 1, rows // tm)] if convert else []
    wl = lambda *z: lambda r: (layer,) + z
    whole = lambda r: (0, 0)
    outs = pl.pallas_call(
        functools.partial(_ffn_kernel, convert=convert),
        grid=(rows // tm,),
        in_specs=[
            pl.BlockSpec((tm, D_MODEL), lambda r: (r, 0)),
            pl.BlockSpec((None, tm, PLE_DIM), lambda r: (layer, r, 0)),
            _const_spec((ns, D_MODEL), whole),
            _const_spec((None, ns, PLE_DIM), wl(0, 0)),
            _const_spec((None, VEC_ROWS, D_MODEL), wl(0, 0)),
            _const_spec((D_MODEL, D_FF), whole),
            _const_spec((D_MODEL, D_FF), whole),
            _const_spec((D_FF, D_MODEL), whole),
            _const_spec((D_MODEL, D_MODEL), whole),
            _const_spec((PLE_DIM, D_MODEL), whole),
        ] + [c[0] for c in conv],
        out_specs=[
            pl.BlockSpec((tm, D_MODEL), lambda r: (r, 0)),
            pl.BlockSpec((ns, D_MODEL), whole),
        ] + [c[1] for c in conv],
        out_shape=[
            jax.ShapeDtypeStruct((rows, D_MODEL), F32),
            jax.ShapeDtypeStruct((ns, D_MODEL), F32),
        ] + [c[2] for c in conv],
        scratch_shapes=[pltpu.VMEM((tm, D_FF), BF16)],
        compiler_params=pltpu.CompilerParams(
            dimension_semantics=("arbitrary",), vmem_limit_bytes=FFN_VMEM_LIMIT_BYTES),
        name=f"ffn_l{layer}",
    )(x, p, xs, ps, prm["vecs"], *ffn_w, *([prm["w_in_f32"]] if convert else []))
    return outs[0], outs[1], (outs[2] if convert else None)


def _t5_bucket(dist):
    n = np.maximum(dist, 0)
    max_exact = N_BUCKETS // 2
    nf = np.maximum(n, 1).astype(np.float32)
    large = max_exact + (np.log(nf / max_exact) / math.log(MAX_DISTANCE / max_exact)
                         * (N_BUCKETS - max_exact)).astype(np.int32)
    large = np.minimum(large, N_BUCKETS - 1)
    return np.where(n < max_exact, n, large)


def _bucket_table(dist):
    return np.where((dist >= 0) & (dist < WINDOW), _t5_bucket(dist), -1).astype(np.int32)


def _regroup_heads(w, axis):
    shape = w.shape
    w = w.reshape(shape[:axis] + (N_KV_HEADS, GROUP, HEAD_DIM) + shape[axis + 1:])
    return jnp.swapaxes(w, axis, axis + 1).reshape(shape)


def _prepare(t5_table, ln1, w_in, q_gain, k_gain, sinks, w_o_attn, conv_w, conv_b, w_a, b_a,
             w_x, b_x, lam, w_o_lru, w_out, ln2, w_gate, w_up, w_down, ln3, w_ple, w_ple_gate):
    depth = w_in.shape[0]
    w_att = lax.optimization_barrier(w_in[:, :, 0:QKV_WIDTH])
    w_qkv = jnp.concatenate(
        [_regroup_heads(w_att[:, :, 0:Q_WIDTH], 2), w_att[:, :, Q_WIDTH:QKV_WIDTH]], axis=2).astype(BF16)

    def row(v):
        return jnp.pad(v, ((0, 0), (0, D_MODEL - v.shape[1])))[:, None, :]

    parts = [(R_LN1, row(ln1)), (R_CW0, conv_w), (R_CB, row(conv_b)), (R_BA, row(b_a)),
             (R_BX, row(b_x)), (R_LAM, row(lam)), (R_QG, row(jnp.tile(q_gain, (1, N_HEADS)))),
             (R_KG, row(jnp.tile(k_gain, (1, N_KV_HEADS)))), (R_LN2, row(ln2)), (R_LN3, row(ln3))]
    used = 0
    for first_row, part in parts:
        assert first_row == used
        used += part.shape[1]
    vecs = jnp.concatenate([part for _, part in parts]
                           + [jnp.zeros((depth, VEC_ROWS - used, D_MODEL), F32)], axis=1)
    head_id = np.arange(Q_WIDTH) // HEAD_DIM
    seg = jnp.asarray(head_id[:, None] == head_id[None, :], BF16)
    bucket_p = _bucket_table((WINDOW + np.arange(WINDOW))[:, None] - np.arange(2 * WINDOW)[None, :])
    bucket_s = _bucket_table((WINDOW - np.arange(WINDOW))[None, :])
    return {
        "t5": t5_table.reshape(-1),
        "sinks": sinks,
        "vecs": vecs,
        "seg": seg,
        "bucket_p": jnp.asarray(bucket_p),
        "bucket_s": jnp.asarray(bucket_s),
        "w_qkv": w_qkv,
        "w_in_f32": w_in,
        "w_in_first": lax.optimization_barrier(w_in[0]).astype(BF16),
        "w_o_attn": _regroup_heads(w_o_attn, 1).astype(BF16),
        "w_ax": jnp.concatenate([w_a, w_x], axis=-1).astype(BF16),
        "w_o_lru": w_o_lru.astype(BF16),
        "w_out": w_out.astype(BF16),
        "w_gate": w_gate,
        "w_up": w_up,
        "w_down": w_down,
        "w_ple_gate": w_ple_gate,
        "w_ple": w_ple,
    }


def kernel(x_prompt, x_sample, cache_k_win, cache_v_win, state_lru_h, state_conv, p_prompt,
           p_sample, t5_table, ln1, w_in, q_gain, k_gain, sinks, w_o_attn, conv_w, conv_b, w_a,
           b_a, w_x, b_x, lam, w_o_lru, w_out, ln2, w_gate, w_up, w_down, ln3, w_ple, w_ple_gate):
    depth = w_in.shape[0]
    nb, seq, _ = x_prompt.shape
    nseq = x_sample.shape[0]
    assert x_sample.shape[1] == 1 and cache_k_win.shape[2] == WINDOW
    prm = _prepare(t5_table, ln1, w_in, q_gain, k_gain, sinks, w_o_attn, conv_w, conv_b, w_a, b_a,
                   w_x, b_x, lam, w_o_lru, w_out, ln2, w_gate, w_up, w_down, ln3, w_ple, w_ple_gate)
    ckt = jnp.transpose(cache_k_win, (0, 1, 3, 4, 2)).reshape(depth, nseq, KV_WIDTH, WINDOW)
    cvt = jnp.transpose(cache_v_win, (0, 1, 3, 4, 2)).reshape(depth, nseq, KV_WIDTH, WINDOW)
    sc = jnp.transpose(state_conv, (0, 2, 1, 3))
    pp = p_prompt.reshape(depth, nb * seq, PLE_DIM)
    ps = p_sample.reshape(depth, nseq, PLE_DIM)

    yp = x_prompt
    ys = x_sample.reshape(nseq, D_MODEL)
    outs = [[] for _ in range(6)]
    nk_buf = jnp.zeros((depth, nseq, KV_WIDTH, WINDOW), F32)
    nv_buf = jnp.zeros((depth, nseq, KV_WIDTH, WINDOW), F32)
    w_in_l = prm["w_in_first"]
    for layer in range(depth):
        x1, nk, nv, nh, nc, *ffn_w = _mixer_prompt(layer, yp, w_in_l, prm)
        outs[0].append(nk.reshape(nb, WINDOW, N_KV_HEADS, HEAD_DIM))
        outs[1].append(nv.reshape(nb, WINDOW, N_KV_HEADS, HEAD_DIM))
        outs[2].append(nh)
        outs[3].append(nc[:, SUBLANES - (CONV_WIDTH - 1):, :])

        x1s, nhs, xr, nk_buf, nv_buf = _mixer_sample(layer, ys, w_in_l, ckt, cvt, state_lru_h, sc,
                                                     prm, nk_buf, nv_buf)
        yp, ys, w_in_l = _ffn(layer, x1.reshape(nb * seq, D_MODEL), pp, x1s, ps, ffn_w, prm)
        yp = yp.reshape(nb, seq, D_MODEL)
        outs[4].append(nhs)
        outs[5].append(jnp.concatenate([state_conv[layer, :, 1:], xr[:, None, :]], axis=1))

    def untranspose(buf):
        return jnp.transpose(buf.reshape(depth, nseq, N_KV_HEADS, HEAD_DIM, WINDOW), (0, 1, 4, 2, 3))

    stacked = [jnp.stack(o) for o in outs]
    return (yp, ys.reshape(nseq, 1, D_MODEL), stacked[0], stacked[1], stacked[2], stacked[3],
            untranspose(nk_buf), untranspose(nv_buf), stacked[4], stacked[5])
```

```python
import functools
import math

import numpy as np
import jax
import jax.numpy as jnp
from jax import lax
from jax.experimental import pallas as pl
from jax.experimental.pallas import tpu as pltpu

F32 = jnp.float32
BF16 = jnp.bfloat16

D_MODEL = 1024
HEAD_DIM = 64
N_HEADS = 8
N_KV_HEADS = 2
GROUP = N_HEADS // N_KV_HEADS
Q_WIDTH = N_HEADS * HEAD_DIM
KV_WIDTH = N_KV_HEADS * HEAD_DIM
WINDOW = 128
N_BUCKETS = 32
MAX_DISTANCE = 128
LRU_WIDTH = D_MODEL
LRU_HEADS = 8
LRU_BLOCK = LRU_WIDTH // LRU_HEADS
LRU_C = 8.0
CONV_WIDTH = 4
D_FF = 2816
PLE_DIM = 256
EPS = 1e-6
NEG_INF = -1e30
TINY = 1e-30
LOG2E = math.log2(math.e)
GELU_C = math.sqrt(2.0 / math.pi)

SUBLANES = 8
BF16_SUBLANES = 16
LANES = 128
VMEM_LIMIT_BYTES = 56 * 1024 * 1024
FFN_VMEM_LIMIT_BYTES = 58 * 1024 * 1024

Q0 = 0
K0 = Q0 + Q_WIDTH
V0 = K0 + KV_WIDTH
QKV_WIDTH = V0 + KV_WIDTH
XR0 = QKV_WIDTH
XG0 = XR0 + LRU_WIDTH
GA0 = XG0 + LRU_WIDTH
GL0 = GA0 + D_MODEL
IN_WIDTH = GL0 + D_MODEL

R_LN1, R_CW0, R_CB, R_BA, R_BX, R_LAM, R_QG, R_KG, R_LN2, R_LN3 = 0, 1, 5, 6, 7, 8, 9, 10, 11, 12
VEC_ROWS = 16

PROMPT_TT = 64
FFN_TM = 1024
SAMPLE_BB = 16
PROJ_CHUNK = 512
N_FFN_WEIGHTS = 5


def _dot(a, b):
    return jnp.dot(a, b, preferred_element_type=F32)


def _dot_nt(a, b):
    return lax.dot_general(a, b, (((1,), (1,)), ((), ())), preferred_element_type=F32)


def _rms(x, g):
    ms = jnp.mean(x * x, axis=-1, keepdims=True)
    return x * lax.rsqrt(ms + EPS) * g


def _seg_rms(x, seg, g):
    x2 = x * x
    ssq = _dot(x2.astype(BF16), seg)
    return x * lax.rsqrt(ssq * (1.0 / HEAD_DIM) + EPS) * g


def _sigmoid(x):
    return 0.5 * jnp.tanh(0.5 * x) + 0.5


def _gelu_times(x, y, half=0.5):
    t = jnp.tanh(x * (GELU_C + (GELU_C * 0.044715) * (x * x)))
    hxy = (half * x) * y
    return hxy + hxy * t


def _softplus(z):
    return jnp.maximum(z, 0.0) + jnp.log1p(jnp.exp(-jnp.abs(z)))


def _gather_bias(bucket, t5_ref, head):
    acc = jnp.full(bucket.shape, NEG_INF, F32)
    for bkt in range(N_BUCKETS):
        acc = jnp.where(bucket == bkt, t5_ref[bkt * N_HEADS + head], acc)
    return acc


def _lru_gates(xch, wax, b_a, b_x, c_row):
    gts = _dot(xch.astype(BF16), wax)
    r = _sigmoid(gts[:, :LRU_BLOCK] + b_a)
    ig = _sigmoid(gts[:, LRU_BLOCK:] + b_x)
    log_a = c_row * r
    a = jnp.exp(log_a)
    y = 1.0 - a * a
    return a, y * lax.rsqrt(jnp.maximum(y, TINY)) * (ig * xch)


def _rows_per_step(rows, nsteps):
    need = -(-rows // nsteps)
    for per in range(BF16_SUBLANES, rows + 1, BF16_SUBLANES):
        if rows % per == 0 and per >= need:
            return per
    raise ValueError((rows, nsteps))


def _convert_specs(stacked, layer, nsteps):
    _, rows, cols = stacked.shape
    per = _rows_per_step(rows, nsteps)
    last = rows // per - 1
    return (pl.BlockSpec((None, per, cols), lambda t: (layer, jnp.minimum(t, last), 0)),
            pl.BlockSpec((per, cols), lambda t: (jnp.minimum(t, last), 0)),
            jax.ShapeDtypeStruct((rows, cols), BF16))


def _convert_blocks(src_refs, dst_refs):
    for src, dst in zip(src_refs, dst_refs):
        dst[...] = src[...].astype(BF16)


def _mixer_prompt_kernel(t5_ref, sinks_ref, x_ref, vec_ref, bucket_ref, seg_ref, wqkv_ref, win_ref,
                         woa_ref, wax_ref, wol_ref, wout_ref, *rest, nb, tt):
    ffn_f32 = rest[0:N_FFN_WEIGHTS]
    x1_ref, nk_ref, nv_ref, nh_ref, nc_ref = rest[N_FFN_WEIGHTS:N_FFN_WEIGHTS + 5]
    ffn_bf16 = rest[N_FFN_WEIGHTS + 5:2 * N_FFN_WEIGHTS + 5]
    (bias_scr, bias_t, kband, vband, cs, a_scr, b_scr, hst, y_scr, o_scr,
     proj_scr, q_scr) = rest[2 * N_FFN_WEIGHTS + 5:]
    tm = nb * tt
    t = pl.program_id(0)
    _convert_blocks(ffn_f32, ffn_bf16)

    @pl.when(t == 0)
    def _init():
        bucket = bucket_ref[...]
        for hh in range(N_HEADS):
            bias_scr[hh] = _gather_bias(bucket, t5_ref, hh) * LOG2E
        kband[...] = jnp.zeros((nb, 2 * WINDOW, KV_WIDTH), F32)
        vband[...] = jnp.zeros((nb, 2 * WINDOW, KV_WIDTH), F32)
        cs[:, 0:SUBLANES, :] = jnp.zeros((nb, SUBLANES, LRU_WIDTH), F32)
        hst[...] = jnp.zeros((nb, LRU_WIDTH), F32)

    x = x_ref[...].reshape(tm, D_MODEL)
    h = (x * vec_ref[R_LN1:R_LN1 + 1, :]).astype(BF16)
    rstd = lax.rsqrt(jnp.mean(x * x, axis=-1, keepdims=True) + EPS)
    half_rstd = 0.5 * rstd

    cs[:, SUBLANES:SUBLANES + tt, :] = (_dot(h, win_ref[:, XR0:XG0]) * rstd).reshape(nb, tt, LRU_WIDTH)
    c_row = -LRU_C * _softplus(-vec_ref[R_LAM:R_LAM + 1, :])
    n_chunks = (IN_WIDTH - XG0) // PROJ_CHUNK
    qkv = None
    for hd in range(LRU_HEADS):
        lc = slice(hd * LRU_BLOCK, (hd + 1) * LRU_BLOCK)
        xc = vec_ref[R_CB:R_CB + 1, lc]
        for j in range(CONV_WIDTH):
            off = SUBLANES - (CONV_WIDTH - 1) + j
            xc = xc + vec_ref[R_CW0 + j:R_CW0 + j + 1, lc] * cs[:, off:off + tt, lc]
        a, bb = _lru_gates(xc.reshape(tm, LRU_BLOCK), wax_ref[hd], vec_ref[R_BA:R_BA + 1, lc],
                           vec_ref[R_BX:R_BX + 1, lc], c_row[:, lc])
        for b in range(nb):
            rows = pl.ds(b, tt, stride=nb)
            a_scr[hd, rows, :] = a[b * tt:(b + 1) * tt]
            b_scr[hd, rows, :] = bb[b * tt:(b + 1) * tt]
        if hd == 0:
            qkv = _dot(h, wqkv_ref[...]) * rstd
        elif hd == 1:
            seg = seg_ref[...]
            q_scr[...] = _seg_rms(qkv[:, Q0:K0], seg,
                                  vec_ref[R_QG:R_QG + 1, 0:Q_WIDTH] * (LOG2E * HEAD_DIM ** -0.5))
            kn = _seg_rms(qkv[:, K0:V0], seg[0:KV_WIDTH, 0:KV_WIDTH],
                          vec_ref[R_KG:R_KG + 1, 0:KV_WIDTH])
            kband[:, WINDOW:WINDOW + tt, :] = kn.reshape(nb, tt, KV_WIDTH)
            vband[:, WINDOW:WINDOW + tt, :] = qkv[:, V0:QKV_WIDTH].reshape(nb, tt, KV_WIDTH)
            nk_ref[...] = kband[:, tt:tt + WINDOW, :]
            nv_ref[...] = vband[:, tt:tt + WINDOW, :]
        if hd >= LRU_HEADS - n_chunks:
            c0 = (hd - (LRU_HEADS - n_chunks)) * PROJ_CHUNK
            scale = half_rstd if XG0 + c0 >= GA0 else rstd
            proj_scr[:, c0:c0 + PROJ_CHUNK] = _dot(h, win_ref[:, XG0 + c0:XG0 + c0 + PROJ_CHUNK]) * scale
    tail = cs[:, tt:tt + SUBLANES, :]
    nc_ref[...] = tail
    cs[:, 0:SUBLANES, :] = tail

    col = lax.broadcasted_iota(jnp.int32, (1, 2 * WINDOW), 1)
    colmask = jnp.where(col >= WINDOW - t * tt, 0.0, NEG_INF)
    for hh in range(N_HEADS):
        bias_t[hh] = bias_scr[hh, 0:tt, :] + colmask

    lane = lax.broadcasted_iota(jnp.int32, (1, LANES), 1)
    low = lane < HEAD_DIM
    for b in range(nb):
        kb = kband[b].astype(BF16)
        vb = vband[b].astype(BF16)
        o_kv = []
        for kh in range(N_KV_HEADS):
            keep = low if kh == 0 else jnp.logical_not(low)
            q4 = jnp.concatenate(
                [jnp.where(keep, q_scr[b * tt:(b + 1) * tt, g * LANES:(g + 1) * LANES], 0.0)
                 for g in range(GROUP)], axis=0).astype(BF16)
            s = _dot_nt(q4, kb)
            ps, invs = [], []
            for g in range(GROUP):
                hh = kh * GROUP + g
                sg = s[g * tt:(g + 1) * tt] + bias_t[hh]
                sink = sinks_ref[hh] * LOG2E
                m = jnp.maximum(jnp.max(sg, axis=-1, keepdims=True), sink)
                e = jnp.exp2(sg - m)
                invs.append(0.5 / (jnp.sum(e, axis=-1, keepdims=True) + jnp.exp2(sink - m)))
                ps.append(e.astype(BF16))
            o4 = _dot(jnp.concatenate(ps, axis=0), vb)
            o_kv.append([o4[g * tt:(g + 1) * tt] * invs[g] for g in range(GROUP)])
        for g in range(GROUP):
            og = jnp.where(low, o_kv[0][g], o_kv[1][g])
            o_scr[b * tt:(b + 1) * tt, g * LANES:(g + 1) * LANES] = og.astype(BF16)

    kband[:, 0:WINDOW, :] = kband[:, tt:tt + WINDOW, :]
    vband[:, 0:WINDOW, :] = vband[:, tt:tt + WINDOW, :]

    o_att_half = _dot(o_scr[...], woa_ref[...])
    m_att = o_att_half * jnp.tanh(proj_scr[:, GA0 - XG0:GL0 - XG0]) + o_att_half

    hs = [hst[:, hd * LRU_BLOCK:(hd + 1) * LRU_BLOCK] for hd in range(LRU_HEADS)]
    for ts in range(tt):
        rows = slice(ts * nb, (ts + 1) * nb)
        for hd in range(LRU_HEADS):
            hs[hd] = a_scr[hd, rows, :] * hs[hd] + b_scr[hd, rows, :]
            b_scr[hd, rows, :] = hs[hd]
    hfin = jnp.concatenate(hs, axis=1)
    hst[...] = hfin
    nh_ref[...] = hfin

    for hd in range(LRU_HEADS):
        lc = slice(hd * LRU_BLOCK, (hd + 1) * LRU_BLOCK)
        hseq = jnp.concatenate([b_scr[hd, pl.ds(b, tt, stride=nb), :] for b in range(nb)], axis=0)
        y_scr[:, lc] = _gelu_times(proj_scr[:, lc], hseq, 0.25).astype(BF16)

    o_lru_half = _dot(y_scr[...], wol_ref[...])
    m_all = m_att + (o_lru_half * jnp.tanh(proj_scr[:, GL0 - XG0:IN_WIDTH - XG0]) + o_lru_half)
    x1 = x + _dot(m_all.astype(BF16), wout_ref[...])
    x1_ref[...] = x1.reshape(nb, tt, D_MODEL)


def _const_spec(shape, index):
    return pl.BlockSpec(shape, index, pipeline_mode=pl.Buffered(1))


def _smem_spec():
    return pl.BlockSpec(memory_space=pltpu.SMEM)


def _mixer_prompt(layer, x, w_in, prm):
    nb, seq, _ = x.shape
    tt = PROMPT_TT
    assert nb == SUBLANES and seq % tt == 0 and WINDOW % tt == 0
    tm = nb * tt
    wl = lambda *z: lambda t: (layer,) + z
    kern = functools.partial(_mixer_prompt_kernel, nb=nb, tt=tt)
    conv = [_convert_specs(prm[name], layer, seq // tt) for name in FFN_WEIGHT_NAMES]
    return pl.pallas_call(
        kern,
        grid=(seq // tt,),
        in_specs=[
            _smem_spec(), _smem_spec(),
            pl.BlockSpec((nb, tt, D_MODEL), lambda t: (0, t, 0)),
            _const_spec((None, VEC_ROWS, D_MODEL), wl(0, 0)),
            _const_spec((WINDOW, 2 * WINDOW), lambda t: (0, 0)),
            _const_spec((Q_WIDTH, Q_WIDTH), lambda t: (0, 0)),
            _const_spec((None, D_MODEL, QKV_WIDTH), wl(0, 0)),
            _const_spec((D_MODEL, IN_WIDTH), lambda t: (0, 0)),
            _const_spec((None, Q_WIDTH, D_MODEL), wl(0, 0)),
            _const_spec((None, LRU_HEADS, LRU_BLOCK, 2 * LRU_BLOCK), wl(0, 0, 0)),
            _const_spec((None, LRU_WIDTH, D_MODEL), wl(0, 0)),
            _const_spec((None, D_MODEL, D_MODEL), wl(0, 0)),
        ] + [c[0] for c in conv],
        out_specs=[
            pl.BlockSpec((nb, tt, D_MODEL), lambda t: (0, t, 0)),
            pl.BlockSpec((nb, WINDOW, KV_WIDTH), lambda t: (0, 0, 0)),
            pl.BlockSpec((nb, WINDOW, KV_WIDTH), lambda t: (0, 0, 0)),
            pl.BlockSpec((nb, LRU_WIDTH), lambda t: (0, 0)),
            pl.BlockSpec((nb, SUBLANES, LRU_WIDTH), lambda t: (0, 0, 0)),
        ] + [c[1] for c in conv],
        out_shape=[
            jax.ShapeDtypeStruct((nb, seq, D_MODEL), F32),
            jax.ShapeDtypeStruct((nb, WINDOW, KV_WIDTH), F32),
            jax.ShapeDtypeStruct((nb, WINDOW, KV_WIDTH), F32),
            jax.ShapeDtypeStruct((nb, LRU_WIDTH), F32),
            jax.ShapeDtypeStruct((nb, SUBLANES, LRU_WIDTH), F32),
        ] + [c[2] for c in conv],
        scratch_shapes=[
            pltpu.VMEM((N_HEADS, WINDOW, 2 * WINDOW), F32),
            pltpu.VMEM((N_HEADS, tt, 2 * WINDOW), F32),
            pltpu.VMEM((nb, 2 * WINDOW, KV_WIDTH), F32),
            pltpu.VMEM((nb, 2 * WINDOW, KV_WIDTH), F32),
            pltpu.VMEM((nb, tt + SUBLANES, LRU_WIDTH), F32),
            pltpu.VMEM((LRU_HEADS, tm, LRU_BLOCK), F32),
            pltpu.VMEM((LRU_HEADS, tm, LRU_BLOCK), F32),
            pltpu.VMEM((nb, LRU_WIDTH), F32),
            pltpu.VMEM((tm, LRU_WIDTH), BF16),
            pltpu.VMEM((tm, Q_WIDTH), BF16),
            pltpu.VMEM((tm, IN_WIDTH - XG0), F32),
            pltpu.VMEM((tm, Q_WIDTH), F32),
        ],
        compiler_params=pltpu.CompilerParams(
            dimension_semantics=("arbitrary",), vmem_limit_bytes=VMEM_LIMIT_BYTES),
        name=f"mixer_prompt_l{layer}",
    )(prm["t5"], prm["sinks"][layer], x, prm["vecs"], prm["bucket_p"], prm["seg"], prm["w_qkv"],
      w_in, prm["w_o_attn"], prm["w_ax"], prm["w_o_lru"], prm["w_out"],
      *[prm[name] for name in FFN_WEIGHT_NAMES])


def _mixer_sample_kernel(t5_ref, sinks_ref, x_ref, vec_ref, bucket_ref, seg_ref, wqkv_ref, win_ref,
                         woa_ref, wax_ref, wol_ref, wout_ref, ck_ref, cv_ref, h0_ref, sc_ref,
                         nk_all_ref, nv_all_ref,
                         x1_ref, nh_ref, xr_ref, nk_ref, nv_ref,
                         h_scr, q_scr, o_scr, mb_scr, kn_scr, vn_scr,
                         *, bb):
    del nk_all_ref, nv_all_ref
    i = pl.program_id(0)
    nkeys = bb * WINDOW

    @pl.when(i == 0)
    def _project():
        x = x_ref[...]
        h = _rms(x, vec_ref[R_LN1:R_LN1 + 1, :]).astype(BF16)
        h_scr[...] = h
        qkv = _dot(h, wqkv_ref[...])
        seg = seg_ref[...]
        q_scr[...] = _seg_rms(qkv[:, Q0:K0], seg,
                              vec_ref[R_QG:R_QG + 1, 0:Q_WIDTH] * (HEAD_DIM ** -0.5))
        kn_scr[...] = _seg_rms(qkv[:, K0:V0], seg[0:KV_WIDTH, 0:KV_WIDTH],
                               vec_ref[R_KG:R_KG + 1, 0:KV_WIDTH])
        vn_scr[...] = qkv[:, V0:QKV_WIDTH]
        bucket = bucket_ref[...]
        rowb = lax.broadcasted_iota(jnp.int32, (bb, nkeys), 0)
        colb = lax.broadcasted_iota(jnp.int32, (bb, nkeys), 1) // WINDOW
        for hh in range(N_HEADS):
            brow = _gather_bias(bucket, t5_ref, hh)
            brow = jnp.concatenate([brow] * bb, axis=1)
            mb_scr[hh * bb:(hh + 1) * bb, :] = jnp.where(rowb == colb, brow, NEG_INF)

    rows = pl.ds(pl.multiple_of(i * bb, bb), bb)
    lane = lax.broadcasted_iota(jnp.int32, (1, LANES), 1)
    low = lane < HEAD_DIM
    qblk = q_scr[rows, :]
    qz = []
    for kh in range(N_KV_HEADS):
        keep = low if kh == 0 else jnp.logical_not(low)
        for g in range(GROUP):
            qz.append(jnp.where(keep, qblk[:, g * LANES:(g + 1) * LANES], 0.0))
    qz = jnp.concatenate(qz, axis=0).astype(BF16)
    kt = jnp.concatenate([ck_ref[b] for b in range(bb)], axis=1).astype(BF16)
    vt = jnp.concatenate([cv_ref[b] for b in range(bb)], axis=1).astype(BF16)
    s = _dot(qz, kt) + mb_scr[...]
    kn_blk = kn_scr[rows, :]
    vn_blk = vn_scr[rows, :]
    knew = jnp.concatenate([kn_blk.astype(BF16).astype(F32)] * N_HEADS, axis=0)
    vnew = jnp.concatenate([vn_blk.astype(BF16).astype(F32)] * N_HEADS, axis=0)
    self_bias = jnp.concatenate(
        [jnp.full((bb, 1), t5_ref[hh], F32) for hh in range(N_HEADS)], axis=0)
    sink = jnp.concatenate(
        [jnp.full((bb, 1), sinks_ref[hh], F32) for hh in range(N_HEADS)], axis=0)
    s_self = jnp.sum(qz.astype(F32) * knew, axis=-1, keepdims=True) + self_bias
    m = jnp.maximum(jnp.maximum(jnp.max(s, axis=-1, keepdims=True), s_self), sink)
    e = jnp.exp(s - m)
    e_self = jnp.exp(s_self - m)
    den = jnp.sum(e, axis=-1, keepdims=True) + e_self + jnp.exp(sink - m)
    inv = 1.0 / den
    o = (_dot_nt((e * inv).astype(BF16), vt)
         + (e_self * inv).astype(BF16).astype(F32) * vnew)
    half = GROUP * bb
    for g in range(GROUP):
        og = jnp.where(low, o[g * bb:(g + 1) * bb], o[half + g * bb:half + (g + 1) * bb])
        o_scr[rows, g * LANES:(g + 1) * LANES] = og

    knt = kn_blk.T
    vnt = vn_blk.T
    last = lax.broadcasted_iota(jnp.int32, (1, WINDOW), 1) == WINDOW - 1
    for b in range(bb):
        nk_ref[b] = jnp.where(last, knt[:, b:b + 1], pltpu.roll(ck_ref[b], WINDOW - 1, 1))
        nv_ref[b] = jnp.where(last, vnt[:, b:b + 1], pltpu.roll(cv_ref[b], WINDOW - 1, 1))

    @pl.when(i == pl.num_programs(0) - 1)
    def _finish():
        x = x_ref[...]
        h = h_scr[...]
        m_att = _sigmoid(_dot(h, win_ref[:, GA0:GL0])) * _dot(o_scr[...].astype(BF16), woa_ref[...])
        xr = _dot(h, win_ref[:, XR0:XG0])
        xr_ref[...] = xr
        xc = vec_ref[R_CB:R_CB + 1, :] + vec_ref[R_CW0 + CONV_WIDTH - 1:R_CW0 + CONV_WIDTH, :] * xr
        for j in range(CONV_WIDTH - 1):
            xc = xc + vec_ref[R_CW0 + j:R_CW0 + j + 1, :] * sc_ref[j]
        c_row = -LRU_C * _softplus(-vec_ref[R_LAM:R_LAM + 1, :])
        hn = []
        for hd in range(LRU_HEADS):
            lc = slice(hd * LRU_BLOCK, (hd + 1) * LRU_BLOCK)
            a, bb_ = _lru_gates(xc[:, lc], wax_ref[hd], vec_ref[R_BA:R_BA + 1, lc],
                                vec_ref[R_BX:R_BX + 1, lc], c_row[:, lc])
            hn.append(a * h0_ref[:, lc] + bb_)
        hn = jnp.concatenate(hn, axis=1)
        nh_ref[...] = hn
        y = _gelu_times(_dot(h, win_ref[:, XG0:GA0]), hn).astype(BF16)
        m_all = m_att + _sigmoid(_dot(h, win_ref[:, GL0:IN_WIDTH])) * _dot(y, wol_ref[...])
        x1_ref[...] = x + _dot(m_all.astype(BF16), wout_ref[...])


def _mixer_sample(layer, x, w_in, ckt, cvt, h0, sc, prm, nk_buf, nv_buf):
    depth, nseq = ckt.shape[0], ckt.shape[1]
    bb = SAMPLE_BB
    assert nseq % bb == 0
    wl = lambda *z: lambda i: (layer,) + z
    full2 = lambda i: (0, 0)
    cache_spec = pl.BlockSpec((None, bb, KV_WIDTH, WINDOW), lambda i: (layer, i, 0, 0))
    n_in = 16
    kern = functools.partial(_mixer_sample_kernel, bb=bb)
    return pl.pallas_call(
        kern,
        grid=(nseq // bb,),
        in_specs=[
            _smem_spec(), _smem_spec(),
            _const_spec((nseq, D_MODEL), full2),
            _const_spec((None, VEC_ROWS, D_MODEL), wl(0, 0)),
            _const_spec((1, WINDOW), full2),
            _const_spec((Q_WIDTH, Q_WIDTH), full2),
            _const_spec((None, D_MODEL, QKV_WIDTH), wl(0, 0)),
            _const_spec((D_MODEL, IN_WIDTH), full2),
            _const_spec((None, Q_WIDTH, D_MODEL), wl(0, 0)),
            _const_spec((None, LRU_HEADS, LRU_BLOCK, 2 * LRU_BLOCK), wl(0, 0, 0)),
            _const_spec((None, LRU_WIDTH, D_MODEL), wl(0, 0)),
            _const_spec((None, D_MODEL, D_MODEL), wl(0, 0)),
            cache_spec, cache_spec,
            _const_spec((None, nseq, LRU_WIDTH), wl(0, 0)),
            _const_spec((None, CONV_WIDTH - 1, nseq, LRU_WIDTH), wl(0, 0, 0)),
            pl.BlockSpec(memory_space=pl.ANY), pl.BlockSpec(memory_space=pl.ANY),
        ],
        out_specs=[
            pl.BlockSpec((nseq, D_MODEL), full2),
            pl.BlockSpec((nseq, LRU_WIDTH), full2),
            pl.BlockSpec((nseq, LRU_WIDTH), full2),
            cache_spec, cache_spec,
        ],
        out_shape=[
            jax.ShapeDtypeStruct((nseq, D_MODEL), F32),
            jax.ShapeDtypeStruct((nseq, LRU_WIDTH), F32),
            jax.ShapeDtypeStruct((nseq, LRU_WIDTH), F32),
            jax.ShapeDtypeStruct((depth, nseq, KV_WIDTH, WINDOW), F32),
            jax.ShapeDtypeStruct((depth, nseq, KV_WIDTH, WINDOW), F32),
        ],
        input_output_aliases={n_in: 3, n_in + 1: 4},
        scratch_shapes=[
            pltpu.VMEM((nseq, D_MODEL), BF16),
            pltpu.VMEM((nseq, Q_WIDTH), F32),
            pltpu.VMEM((nseq, Q_WIDTH), F32),
            pltpu.VMEM((N_HEADS * bb, bb * WINDOW), F32),
            pltpu.VMEM((nseq, KV_WIDTH), F32),
            pltpu.VMEM((nseq, KV_WIDTH), F32),
        ],
        compiler_params=pltpu.CompilerParams(
            dimension_semantics=("arbitrary",), vmem_limit_bytes=VMEM_LIMIT_BYTES),
        name=f"mixer_sample_l{layer}",
    )(prm["t5"], prm["sinks"][layer], x, prm["vecs"], prm["bucket_s"], prm["seg"], prm["w_qkv"],
      w_in, prm["w_o_attn"], prm["w_ax"], prm["w_o_lru"], prm["w_out"], ckt, cvt, h0, sc,
      nk_buf, nv_buf)


FF_CHUNKS = ((0, 1024), (1024, 2048), (2048, D_FF))
FFN_WEIGHT_NAMES = ("w_gate", "w_up", "w_down", "w_ple_gate", "w_ple")


def _ffn_rows(x, p, vec_ref, wg_ref, wu_ref, wd_ref, wpg_ref, wp_ref, act_scr):
    h2 = (x * vec_ref[R_LN2:R_LN2 + 1, :]).astype(BF16)
    r2 = lax.rsqrt(jnp.mean(x * x, axis=-1, keepdims=True) + EPS)
    for lo, hi in FF_CHUNKS:
        g = _dot(h2, wg_ref[:, lo:hi]) * r2
        act = g * _sigmoid(g) * (_dot(h2, wu_ref[:, lo:hi]) * r2)
        act_scr[:, lo:hi] = act.astype(BF16)
    x = x + _dot(act_scr[...], wd_ref[...])
    h3 = (x * vec_ref[R_LN3:R_LN3 + 1, :]).astype(BF16)
    r3 = lax.rsqrt(jnp.mean(x * x, axis=-1, keepdims=True) + EPS)
    gate = _sigmoid(_dot(h3, wpg_ref[...]) * r3)
    return x + gate * _dot(p.astype(BF16), wp_ref[...])


def _ffn_kernel(x_ref, p_ref, xs_ref, ps_ref, vec_ref, wg_ref, wu_ref, wd_ref, wpg_ref, wp_ref,
                *rest, convert):
    if convert:
        win_f32, o_ref, os_ref, win_bf16, act_scr = rest
        _convert_blocks([win_f32], [win_bf16])
    else:
        o_ref, os_ref, act_scr = rest
    weights = (vec_ref, wg_ref, wu_ref, wd_ref, wpg_ref, wp_ref)
    o_ref[...] = _ffn_rows(x_ref[...], p_ref[...], *weights, act_scr)

    @pl.when(pl.program_id(0) == pl.num_programs(0) - 1)
    def _sample():
        ns = xs_ref.shape[0]
        os_ref[...] = _ffn_rows(xs_ref[...], ps_ref[...], *weights, act_scr.at[0:ns, :])


def _ffn(layer, x, p, xs, ps, ffn_w, prm):
    rows, ns = x.shape[0], xs.shape[0]
    tm = min(FFN_TM, rows)
    assert rows % tm == 0 and ns <= tm
    depth = prm["w_in_f32"].shape[0]
    convert = layer + 1 < depth
    conv = [_convert_specs(prm["w_in_f32"], layer + 1, rows // tm)] if convert else []
    wl = lambda *z: lambda r: (layer,) + z
    whole = lambda r: (0, 0)
    outs = pl.pallas_call(
        functools.partial(_ffn_kernel, convert=convert),
        grid=(rows // tm,),
        in_specs=[
            pl.BlockSpec((tm, D_MODEL), lambda r: (r, 0)),
            pl.BlockSpec((None, tm, PLE_DIM), lambda r: (layer, r, 0)),
            _const_spec((ns, D_MODEL), whole),
            _const_spec((None, ns, PLE_DIM), wl(0, 0)),
            _const_spec((None, VEC_ROWS, D_MODEL), wl(0, 0)),
            _const_spec((D_MODEL, D_FF), whole),
            _const_spec((D_MODEL, D_FF), whole),
            _const_spec((D_FF, D_MODEL), whole),
            _const_spec((D_MODEL, D_MODEL), whole),
            _const_spec((PLE_DIM, D_MODEL), whole),
        ] + [c[0] for c in conv],
        out_specs=[
            pl.BlockSpec((tm, D_MODEL), lambda r: (r, 0)),
            pl.BlockSpec((ns, D_MODEL), whole),
        ] + [c[1] for c in conv],
        out_shape=[
            jax.ShapeDtypeStruct((rows, D_MODEL), F32),
            jax.ShapeDtypeStruct((ns, D_MODEL), F32),
        ] + [c[2] for c in conv],
        scratch_shapes=[pltpu.VMEM((tm, D_FF), BF16)],
        compiler_params=pltpu.CompilerParams(
            dimension_semantics=("arbitrary",), vmem_limit_bytes=FFN_VMEM_LIMIT_BYTES),
        name=f"ffn_l{layer}",
    )(x, p, xs, ps, prm["vecs"], *ffn_w, *([prm["w_in_f32"]] if convert else []))
    return outs[0], outs[1], (outs[2] if convert else None)


def _t5_bucket(dist):
    n = np.maximum(dist, 0)
    max_exact = N_BUCKETS // 2
    nf = np.maximum(n, 1).astype(np.float32)
    large = max_exact + (np.log(nf / max_exact) / math.log(MAX_DISTANCE / max_exact)
                         * (N_BUCKETS - max_exact)).astype(np.int32)
    large = np.minimum(large, N_BUCKETS - 1)
    return np.where(n < max_exact, n, large)


def _bucket_table(dist):
    return np.where((dist >= 0) & (dist < WINDOW), _t5_bucket(dist), -1).astype(np.int32)


def _regroup_heads(w, axis):
    shape = w.shape
    w = w.reshape(shape[:axis] + (N_KV_HEADS, GROUP, HEAD_DIM) + shape[axis + 1:])
    return jnp.swapaxes(w, axis, axis + 1).reshape(shape)


def _prepare(t5_table, ln1, w_in, q_gain, k_gain, sinks, w_o_attn, conv_w, conv_b, w_a, b_a,
             w_x, b_x, lam, w_o_lru, w_out, ln2, w_gate, w_up, w_down, ln3, w_ple, w_ple_gate):
    depth = w_in.shape[0]
    w_att = lax.optimization_barrier(w_in[:, :, 0:QKV_WIDTH])
    w_qkv = jnp.concatenate(
        [_regroup_heads(w_att[:, :, 0:Q_WIDTH], 2), w_att[:, :, Q_WIDTH:QKV_WIDTH]], axis=2).astype(BF16)

    def row(v):
        return jnp.pad(v, ((0, 0), (0, D_MODEL - v.shape[1])))[:, None, :]

    parts = [(R_LN1, row(ln1)), (R_CW0, conv_w), (R_CB, row(conv_b)), (R_BA, row(b_a)),
             (R_BX, row(b_x)), (R_LAM, row(lam)), (R_QG, row(jnp.tile(q_gain, (1, N_HEADS)))),
             (R_KG, row(jnp.tile(k_gain, (1, N_KV_HEADS)))), (R_LN2, row(ln2)), (R_LN3, row(ln3))]
    used = 0
    for first_row, part in parts:
        assert first_row == used
        used += part.shape[1]
    vecs = jnp.concatenate([part for _, part in parts]
                           + [jnp.zeros((depth, VEC_ROWS - used, D_MODEL), F32)], axis=1)
    head_id = np.arange(Q_WIDTH) // HEAD_DIM
    seg = jnp.asarray(head_id[:, None] == head_id[None, :], BF16)
    bucket_p = _bucket_table((WINDOW + np.arange(WINDOW))[:, None] - np.arange(2 * WINDOW)[None, :])
    bucket_s = _bucket_table((WINDOW - np.arange(WINDOW))[None, :])
    return {
        "t5": t5_table.reshape(-1),
        "sinks": sinks,
        "vecs": vecs,
        "seg": seg,
        "bucket_p": jnp.asarray(bucket_p),
        "bucket_s": jnp.asarray(bucket_s),
        "w_qkv": w_qkv,
        "w_in_f32": w_in,
        "w_in_first": lax.optimization_barrier(w_in[0]).astype(BF16),
        "w_o_attn": _regroup_heads(w_o_attn, 1).astype(BF16),
        "w_ax": jnp.concatenate([w_a, w_x], axis=-1).astype(BF16),
        "w_o_lru": w_o_lru.astype(BF16),
        "w_out": w_out.astype(BF16),
        "w_gate": w_gate,
        "w_up": w_up,
        "w_down": w_down,
        "w_ple_gate": w_ple_gate,
        "w_ple": w_ple,
    }


def kernel(x_prompt, x_sample, cache_k_win, cache_v_win, state_lru_h, state_conv, p_prompt,
           p_sample, t5_table, ln1, w_in, q_gain, k_gain, sinks, w_o_attn, conv_w, conv_b, w_a,
           b_a, w_x, b_x, lam, w_o_lru, w_out, ln2, w_gate, w_up, w_down, ln3, w_ple, w_ple_gate):
    depth = w_in.shape[0]
    nb, seq, _ = x_prompt.shape
    nseq = x_sample.shape[0]
    assert x_sample.shape[1] == 1 and cache_k_win.shape[2] == WINDOW
    prm = _prepare(t5_table, ln1, w_in, q_gain, k_gain, sinks, w_o_attn, conv_w, conv_b, w_a, b_a,
                   w_x, b_x, lam, w_o_lru, w_out, ln2, w_gate, w_up, w_down, ln3, w_ple, w_ple_gate)
    ckt = jnp.transpose(cache_k_win, (0, 1, 3, 4, 2)).reshape(depth, nseq, KV_WIDTH, WINDOW)
    cvt = jnp.transpose(cache_v_win, (0, 1, 3, 4, 2)).reshape(depth, nseq, KV_WIDTH, WINDOW)
    sc = jnp.transpose(state_conv, (0, 2, 1, 3))
    pp = p_prompt.reshape(depth, nb * seq, PLE_DIM)
    ps = p_sample.reshape(depth, nseq, PLE_DIM)

    yp = x_prompt
    ys = x_sample.reshape(nseq, D_MODEL)
    outs = [[] for _ in range(6)]
    nk_buf = jnp.zeros((depth, nseq, KV_WIDTH, WINDOW), F32)
    nv_buf = jnp.zeros((depth, nseq, KV_WIDTH, WINDOW), F32)
    w_in_l = prm["w_in_first"]
    for layer in range(depth):
        x1, nk, nv, nh, nc, *ffn_w = _mixer_prompt(layer, yp, w_in_l, prm)
        outs[0].append(nk.reshape(nb, WINDOW, N_KV_HEADS, HEAD_DIM))
        outs[1].append(nv.reshape(nb, WINDOW, N_KV_HEADS, HEAD_DIM))
        outs[2].append(nh)
        outs[3].append(nc[:, SUBLANES - (CONV_WIDTH - 1):, :])

        x1s, nhs, xr, nk_buf, nv_buf = _mixer_sample(layer, ys, w_in_l, ckt, cvt, state_lru_h, sc,
                                                     prm, nk_buf, nv_buf)
        yp, ys, w_in_l = _ffn(layer, x1.reshape(nb * seq, D_MODEL), pp, x1s, ps, ffn_w, prm)
        yp = yp.reshape(nb, seq, D_MODEL)
        outs[4].append(nhs)
        outs[5].append(jnp.concatenate([state_conv[layer, :, 1:], xr[:, None, :]], axis=1))

    def untranspose(buf):
        return jnp.transpose(buf.reshape(depth, nseq, N_KV_HEADS, HEAD_DIM, WINDOW), (0, 1, 4, 2, 3))

    stacked = [jnp.stack(o) for o in outs]
    return (yp, ys.reshape(nseq, 1, D_MODEL), stacked[0], stacked[1], stacked[2], stacked[3],
            untranspose(nk_buf), untranspose(nv_buf), stacked[4], stacked[5])
```

```python
import functools
import math

import numpy as np
import jax
import jax.numpy as jnp
from jax import lax
from jax.experimental import pallas as pl
from jax.experimental.pallas import tpu as pltpu

F32 = jnp.float32
BF16 = jnp.bfloat16

D_MODEL = 1024
HEAD_DIM = 64
N_HEADS = 8
N_KV_HEADS = 2
GROUP = N_HEADS // N_KV_HEADS
Q_WIDTH = N_HEADS * HEAD_DIM
KV_WIDTH = N_KV_HEADS * HEAD_DIM
WINDOW = 128
N_BUCKETS = 32
MAX_DISTANCE = 128
LRU_WIDTH = D_MODEL
LRU_HEADS = 8
LRU_BLOCK = LRU_WIDTH // LRU_HEADS
LRU_C = 8.0
CONV_WIDTH = 4
D_FF = 2816
PLE_DIM = 256
EPS = 1e-6
NEG_INF = -1e30
TINY = 1e-30
LOG2E = math.log2(math.e)
GELU_C = math.sqrt(2.0 / math.pi)

SUBLANES = 8
BF16_SUBLANES = 16
LANES = 128
VMEM_LIMIT_BYTES = 56 * 1024 * 1024
FFN_VMEM_LIMIT_BYTES = 58 * 1024 * 1024

Q0 = 0
K0 = Q0 + Q_WIDTH
V0 = K0 + KV_WIDTH
QKV_WIDTH = V0 + KV_WIDTH
XR0 = QKV_WIDTH
XG0 = XR0 + LRU_WIDTH
GA0 = XG0 + LRU_WIDTH
GL0 = GA0 + D_MODEL
IN_WIDTH = GL0 + D_MODEL

R_LN1, R_CW0, R_CB, R_BA, R_BX, R_LAM, R_QG, R_KG, R_LN2, R_LN3 = 0, 1, 5, 6, 7, 8, 9, 10, 11, 12
VEC_ROWS = 16

PROMPT_TT = 64
FFN_TM = 1024
SAMPLE_BB = 16
PROJ_CHUNK = 512
N_FFN_WEIGHTS = 5


def _dot(a, b):
    return jnp.dot(a, b, preferred_element_type=F32)


def _dot_nt(a, b):
    return lax.dot_general(a, b, (((1,), (1,)), ((), ())), preferred_element_type=F32)


def _rms(x, g):
    ms = jnp.mean(x * x, axis=-1, keepdims=True)
    return x * lax.rsqrt(ms + EPS) * g


def _seg_rms(x, seg, g):
    x2 = x * x
    ssq = _dot(x2.astype(BF16), seg)
    return x * lax.rsqrt(ssq * (1.0 / HEAD_DIM) + EPS) * g


def _sigmoid(x):
    return 0.5 * jnp.tanh(0.5 * x) + 0.5


def _gelu_times(x, y):
    t = jnp.tanh(x * (GELU_C + (GELU_C * 0.044715) * (x * x)))
    hxy = (0.5 * x) * y
    return hxy + hxy * t


def _softplus(z):
    return jnp.maximum(z, 0.0) + jnp.log1p(jnp.exp(-jnp.abs(z)))


def _gather_bias(bucket, t5_ref, head):
    acc = jnp.full(bucket.shape, NEG_INF, F32)
    for bkt in range(N_BUCKETS):
        acc = jnp.where(bucket == bkt, t5_ref[bkt * N_HEADS + head], acc)
    return acc


def _lru_gates(xch, wax, b_a, b_x, c_row):
    gts = _dot(xch.astype(BF16), wax)
    r = _sigmoid(gts[:, :LRU_BLOCK] + b_a)
    ig = _sigmoid(gts[:, LRU_BLOCK:] + b_x)
    log_a = c_row * r
    a = jnp.exp(log_a)
    y = 1.0 - a * a
    return a, y * lax.rsqrt(jnp.maximum(y, TINY)) * (ig * xch)


def _rows_per_step(rows, nsteps):
    need = -(-rows // nsteps)
    for per in range(BF16_SUBLANES, rows + 1, BF16_SUBLANES):
        if rows % per == 0 and per >= need:
            return per
    raise ValueError((rows, nsteps))


def _convert_specs(stacked, layer, nsteps):
    _, rows, cols = stacked.shape
    per = _rows_per_step(rows, nsteps)
    last = rows // per - 1
    return (pl.BlockSpec((None, per, cols), lambda t: (layer, jnp.minimum(t, last), 0)),
            pl.BlockSpec((per, cols), lambda t: (jnp.minimum(t, last), 0)),
            jax.ShapeDtypeStruct((rows, cols), BF16))


def _convert_blocks(src_refs, dst_refs):
    for src, dst in zip(src_refs, dst_refs):
        dst[...] = src[...].astype(BF16)


def _mixer_prompt_kernel(t5_ref, sinks_ref, x_ref, vec_ref, bucket_ref, seg_ref, wqkv_ref, win_ref,
                         woa_ref, wax_ref, wol_ref, wout_ref, *rest, nb, tt):
    ffn_f32 = rest[0:N_FFN_WEIGHTS]
    x1_ref, nk_ref, nv_ref, nh_ref, nc_ref = rest[N_FFN_WEIGHTS:N_FFN_WEIGHTS + 5]
    ffn_bf16 = rest[N_FFN_WEIGHTS + 5:2 * N_FFN_WEIGHTS + 5]
    (bias_scr, bias_t, kband, vband, cs, a_scr, b_scr, hst, y_scr, o_scr,
     proj_scr, q_scr) = rest[2 * N_FFN_WEIGHTS + 5:]
    tm = nb * tt
    t = pl.program_id(0)
    _convert_blocks(ffn_f32, ffn_bf16)

    @pl.when(t == 0)
    def _init():
        bucket = bucket_ref[...]
        for hh in range(N_HEADS):
            bias_scr[hh] = _gather_bias(bucket, t5_ref, hh) * LOG2E
        kband[...] = jnp.zeros((nb, 2 * WINDOW, KV_WIDTH), F32)
        vband[...] = jnp.zeros((nb, 2 * WINDOW, KV_WIDTH), F32)
        cs[:, 0:SUBLANES, :] = jnp.zeros((nb, SUBLANES, LRU_WIDTH), F32)
        hst[...] = jnp.zeros((nb, LRU_WIDTH), F32)

    x = x_ref[...].reshape(tm, D_MODEL)
    h = (x * vec_ref[R_LN1:R_LN1 + 1, :]).astype(BF16)
    rstd = lax.rsqrt(jnp.mean(x * x, axis=-1, keepdims=True) + EPS)

    cs[:, SUBLANES:SUBLANES + tt, :] = (_dot(h, win_ref[:, XR0:XG0]) * rstd).reshape(nb, tt, LRU_WIDTH)
    c_row = -LRU_C * _softplus(-vec_ref[R_LAM:R_LAM + 1, :])
    n_chunks = (IN_WIDTH - XG0) // PROJ_CHUNK
    qkv = None
    for hd in range(LRU_HEADS):
        lc = slice(hd * LRU_BLOCK, (hd + 1) * LRU_BLOCK)
        xc = vec_ref[R_CB:R_CB + 1, lc]
        for j in range(CONV_WIDTH):
            off = SUBLANES - (CONV_WIDTH - 1) + j
            xc = xc + vec_ref[R_CW0 + j:R_CW0 + j + 1, lc] * cs[:, off:off + tt, lc]
        a, bb = _lru_gates(xc.reshape(tm, LRU_BLOCK), wax_ref[hd], vec_ref[R_BA:R_BA + 1, lc],
                           vec_ref[R_BX:R_BX + 1, lc], c_row[:, lc])
        for b in range(nb):
            rows = pl.ds(b, tt, stride=nb)
            a_scr[hd, rows, :] = a[b * tt:(b + 1) * tt]
            b_scr[hd, rows, :] = bb[b * tt:(b + 1) * tt]
        if hd == 0:
            qkv = _dot(h, wqkv_ref[...]) * rstd
        elif hd == 1:
            seg = seg_ref[...]
            q_scr[...] = _seg_rms(qkv[:, Q0:K0], seg,
                                  vec_ref[R_QG:R_QG + 1, 0:Q_WIDTH] * (LOG2E * HEAD_DIM ** -0.5))
            kn = _seg_rms(qkv[:, K0:V0], seg[0:KV_WIDTH, 0:KV_WIDTH],
                          vec_ref[R_KG:R_KG + 1, 0:KV_WIDTH])
            kband[:, WINDOW:WINDOW + tt, :] = kn.reshape(nb, tt, KV_WIDTH)
            vband[:, WINDOW:WINDOW + tt, :] = qkv[:, V0:QKV_WIDTH].reshape(nb, tt, KV_WIDTH)
            nk_ref[...] = kband[:, tt:tt + WINDOW, :]
            nv_ref[...] = vband[:, tt:tt + WINDOW, :]
        if hd >= LRU_HEADS - n_chunks:
            c0 = (hd - (LRU_HEADS - n_chunks)) * PROJ_CHUNK
            proj_scr[:, c0:c0 + PROJ_CHUNK] = _dot(h, win_ref[:, XG0 + c0:XG0 + c0 + PROJ_CHUNK]) * rstd
    tail = cs[:, tt:tt + SUBLANES, :]
    nc_ref[...] = tail
    cs[:, 0:SUBLANES, :] = tail

    col = lax.broadcasted_iota(jnp.int32, (1, 2 * WINDOW), 1)
    colmask = jnp.where(col >= WINDOW - t * tt, 0.0, NEG_INF)
    for hh in range(N_HEADS):
        bias_t[hh] = bias_scr[hh, 0:tt, :] + colmask

    lane = lax.broadcasted_iota(jnp.int32, (1, LANES), 1)
    low = lane < HEAD_DIM
    for b in range(nb):
        kb = kband[b].astype(BF16)
        vb = vband[b].astype(BF16)
        o_kv = []
        q8 = jnp.concatenate(
            [jnp.where(low if kh == 0 else jnp.logical_not(low),
                       q_scr[b * tt:(b + 1) * tt, g * LANES:(g + 1) * LANES], 0.0)
             for kh in range(N_KV_HEADS) for g in range(GROUP)], axis=0).astype(BF16)
        s_all = _dot_nt(q8, kb)
        for kh in range(N_KV_HEADS):
            s = s_all[kh * GROUP * tt:(kh + 1) * GROUP * tt]
            ps, invs = [], []
            for g in range(GROUP):
                hh = kh * GROUP + g
                sg = s[g * tt:(g + 1) * tt] + bias_t[hh]
                sink = sinks_ref[hh] * LOG2E
                m = jnp.maximum(jnp.max(sg, axis=-1, keepdims=True), sink)
                e = jnp.exp2(sg - m)
                invs.append(1.0 / (jnp.sum(e, axis=-1, keepdims=True) + jnp.exp2(sink - m)))
                ps.append(e.astype(BF16))
            o4 = _dot(jnp.concatenate(ps, axis=0), vb)
            o_kv.append([o4[g * tt:(g + 1) * tt] * invs[g] for g in range(GROUP)])
        for g in range(GROUP):
            og = jnp.where(low, o_kv[0][g], o_kv[1][g])
            o_scr[b * tt:(b + 1) * tt, g * LANES:(g + 1) * LANES] = og.astype(BF16)

    kband[:, 0:WINDOW, :] = kband[:, tt:tt + WINDOW, :]
    vband[:, 0:WINDOW, :] = vband[:, tt:tt + WINDOW, :]

    m_att = _sigmoid(proj_scr[:, GA0 - XG0:GL0 - XG0]) * _dot(o_scr[...], woa_ref[...])

    hs = [hst[:, hd * LRU_BLOCK:(hd + 1) * LRU_BLOCK] for hd in range(LRU_HEADS)]
    for ts in range(tt):
        rows = slice(ts * nb, (ts + 1) * nb)
        for hd in range(LRU_HEADS):
            hs[hd] = a_scr[hd, rows, :] * hs[hd] + b_scr[hd, rows, :]
            b_scr[hd, rows, :] = hs[hd]
    hfin = jnp.concatenate(hs, axis=1)
    hst[...] = hfin
    nh_ref[...] = hfin

    for hd in range(LRU_HEADS):
        lc = slice(hd * LRU_BLOCK, (hd + 1) * LRU_BLOCK)
        hseq = jnp.concatenate([b_scr[hd, pl.ds(b, tt, stride=nb), :] for b in range(nb)], axis=0)
        y_scr[:, lc] = _gelu_times(proj_scr[:, lc], hseq).astype(BF16)

    m_all = m_att + _sigmoid(proj_scr[:, GL0 - XG0:IN_WIDTH - XG0]) * _dot(y_scr[...], wol_ref[...])
    x1 = x + _dot(m_all.astype(BF16), wout_ref[...])
    x1_ref[...] = x1.reshape(nb, tt, D_MODEL)


def _const_spec(shape, index):
    return pl.BlockSpec(shape, index, pipeline_mode=pl.Buffered(1))


def _smem_spec():
    return pl.BlockSpec(memory_space=pltpu.SMEM)


def _mixer_prompt(layer, x, w_in, prm):
    nb, seq, _ = x.shape
    tt = PROMPT_TT
    assert nb == SUBLANES and seq % tt == 0 and WINDOW % tt == 0
    tm = nb * tt
    wl = lambda *z: lambda t: (layer,) + z
    kern = functools.partial(_mixer_prompt_kernel, nb=nb, tt=tt)
    conv = [_convert_specs(prm[name], layer, seq // tt) for name in FFN_WEIGHT_NAMES]
    return pl.pallas_call(
        kern,
        grid=(seq // tt,),
        in_specs=[
            _smem_spec(), _smem_spec(),
            pl.BlockSpec((nb, tt, D_MODEL), lambda t: (0, t, 0)),
            _const_spec((None, VEC_ROWS, D_MODEL), wl(0, 0)),
            _const_spec((WINDOW, 2 * WINDOW), lambda t: (0, 0)),
            _const_spec((Q_WIDTH, Q_WIDTH), lambda t: (0, 0)),
            _const_spec((None, D_MODEL, QKV_WIDTH), wl(0, 0)),
            _const_spec((D_MODEL, IN_WIDTH), lambda t: (0, 0)),
            _const_spec((None, Q_WIDTH, D_MODEL), wl(0, 0)),
            _const_spec((None, LRU_HEADS, LRU_BLOCK, 2 * LRU_BLOCK), wl(0, 0, 0)),
            _const_spec((None, LRU_WIDTH, D_MODEL), wl(0, 0)),
            _const_spec((None, D_MODEL, D_MODEL), wl(0, 0)),
        ] + [c[0] for c in conv],
        out_specs=[
            pl.BlockSpec((nb, tt, D_MODEL), lambda t: (0, t, 0)),
            pl.BlockSpec((nb, WINDOW, KV_WIDTH), lambda t: (0, 0, 0)),
            pl.BlockSpec((nb, WINDOW, KV_WIDTH), lambda t: (0, 0, 0)),
            pl.BlockSpec((nb, LRU_WIDTH), lambda t: (0, 0)),
            pl.BlockSpec((nb, SUBLANES, LRU_WIDTH), lambda t: (0, 0, 0)),
        ] + [c[1] for c in conv],
        out_shape=[
            jax.ShapeDtypeStruct((nb, seq, D_MODEL), F32),
            jax.ShapeDtypeStruct((nb, WINDOW, KV_WIDTH), F32),
            jax.ShapeDtypeStruct((nb, WINDOW, KV_WIDTH), F32),
            jax.ShapeDtypeStruct((nb, LRU_WIDTH), F32),
            jax.ShapeDtypeStruct((nb, SUBLANES, LRU_WIDTH), F32),
        ] + [c[2] for c in conv],
        scratch_shapes=[
            pltpu.VMEM((N_HEADS, WINDOW, 2 * WINDOW), F32),
            pltpu.VMEM((N_HEADS, tt, 2 * WINDOW), F32),
            pltpu.VMEM((nb, 2 * WINDOW, KV_WIDTH), F32),
            pltpu.VMEM((nb, 2 * WINDOW, KV_WIDTH), F32),
            pltpu.VMEM((nb, tt + SUBLANES, LRU_WIDTH), F32),
            pltpu.VMEM((LRU_HEADS, tm, LRU_BLOCK), F32),
            pltpu.VMEM((LRU_HEADS, tm, LRU_BLOCK), F32),
            pltpu.VMEM((nb, LRU_WIDTH), F32),
            pltpu.VMEM((tm, LRU_WIDTH), BF16),
            pltpu.VMEM((tm, Q_WIDTH), BF16),
            pltpu.VMEM((tm, IN_WIDTH - XG0), F32),
            pltpu.VMEM((tm, Q_WIDTH), F32),
        ],
        compiler_params=pltpu.CompilerParams(
            dimension_semantics=("arbitrary",), vmem_limit_bytes=VMEM_LIMIT_BYTES),
        name=f"mixer_prompt_l{layer}",
    )(prm["t5"], prm["sinks"][layer], x, prm["vecs"], prm["bucket_p"], prm["seg"], prm["w_qkv"],
      w_in, prm["w_o_attn"], prm["w_ax"], prm["w_o_lru"], prm["w_out"],
      *[prm[name] for name in FFN_WEIGHT_NAMES])


def _mixer_sample_kernel(t5_ref, sinks_ref, x_ref, vec_ref, bucket_ref, seg_ref, wqkv_ref, win_ref,
                         woa_ref, wax_ref, wol_ref, wout_ref, ck_ref, cv_ref, h0_ref, sc_ref,
                         nk_all_ref, nv_all_ref,
                         x1_ref, nh_ref, xr_ref, nk_ref, nv_ref,
                         h_scr, q_scr, o_scr, mb_scr, kn_scr, vn_scr,
                         *, bb):
    del nk_all_ref, nv_all_ref
    i = pl.program_id(0)
    nkeys = bb * WINDOW

    @pl.when(i == 0)
    def _project():
        x = x_ref[...]
        h = _rms(x, vec_ref[R_LN1:R_LN1 + 1, :]).astype(BF16)
        h_scr[...] = h
        qkv = _dot(h, wqkv_ref[...])
        seg = seg_ref[...]
        q_scr[...] = _seg_rms(qkv[:, Q0:K0], seg,
                              vec_ref[R_QG:R_QG + 1, 0:Q_WIDTH] * (HEAD_DIM ** -0.5))
        kn_scr[...] = _seg_rms(qkv[:, K0:V0], seg[0:KV_WIDTH, 0:KV_WIDTH],
                               vec_ref[R_KG:R_KG + 1, 0:KV_WIDTH])
        vn_scr[...] = qkv[:, V0:QKV_WIDTH]
        bucket = bucket_ref[...]
        rowb = lax.broadcasted_iota(jnp.int32, (bb, nkeys), 0)
        colb = lax.broadcasted_iota(jnp.int32, (bb, nkeys), 1) // WINDOW
        for hh in range(N_HEADS):
            brow = _gather_bias(bucket, t5_ref, hh)
            brow = jnp.concatenate([brow] * bb, axis=1)
            mb_scr[hh * bb:(hh + 1) * bb, :] = jnp.where(rowb == colb, brow, NEG_INF)

    rows = pl.ds(pl.multiple_of(i * bb, bb), bb)
    lane = lax.broadcasted_iota(jnp.int32, (1, LANES), 1)
    low = lane < HEAD_DIM
    qblk = q_scr[rows, :]
    qz = []
    for kh in range(N_KV_HEADS):
        keep = low if kh == 0 else jnp.logical_not(low)
        for g in range(GROUP):
            qz.append(jnp.where(keep, qblk[:, g * LANES:(g + 1) * LANES], 0.0))
    qz = jnp.concatenate(qz, axis=0).astype(BF16)
    kt = jnp.concatenate([ck_ref[b] for b in range(bb)], axis=1).astype(BF16)
    vt = jnp.concatenate([cv_ref[b] for b in range(bb)], axis=1).astype(BF16)
    s = _dot(qz, kt) + mb_scr[...]
    kn_blk = kn_scr[rows, :]
    vn_blk = vn_scr[rows, :]
    knew = jnp.concatenate([kn_blk.astype(BF16).astype(F32)] * N_HEADS, axis=0)
    vnew = jnp.concatenate([vn_blk.astype(BF16).astype(F32)] * N_HEADS, axis=0)
    self_bias = jnp.concatenate(
        [jnp.full((bb, 1), t5_ref[hh], F32) for hh in range(N_HEADS)], axis=0)
    sink = jnp.concatenate(
        [jnp.full((bb, 1), sinks_ref[hh], F32) for hh in range(N_HEADS)], axis=0)
    s_self = jnp.sum(qz.astype(F32) * knew, axis=-1, keepdims=True) + self_bias
    m = jnp.maximum(jnp.maximum(jnp.max(s, axis=-1, keepdims=True), s_self), sink)
    e = jnp.exp(s - m)
    e_self = jnp.exp(s_self - m)
    den = jnp.sum(e, axis=-1, keepdims=True) + e_self + jnp.exp(sink - m)
    inv = 1.0 / den
    o = (_dot_nt((e * inv).astype(BF16), vt)
         + (e_self * inv).astype(BF16).astype(F32) * vnew)
    half = GROUP * bb
    for g in range(GROUP):
        og = jnp.where(low, o[g * bb:(g + 1) * bb], o[half + g * bb:half + (g + 1) * bb])
        o_scr[rows, g * LANES:(g + 1) * LANES] = og

    knt = kn_blk.T
    vnt = vn_blk.T
    last = lax.broadcasted_iota(jnp.int32, (1, WINDOW), 1) == WINDOW - 1
    for b in range(bb):
        nk_ref[b] = jnp.where(last, knt[:, b:b + 1], pltpu.roll(ck_ref[b], WINDOW - 1, 1))
        nv_ref[b] = jnp.where(last, vnt[:, b:b + 1], pltpu.roll(cv_ref[b], WINDOW - 1, 1))

    @pl.when(i == pl.num_programs(0) - 1)
    def _finish():
        x = x_ref[...]
        h = h_scr[...]
        m_att = _sigmoid(_dot(h, win_ref[:, GA0:GL0])) * _dot(o_scr[...].astype(BF16), woa_ref[...])
        xr = _dot(h, win_ref[:, XR0:XG0])
        xr_ref[...] = xr
        xc = vec_ref[R_CB:R_CB + 1, :] + vec_ref[R_CW0 + CONV_WIDTH - 1:R_CW0 + CONV_WIDTH, :] * xr
        for j in range(CONV_WIDTH - 1):
            xc = xc + vec_ref[R_CW0 + j:R_CW0 + j + 1, :] * sc_ref[j]
        c_row = -LRU_C * _softplus(-vec_ref[R_LAM:R_LAM + 1, :])
        hn = []
        for hd in range(LRU_HEADS):
            lc = slice(hd * LRU_BLOCK, (hd + 1) * LRU_BLOCK)
            a, bb_ = _lru_gates(xc[:, lc], wax_ref[hd], vec_ref[R_BA:R_BA + 1, lc],
                                vec_ref[R_BX:R_BX + 1, lc], c_row[:, lc])
            hn.append(a * h0_ref[:, lc] + bb_)
        hn = jnp.concatenate(hn, axis=1)
        nh_ref[...] = hn
        y = _gelu_times(_dot(h, win_ref[:, XG0:GA0]), hn).astype(BF16)
        m_all = m_att + _sigmoid(_dot(h, win_ref[:, GL0:IN_WIDTH])) * _dot(y, wol_ref[...])
        x1_ref[...] = x + _dot(m_all.astype(BF16), wout_ref[...])


def _mixer_sample(layer, x, w_in, ckt, cvt, h0, sc, prm, nk_buf, nv_buf):
    depth, nseq = ckt.shape[0], ckt.shape[1]
    bb = SAMPLE_BB
    assert nseq % bb == 0
    wl = lambda *z: lambda i: (layer,) + z
    full2 = lambda i: (0, 0)
    cache_spec = pl.BlockSpec((None, bb, KV_WIDTH, WINDOW), lambda i: (layer, i, 0, 0))
    n_in = 16
    kern = functools.partial(_mixer_sample_kernel, bb=bb)
    return pl.pallas_call(
        kern,
        grid=(nseq // bb,),
        in_specs=[
            _smem_spec(), _smem_spec(),
            _const_spec((nseq, D_MODEL), full2),
            _const_spec((None, VEC_ROWS, D_MODEL), wl(0, 0)),
            _const_spec((1, WINDOW), full2),
            _const_spec((Q_WIDTH, Q_WIDTH), full2),
            _const_spec((None, D_MODEL, QKV_WIDTH), wl(0, 0)),
            _const_spec((D_MODEL, IN_WIDTH), full2),
            _const_spec((None, Q_WIDTH, D_MODEL), wl(0, 0)),
            _const_spec((None, LRU_HEADS, LRU_BLOCK, 2 * LRU_BLOCK), wl(0, 0, 0)),
            _const_spec((None, LRU_WIDTH, D_MODEL), wl(0, 0)),
            _const_spec((None, D_MODEL, D_MODEL), wl(0, 0)),
            cache_spec, cache_spec,
            _const_spec((None, nseq, LRU_WIDTH), wl(0, 0)),
            _const_spec((None, CONV_WIDTH - 1, nseq, LRU_WIDTH), wl(0, 0, 0)),
            pl.BlockSpec(memory_space=pl.ANY), pl.BlockSpec(memory_space=pl.ANY),
        ],
        out_specs=[
            pl.BlockSpec((nseq, D_MODEL), full2),
            pl.BlockSpec((nseq, LRU_WIDTH), full2),
            pl.BlockSpec((nseq, LRU_WIDTH), full2),
            cache_spec, cache_spec,
        ],
        out_shape=[
            jax.ShapeDtypeStruct((nseq, D_MODEL), F32),
            jax.ShapeDtypeStruct((nseq, LRU_WIDTH), F32),
            jax.ShapeDtypeStruct((nseq, LRU_WIDTH), F32),
            jax.ShapeDtypeStruct((depth, nseq, KV_WIDTH, WINDOW), F32),
            jax.ShapeDtypeStruct((depth, nseq, KV_WIDTH, WINDOW), F32),
        ],
        input_output_aliases={n_in: 3, n_in + 1: 4},
        scratch_shapes=[
            pltpu.VMEM((nseq, D_MODEL), BF16),
            pltpu.VMEM((nseq, Q_WIDTH), F32),
            pltpu.VMEM((nseq, Q_WIDTH), F32),
            pltpu.VMEM((N_HEADS * bb, bb * WINDOW), F32),
            pltpu.VMEM((nseq, KV_WIDTH), F32),
            pltpu.VMEM((nseq, KV_WIDTH), F32),
        ],
        compiler_params=pltpu.CompilerParams(
            dimension_semantics=("arbitrary",), vmem_limit_bytes=VMEM_LIMIT_BYTES),
        name=f"mixer_sample_l{layer}",
    )(prm["t5"], prm["sinks"][layer], x, prm["vecs"], prm["bucket_s"], prm["seg"], prm["w_qkv"],
      w_in, prm["w_o_attn"], prm["w_ax"], prm["w_o_lru"], prm["w_out"], ckt, cvt, h0, sc,
      nk_buf, nv_buf)


FF_CHUNKS = ((0, 1024), (1024, 2048), (2048, D_FF))
FFN_WEIGHT_NAMES = ("w_gate", "w_up", "w_down", "w_ple_gate", "w_ple")


def _ffn_rows(x, p, vec_ref, wg_ref, wu_ref, wd_ref, wpg_ref, wp_ref, act_scr):
    h2 = (x * vec_ref[R_LN2:R_LN2 + 1, :]).astype(BF16)
    r2 = lax.rsqrt(jnp.mean(x * x, axis=-1, keepdims=True) + EPS)
    for lo, hi in FF_CHUNKS:
        g = _dot(h2, wg_ref[:, lo:hi]) * r2
        act = g * _sigmoid(g) * (_dot(h2, wu_ref[:, lo:hi]) * r2)
        act_scr[:, lo:hi] = act.astype(BF16)
    x = x + _dot(act_scr[...], wd_ref[...])
    h3 = (x * vec_ref[R_LN3:R_LN3 + 1, :]).astype(BF16)
    r3 = lax.rsqrt(jnp.mean(x * x, axis=-1, keepdims=True) + EPS)
    gate = _sigmoid(_dot(h3, wpg_ref[...]) * r3)
    return x + gate * _dot(p.astype(BF16), wp_ref[...])


def _ffn_kernel(x_ref, p_ref, xs_ref, ps_ref, vec_ref, wg_ref, wu_ref, wd_ref, wpg_ref, wp_ref,
                *rest, convert):
    if convert:
        win_f32, o_ref, os_ref, win_bf16, act_scr = rest
        _convert_blocks([win_f32], [win_bf16])
    else:
        o_ref, os_ref, act_scr = rest
    weights = (vec_ref, wg_ref, wu_ref, wd_ref, wpg_ref, wp_ref)
    o_ref[...] = _ffn_rows(x_ref[...], p_ref[...], *weights, act_scr)

    @pl.when(pl.program_id(0) == pl.num_programs(0) - 1)
    def _sample():
        ns = xs_ref.shape[0]
        os_ref[...] = _ffn_rows(xs_ref[...], ps_ref[...], *weights, act_scr.at[0:ns, :])


def _ffn(layer, x, p, xs, ps, ffn_w, prm):
    rows, ns = x.shape[0], xs.shape[0]
    tm = min(FFN_TM, rows)
    assert rows % tm == 0 and ns <= tm
    depth = prm["w_in_f32"].shape[0]
    convert = layer + 1 < depth
    conv = [_convert_specs(prm["w_in_f32"], layer + 1, rows // tm)] if convert else []
    wl = lambda *z: lambda r: (layer,) + z
    whole = lambda r: (0, 0)
    outs = pl.pallas_call(
        functools.partial(_ffn_kernel, convert=convert),
        grid=(rows // tm,),
        in_specs=[
            pl.BlockSpec((tm, D_MODEL), lambda r: (r, 0)),
            pl.BlockSpec((None, tm, PLE_DIM), lambda r: (layer, r, 0)),
            _const_spec((ns, D_MODEL), whole),
            _const_spec((None, ns, PLE_DIM), wl(0, 0)),
            _const_spec((None, VEC_ROWS, D_MODEL), wl(0, 0)),
            _const_spec((D_MODEL, D_FF), whole),
            _const_spec((D_MODEL, D_FF), whole),
            _const_spec((D_FF, D_MODEL), whole),
            _const_spec((D_MODEL, D_MODEL), whole),
            _const_spec((PLE_DIM, D_MODEL), whole),
        ] + [c[0] for c in conv],
        out_specs=[
            pl.BlockSpec((tm, D_MODEL), lambda r: (r, 0)),
            pl.BlockSpec((ns, D_MODEL), whole),
        ] + [c[1] for c in conv],
        out_shape=[
            jax.ShapeDtypeStruct((rows, D_MODEL), F32),
            jax.ShapeDtypeStruct((ns, D_MODEL), F32),
        ] + [c[2] for c in conv],
        scratch_shapes=[pltpu.VMEM((tm, D_FF), BF16)],
        compiler_params=pltpu.CompilerParams(
            dimension_semantics=("arbitrary",), vmem_limit_bytes=FFN_VMEM_LIMIT_BYTES),
        name=f"ffn_l{layer}",
    )(x, p, xs, ps, prm["vecs"], *ffn_w, *([prm["w_in_f32"]] if convert else []))
    return outs[0], outs[1], (outs[2] if convert else None)


def _t5_bucket(dist):
    n = np.maximum(dist, 0)
    max_exact = N_BUCKETS // 2
    nf = np.maximum(n, 1).astype(np.float32)
    large = max_exact + (np.log(nf / max_exact) / math.log(MAX_DISTANCE / max_exact)
                         * (N_BUCKETS - max_exact)).astype(np.int32)
    large = np.minimum(large, N_BUCKETS - 1)
    return np.where(n < max_exact, n, large)


def _bucket_table(dist):
    return np.where((dist >= 0) & (dist < WINDOW), _t5_bucket(dist), -1).astype(np.int32)


def _regroup_heads(w, axis):
    shape = w.shape
    w = w.reshape(shape[:axis] + (N_KV_HEADS, GROUP, HEAD_DIM) + shape[axis + 1:])
    return jnp.swapaxes(w, axis, axis + 1).reshape(shape)


def _prepare(t5_table, ln1, w_in, q_gain, k_gain, sinks, w_o_attn, conv_w, conv_b, w_a, b_a,
             w_x, b_x, lam, w_o_lru, w_out, ln2, w_gate, w_up, w_down, ln3, w_ple, w_ple_gate):
    depth = w_in.shape[0]
    w_att = lax.optimization_barrier(w_in[:, :, 0:QKV_WIDTH])
    w_qkv = jnp.concatenate(
        [_regroup_heads(w_att[:, :, 0:Q_WIDTH], 2), w_att[:, :, Q_WIDTH:QKV_WIDTH]], axis=2).astype(BF16)

    def row(v):
        return jnp.pad(v, ((0, 0), (0, D_MODEL - v.shape[1])))[:, None, :]

    parts = [(R_LN1, row(ln1)), (R_CW0, conv_w), (R_CB, row(conv_b)), (R_BA, row(b_a)),
             (R_BX, row(b_x)), (R_LAM, row(lam)), (R_QG, row(jnp.tile(q_gain, (1, N_HEADS)))),
             (R_KG, row(jnp.tile(k_gain, (1, N_KV_HEADS)))), (R_LN2, row(ln2)), (R_LN3, row(ln3))]
    used = 0
    for first_row, part in parts:
        assert first_row == used
        used += part.shape[1]
    vecs = jnp.concatenate([part for _, part in parts]
                           + [jnp.zeros((depth, VEC_ROWS - used, D_MODEL), F32)], axis=1)
    head_id = np.arange(Q_WIDTH) // HEAD_DIM
    seg = jnp.asarray(head_id[:, None] == head_id[None, :], BF16)
    bucket_p = _bucket_table((WINDOW + np.arange(WINDOW))[:, None] - np.arange(2 * WINDOW)[None, :])
    bucket_s = _bucket_table((WINDOW - np.arange(WINDOW))[None, :])
    return {
        "t5": t5_table.reshape(-1),
        "sinks": sinks,
        "vecs": vecs,
        "seg": seg,
        "bucket_p": jnp.asarray(bucket_p),
        "bucket_s": jnp.asarray(bucket_s),
        "w_qkv": w_qkv,
        "w_in_f32": w_in,
        "w_in_first": lax.optimization_barrier(w_in[0]).astype(BF16),
        "w_o_attn": _regroup_heads(w_o_attn, 1).astype(BF16),
        "w_ax": jnp.concatenate([w_a, w_x], axis=-1).astype(BF16),
        "w_o_lru": w_o_lru.astype(BF16),
        "w_out": w_out.astype(BF16),
        "w_gate": w_gate,
        "w_up": w_up,
        "w_down": w_down,
        "w_ple_gate": w_ple_gate,
        "w_ple": w_ple,
    }


def kernel(x_prompt, x_sample, cache_k_win, cache_v_win, state_lru_h, state_conv, p_prompt,
           p_sample, t5_table, ln1, w_in, q_gain, k_gain, sinks, w_o_attn, conv_w, conv_b, w_a,
           b_a, w_x, b_x, lam, w_o_lru, w_out, ln2, w_gate, w_up, w_down, ln3, w_ple, w_ple_gate):
    depth = w_in.shape[0]
    nb, seq, _ = x_prompt.shape
    nseq = x_sample.shape[0]
    assert x_sample.shape[1] == 1 and cache_k_win.shape[2] == WINDOW
    prm = _prepare(t5_table, ln1, w_in, q_gain, k_gain, sinks, w_o_attn, conv_w, conv_b, w_a, b_a,
                   w_x, b_x, lam, w_o_lru, w_out, ln2, w_gate, w_up, w_down, ln3, w_ple, w_ple_gate)
    ckt = jnp.transpose(cache_k_win, (0, 1, 3, 4, 2)).reshape(depth, nseq, KV_WIDTH, WINDOW)
    cvt = jnp.transpose(cache_v_win, (0, 1, 3, 4, 2)).reshape(depth, nseq, KV_WIDTH, WINDOW)
    sc = jnp.transpose(state_conv, (0, 2, 1, 3))
    pp = p_prompt.reshape(depth, nb * seq, PLE_DIM)
    ps = p_sample.reshape(depth, nseq, PLE_DIM)

    yp = x_prompt
    ys = x_sample.reshape(nseq, D_MODEL)
    outs = [[] for _ in range(6)]
    nk_buf = jnp.zeros((depth, nseq, KV_WIDTH, WINDOW), F32)
    nv_buf = jnp.zeros((depth, nseq, KV_WIDTH, WINDOW), F32)
    w_in_l = prm["w_in_first"]
    for layer in range(depth):
        x1, nk, nv, nh, nc, *ffn_w = _mixer_prompt(layer, yp, w_in_l, prm)
        outs[0].append(nk.reshape(nb, WINDOW, N_KV_HEADS, HEAD_DIM))
        outs[1].append(nv.reshape(nb, WINDOW, N_KV_HEADS, HEAD_DIM))
        outs[2].append(nh)
        outs[3].append(nc[:, SUBLANES - (CONV_WIDTH - 1):, :])

        x1s, nhs, xr, nk_buf, nv_buf = _mixer_sample(layer, ys, w_in_l, ckt, cvt, state_lru_h, sc,
                                                     prm, nk_buf, nv_buf)
        yp, ys, w_in_l = _ffn(layer, x1.reshape(nb * seq, D_MODEL), pp, x1s, ps, ffn_w, prm)
        yp = yp.reshape(nb, seq, D_MODEL)
        outs[4].append(nhs)
        outs[5].append(jnp.concatenate([state_conv[layer, :, 1:], xr[:, None, :]], axis=1))

    def untranspose(buf):
        return jnp.transpose(buf.reshape(depth, nseq, N_KV_HEADS, HEAD_DIM, WINDOW), (0, 1, 4, 2, 3))

    stacked = [jnp.stack(o) for o in outs]
    return (yp, ys.reshape(nseq, 1, D_MODEL), stacked[0], stacked[1], stacked[2], stacked[3],
            untranspose(nk_buf), untranspose(nv_buf), stacked[4], stacked[5])
```

```python
import functools
import math

import numpy as np
import jax
import jax.numpy as jnp
from jax import lax
from jax.experimental import pallas as pl
from jax.experimental.pallas import tpu as pltpu

F32 = jnp.float32
BF16 = jnp.bfloat16

D_MODEL = 1024
HEAD_DIM = 64
N_HEADS = 8
N_KV_HEADS = 2
GROUP = N_HEADS // N_KV_HEADS
Q_WIDTH = N_HEADS * HEAD_DIM
KV_WIDTH = N_KV_HEADS * HEAD_DIM
WINDOW = 128
N_BUCKETS = 32
MAX_DISTANCE = 128
LRU_WIDTH = D_MODEL
LRU_HEADS = 8
LRU_BLOCK = LRU_WIDTH // LRU_HEADS
LRU_C = 8.0
CONV_WIDTH = 4
D_FF = 2816
PLE_DIM = 256
EPS = 1e-6
NEG_INF = -1e30
TINY = 1e-30
LOG2E = math.log2(math.e)
GELU_C = math.sqrt(2.0 / math.pi)

SUBLANES = 8
BF16_SUBLANES = 16
LANES = 128
VMEM_LIMIT_BYTES = 56 * 1024 * 1024
FFN_VMEM_LIMIT_BYTES = 58 * 1024 * 1024

Q0 = 0
K0 = Q0 + Q_WIDTH
V0 = K0 + KV_WIDTH
QKV_WIDTH = V0 + KV_WIDTH
XR0 = QKV_WIDTH
XG0 = XR0 + LRU_WIDTH
GA0 = XG0 + LRU_WIDTH
GL0 = GA0 + D_MODEL
IN_WIDTH = GL0 + D_MODEL

R_LN1, R_CW0, R_CB, R_BA, R_BX, R_LAM, R_QG, R_KG, R_LN2, R_LN3 = 0, 1, 5, 6, 7, 8, 9, 10, 11, 12
VEC_ROWS = 16

PROMPT_TT = 64
FFN_TM = 1024
SAMPLE_BB = 16
PROJ_CHUNK = 512
N_FFN_WEIGHTS = 5


def _dot(a, b):
    return jnp.dot(a, b, preferred_element_type=F32)


def _dot_nt(a, b):
    return lax.dot_general(a, b, (((1,), (1,)), ((), ())), preferred_element_type=F32)


def _rms(x, g):
    ms = jnp.mean(x * x, axis=-1, keepdims=True)
    return x * lax.rsqrt(ms + EPS) * g


def _seg_rms(x, seg, g):
    x2 = x * x
    ssq = _dot(x2.astype(BF16), seg)
    return x * lax.rsqrt(ssq * (1.0 / HEAD_DIM) + EPS) * g


def _sigmoid(x):
    return 0.5 * jnp.tanh(0.5 * x) + 0.5


def _gelu_times(x, y):
    t = jnp.tanh(x * (GELU_C + (GELU_C * 0.044715) * (x * x)))
    hxy = (0.5 * x) * y
    return hxy + hxy * t


def _softplus(z):
    return jnp.maximum(z, 0.0) + jnp.log1p(jnp.exp(-jnp.abs(z)))


def _gather_bias(bucket, t5_ref, head):
    acc = jnp.full(bucket.shape, NEG_INF, F32)
    for bkt in range(N_BUCKETS):
        acc = jnp.where(bucket == bkt, t5_ref[bkt * N_HEADS + head], acc)
    return acc


def _lru_gates(xch, wax, b_a, b_x, c_row):
    gts = _dot(xch.astype(BF16), wax)
    r = _sigmoid(gts[:, :LRU_BLOCK] + b_a)
    ig = _sigmoid(gts[:, LRU_BLOCK:] + b_x)
    log_a = c_row * r
    a = jnp.exp(log_a)
    y = 1.0 - a * a
    return a, y * lax.rsqrt(jnp.maximum(y, TINY)) * (ig * xch)


def _rows_per_step(rows, nsteps):
    need = -(-rows // nsteps)
    for per in range(BF16_SUBLANES, rows + 1, BF16_SUBLANES):
        if rows % per == 0 and per >= need:
            return per
    raise ValueError((rows, nsteps))


def _convert_specs(stacked, layer, nsteps):
    _, rows, cols = stacked.shape
    per = _rows_per_step(rows, nsteps)
    last = rows // per - 1
    return (pl.BlockSpec((None, per, cols), lambda t: (layer, jnp.minimum(t, last), 0)),
            pl.BlockSpec((per, cols), lambda t: (jnp.minimum(t, last), 0)),
            jax.ShapeDtypeStruct((rows, cols), BF16))


def _convert_blocks(src_refs, dst_refs):
    for src, dst in zip(src_refs, dst_refs):
        dst[...] = src[...].astype(BF16)


def _mixer_prompt_kernel(t5_ref, sinks_ref, x_ref, vec_ref, bucket_ref, seg_ref, wqkv_ref, win_ref,
                         woa_ref, wax_ref, wol_ref, wout_ref, *rest, nb, tt):
    ffn_f32 = rest[0:N_FFN_WEIGHTS]
    x1_ref, nk_ref, nv_ref, nh_ref, nc_ref = rest[N_FFN_WEIGHTS:N_FFN_WEIGHTS + 5]
    ffn_bf16 = rest[N_FFN_WEIGHTS + 5:2 * N_FFN_WEIGHTS + 5]
    (bias_scr, bias_t, kband, vband, cs, a_scr, b_scr, hst, y_scr, o_scr,
     proj_scr, q_scr) = rest[2 * N_FFN_WEIGHTS + 5:]
    tm = nb * tt
    t = pl.program_id(0)
    _convert_blocks(ffn_f32, ffn_bf16)

    @pl.when(t == 0)
    def _init():
        bucket = bucket_ref[...]
        for hh in range(N_HEADS):
            bias_scr[hh] = _gather_bias(bucket, t5_ref, hh) * LOG2E
        kband[...] = jnp.zeros((nb, 2 * WINDOW, KV_WIDTH), F32)
        vband[...] = jnp.zeros((nb, 2 * WINDOW, KV_WIDTH), F32)
        cs[:, 0:SUBLANES, :] = jnp.zeros((nb, SUBLANES, LRU_WIDTH), F32)
        hst[...] = jnp.zeros((nb, LRU_WIDTH), F32)

    x = x_ref[...].reshape(tm, D_MODEL)
    h = (x * vec_ref[R_LN1:R_LN1 + 1, :]).astype(BF16)
    rstd = lax.rsqrt(jnp.mean(x * x, axis=-1, keepdims=True) + EPS)

    cs[:, SUBLANES:SUBLANES + tt, :] = (_dot(h, win_ref[:, XR0:XG0]) * rstd).reshape(nb, tt, LRU_WIDTH)
    c_row = -LRU_C * _softplus(-vec_ref[R_LAM:R_LAM + 1, :])
    n_chunks = (IN_WIDTH - XG0) // PROJ_CHUNK
    qkv = None
    for hd in range(LRU_HEADS):
        lc = slice(hd * LRU_BLOCK, (hd + 1) * LRU_BLOCK)
        xc = vec_ref[R_CB:R_CB + 1, lc]
        for j in range(CONV_WIDTH):
            off = SUBLANES - (CONV_WIDTH - 1) + j
            xc = xc + vec_ref[R_CW0 + j:R_CW0 + j + 1, lc] * cs[:, off:off + tt, lc]
        a, bb = _lru_gates(xc.reshape(tm, LRU_BLOCK), wax_ref[hd], vec_ref[R_BA:R_BA + 1, lc],
                           vec_ref[R_BX:R_BX + 1, lc], c_row[:, lc])
        for b in range(nb):
            rows = pl.ds(b, tt, stride=nb)
            a_scr[hd, rows, :] = a[b * tt:(b + 1) * tt]
            b_scr[hd, rows, :] = bb[b * tt:(b + 1) * tt]
        if hd == 0:
            qkv = _dot(h, wqkv_ref[...]) * rstd
        elif hd == 1:
            seg = seg_ref[...]
            q_scr[...] = _seg_rms(qkv[:, Q0:K0], seg,
                                  vec_ref[R_QG:R_QG + 1, 0:Q_WIDTH] * (LOG2E * HEAD_DIM ** -0.5))
            kn = _seg_rms(qkv[:, K0:V0], seg[0:KV_WIDTH, 0:KV_WIDTH],
                          vec_ref[R_KG:R_KG + 1, 0:KV_WIDTH])
            kband[:, WINDOW:WINDOW + tt, :] = kn.reshape(nb, tt, KV_WIDTH)
            vband[:, WINDOW:WINDOW + tt, :] = qkv[:, V0:QKV_WIDTH].reshape(nb, tt, KV_WIDTH)
            nk_ref[...] = kband[:, tt:tt + WINDOW, :]
            nv_ref[...] = vband[:, tt:tt + WINDOW, :]
        if hd >= LRU_HEADS - n_chunks:
            c0 = (hd - (LRU_HEADS - n_chunks)) * PROJ_CHUNK
            proj_scr[:, c0:c0 + PROJ_CHUNK] = _dot(h, win_ref[:, XG0 + c0:XG0 + c0 + PROJ_CHUNK]) * rstd
    tail = cs[:, tt:tt + SUBLANES, :]
    nc_ref[...] = tail
    cs[:, 0:SUBLANES, :] = tail

    col = lax.broadcasted_iota(jnp.int32, (1, 2 * WINDOW), 1)
    colmask = jnp.where(col >= WINDOW - t * tt, 0.0, NEG_INF)
    for hh in range(N_HEADS):
        bias_t[hh] = bias_scr[hh, 0:tt, :] + colmask

    lane = lax.broadcasted_iota(jnp.int32, (1, LANES), 1)
    low = lane < HEAD_DIM
    for b in range(nb):
        kb = kband[b].astype(BF16)
        vb = vband[b].astype(BF16)
        q8 = jnp.concatenate(
            [jnp.where(low if kh == 0 else jnp.logical_not(low),
                       q_scr[b * tt:(b + 1) * tt, g * LANES:(g + 1) * LANES], 0.0)
             for kh in range(N_KV_HEADS) for g in range(GROUP)], axis=0).astype(BF16)
        s_all = _dot_nt(q8, kb)
        ps, invs = [], []
        for hh in range(N_HEADS):
            sg = s_all[hh * tt:(hh + 1) * tt] + bias_t[hh]
            sink = sinks_ref[hh] * LOG2E
            m = jnp.maximum(jnp.max(sg, axis=-1, keepdims=True), sink)
            e = jnp.exp2(sg - m)
            invs.append(1.0 / (jnp.sum(e, axis=-1, keepdims=True) + jnp.exp2(sink - m)))
            ps.append(e.astype(BF16))
        o8 = _dot(jnp.concatenate(ps, axis=0), vb)
        for g in range(GROUP):
            lo_rows = o8[g * tt:(g + 1) * tt] * invs[g]
            hi_rows = o8[(GROUP + g) * tt:(GROUP + g + 1) * tt] * invs[GROUP + g]
            o_scr[b * tt:(b + 1) * tt, g * LANES:(g + 1) * LANES] = jnp.where(low, lo_rows, hi_rows).astype(BF16)

    kband[:, 0:WINDOW, :] = kband[:, tt:tt + WINDOW, :]
    vband[:, 0:WINDOW, :] = vband[:, tt:tt + WINDOW, :]

    m_att = _sigmoid(proj_scr[:, GA0 - XG0:GL0 - XG0]) * _dot(o_scr[...], woa_ref[...])

    hs = [hst[:, hd * LRU_BLOCK:(hd + 1) * LRU_BLOCK] for hd in range(LRU_HEADS)]
    for ts in range(tt):
        rows = slice(ts * nb, (ts + 1) * nb)
        for hd in range(LRU_HEADS):
            hs[hd] = a_scr[hd, rows, :] * hs[hd] + b_scr[hd, rows, :]
            b_scr[hd, rows, :] = hs[hd]
    hfin = jnp.concatenate(hs, axis=1)
    hst[...] = hfin
    nh_ref[...] = hfin

    for hd in range(LRU_HEADS):
        lc = slice(hd * LRU_BLOCK, (hd + 1) * LRU_BLOCK)
        hseq = jnp.concatenate([b_scr[hd, pl.ds(b, tt, stride=nb), :] for b in range(nb)], axis=0)
        y_scr[:, lc] = _gelu_times(proj_scr[:, lc], hseq).astype(BF16)

    m_all = m_att + _sigmoid(proj_scr[:, GL0 - XG0:IN_WIDTH - XG0]) * _dot(y_scr[...], wol_ref[...])
    x1 = x + _dot(m_all.astype(BF16), wout_ref[...])
    x1_ref[...] = x1.reshape(nb, tt, D_MODEL)


def _const_spec(shape, index):
    return pl.BlockSpec(shape, index, pipeline_mode=pl.Buffered(1))


def _smem_spec():
    return pl.BlockSpec(memory_space=pltpu.SMEM)


def _mixer_prompt(layer, x, w_in, prm):
    nb, seq, _ = x.shape
    tt = PROMPT_TT
    assert nb == SUBLANES and seq % tt == 0 and WINDOW % tt == 0
    tm = nb * tt
    wl = lambda *z: lambda t: (layer,) + z
    kern = functools.partial(_mixer_prompt_kernel, nb=nb, tt=tt)
    conv = [_convert_specs(prm[name], layer, seq // tt) for name in FFN_WEIGHT_NAMES]
    return pl.pallas_call(
        kern,
        grid=(seq // tt,),
        in_specs=[
            _smem_spec(), _smem_spec(),
            pl.BlockSpec((nb, tt, D_MODEL), lambda t: (0, t, 0)),
            _const_spec((None, VEC_ROWS, D_MODEL), wl(0, 0)),
            _const_spec((WINDOW, 2 * WINDOW), lambda t: (0, 0)),
            _const_spec((Q_WIDTH, Q_WIDTH), lambda t: (0, 0)),
            _const_spec((None, D_MODEL, QKV_WIDTH), wl(0, 0)),
            _const_spec((D_MODEL, IN_WIDTH), lambda t: (0, 0)),
            _const_spec((None, Q_WIDTH, D_MODEL), wl(0, 0)),
            _const_spec((None, LRU_HEADS, LRU_BLOCK, 2 * LRU_BLOCK), wl(0, 0, 0)),
            _const_spec((None, LRU_WIDTH, D_MODEL), wl(0, 0)),
            _const_spec((None, D_MODEL, D_MODEL), wl(0, 0)),
        ] + [c[0] for c in conv],
        out_specs=[
            pl.BlockSpec((nb, tt, D_MODEL), lambda t: (0, t, 0)),
            pl.BlockSpec((nb, WINDOW, KV_WIDTH), lambda t: (0, 0, 0)),
            pl.BlockSpec((nb, WINDOW, KV_WIDTH), lambda t: (0, 0, 0)),
            pl.BlockSpec((nb, LRU_WIDTH), lambda t: (0, 0)),
            pl.BlockSpec((nb, SUBLANES, LRU_WIDTH), lambda t: (0, 0, 0)),
        ] + [c[1] for c in conv],
        out_shape=[
            jax.ShapeDtypeStruct((nb, seq, D_MODEL), F32),
            jax.ShapeDtypeStruct((nb, WINDOW, KV_WIDTH), F32),
            jax.ShapeDtypeStruct((nb, WINDOW, KV_WIDTH), F32),
            jax.ShapeDtypeStruct((nb, LRU_WIDTH), F32),
            jax.ShapeDtypeStruct((nb, SUBLANES, LRU_WIDTH), F32),
        ] + [c[2] for c in conv],
        scratch_shapes=[
            pltpu.VMEM((N_HEADS, WINDOW, 2 * WINDOW), F32),
            pltpu.VMEM((N_HEADS, tt, 2 * WINDOW), F32),
            pltpu.VMEM((nb, 2 * WINDOW, KV_WIDTH), F32),
            pltpu.VMEM((nb, 2 * WINDOW, KV_WIDTH), F32),
            pltpu.VMEM((nb, tt + SUBLANES, LRU_WIDTH), F32),
            pltpu.VMEM((LRU_HEADS, tm, LRU_BLOCK), F32),
            pltpu.VMEM((LRU_HEADS, tm, LRU_BLOCK), F32),
            pltpu.VMEM((nb, LRU_WIDTH), F32),
            pltpu.VMEM((tm, LRU_WIDTH), BF16),
            pltpu.VMEM((tm, Q_WIDTH), BF16),
            pltpu.VMEM((tm, IN_WIDTH - XG0), F32),
            pltpu.VMEM((tm, Q_WIDTH), F32),
        ],
        compiler_params=pltpu.CompilerParams(
            dimension_semantics=("arbitrary",), vmem_limit_bytes=VMEM_LIMIT_BYTES),
        name=f"mixer_prompt_l{layer}",
    )(prm["t5"], prm["sinks"][layer], x, prm["vecs"], prm["bucket_p"], prm["seg"], prm["w_qkv"],
      w_in, prm["w_o_attn"], prm["w_ax"], prm["w_o_lru"], prm["w_out"],
      *[prm[name] for name in FFN_WEIGHT_NAMES])


def _mixer_sample_kernel(t5_ref, sinks_ref, x_ref, vec_ref, bucket_ref, seg_ref, wqkv_ref, win_ref,
                         woa_ref, wax_ref, wol_ref, wout_ref, ck_ref, cv_ref, h0_ref, sc_ref,
                         nk_all_ref, nv_all_ref,
                         x1_ref, nh_ref, xr_ref, nk_ref, nv_ref,
                         h_scr, q_scr, o_scr, mb_scr, kn_scr, vn_scr,
                         *, bb):
    del nk_all_ref, nv_all_ref
    i = pl.program_id(0)
    nkeys = bb * WINDOW

    @pl.when(i == 0)
    def _project():
        x = x_ref[...]
        h = _rms(x, vec_ref[R_LN1:R_LN1 + 1, :]).astype(BF16)
        h_scr[...] = h
        qkv = _dot(h, wqkv_ref[...])
        seg = seg_ref[...]
        q_scr[...] = _seg_rms(qkv[:, Q0:K0], seg,
                              vec_ref[R_QG:R_QG + 1, 0:Q_WIDTH] * (HEAD_DIM ** -0.5))
        kn_scr[...] = _seg_rms(qkv[:, K0:V0], seg[0:KV_WIDTH, 0:KV_WIDTH],
                               vec_ref[R_KG:R_KG + 1, 0:KV_WIDTH])
        vn_scr[...] = qkv[:, V0:QKV_WIDTH]
        bucket = bucket_ref[...]
        rowb = lax.broadcasted_iota(jnp.int32, (bb, nkeys), 0)
        colb = lax.broadcasted_iota(jnp.int32, (bb, nkeys), 1) // WINDOW
        for hh in range(N_HEADS):
            brow = _gather_bias(bucket, t5_ref, hh)
            brow = jnp.concatenate([brow] * bb, axis=1)
            mb_scr[hh * bb:(hh + 1) * bb, :] = jnp.where(rowb == colb, brow, NEG_INF)

    rows = pl.ds(pl.multiple_of(i * bb, bb), bb)
    lane = lax.broadcasted_iota(jnp.int32, (1, LANES), 1)
    low = lane < HEAD_DIM
    qblk = q_scr[rows, :]
    qz = []
    for kh in range(N_KV_HEADS):
        keep = low if kh == 0 else jnp.logical_not(low)
        for g in range(GROUP):
            qz.append(jnp.where(keep, qblk[:, g * LANES:(g + 1) * LANES], 0.0))
    qz = jnp.concatenate(qz, axis=0).astype(BF16)
    kt = jnp.concatenate([ck_ref[b] for b in range(bb)], axis=1).astype(BF16)
    vt = jnp.concatenate([cv_ref[b] for b in range(bb)], axis=1).astype(BF16)
    s = _dot(qz, kt) + mb_scr[...]
    kn_blk = kn_scr[rows, :]
    vn_blk = vn_scr[rows, :]
    knew = jnp.concatenate([kn_blk.astype(BF16).astype(F32)] * N_HEADS, axis=0)
    vnew = jnp.concatenate([vn_blk.astype(BF16).astype(F32)] * N_HEADS, axis=0)
    self_bias = jnp.concatenate(
        [jnp.full((bb, 1), t5_ref[hh], F32) for hh in range(N_HEADS)], axis=0)
    sink = jnp.concatenate(
        [jnp.full((bb, 1), sinks_ref[hh], F32) for hh in range(N_HEADS)], axis=0)
    s_self = jnp.sum(qz.astype(F32) * knew, axis=-1, keepdims=True) + self_bias
    m = jnp.maximum(jnp.maximum(jnp.max(s, axis=-1, keepdims=True), s_self), sink)
    e = jnp.exp(s - m)
    e_self = jnp.exp(s_self - m)
    den = jnp.sum(e, axis=-1, keepdims=True) + e_self + jnp.exp(sink - m)
    inv = 1.0 / den
    o = (_dot_nt((e * inv).astype(BF16), vt)
         + (e_self * inv).astype(BF16).astype(F32) * vnew)
    half = GROUP * bb
    for g in range(GROUP):
        og = jnp.where(low, o[g * bb:(g + 1) * bb], o[half + g * bb:half + (g + 1) * bb])
        o_scr[rows, g * LANES:(g + 1) * LANES] = og

    knt = kn_blk.T
    vnt = vn_blk.T
    last = lax.broadcasted_iota(jnp.int32, (1, WINDOW), 1) == WINDOW - 1
    for b in range(bb):
        nk_ref[b] = jnp.where(last, knt[:, b:b + 1], pltpu.roll(ck_ref[b], WINDOW - 1, 1))
        nv_ref[b] = jnp.where(last, vnt[:, b:b + 1], pltpu.roll(cv_ref[b], WINDOW - 1, 1))

    @pl.when(i == pl.num_programs(0) - 1)
    def _finish():
        x = x_ref[...]
        h = h_scr[...]
        m_att = _sigmoid(_dot(h, win_ref[:, GA0:GL0])) * _dot(o_scr[...].astype(BF16), woa_ref[...])
        xr = _dot(h, win_ref[:, XR0:XG0])
        xr_ref[...] = xr
        xc = vec_ref[R_CB:R_CB + 1, :] + vec_ref[R_CW0 + CONV_WIDTH - 1:R_CW0 + CONV_WIDTH, :] * xr
        for j in range(CONV_WIDTH - 1):
            xc = xc + vec_ref[R_CW0 + j:R_CW0 + j + 1, :] * sc_ref[j]
        c_row = -LRU_C * _softplus(-vec_ref[R_LAM:R_LAM + 1, :])
        hn = []
        for hd in range(LRU_HEADS):
            lc = slice(hd * LRU_BLOCK, (hd + 1) * LRU_BLOCK)
            a, bb_ = _lru_gates(xc[:, lc], wax_ref[hd], vec_ref[R_BA:R_BA + 1, lc],
                                vec_ref[R_BX:R_BX + 1, lc], c_row[:, lc])
            hn.append(a * h0_ref[:, lc] + bb_)
        hn = jnp.concatenate(hn, axis=1)
        nh_ref[...] = hn
        y = _gelu_times(_dot(h, win_ref[:, XG0:GA0]), hn).astype(BF16)
        m_all = m_att + _sigmoid(_dot(h, win_ref[:, GL0:IN_WIDTH])) * _dot(y, wol_ref[...])
        x1_ref[...] = x + _dot(m_all.astype(BF16), wout_ref[...])


def _mixer_sample(layer, x, w_in, ckt, cvt, h0, sc, prm, nk_buf, nv_buf):
    depth, nseq = ckt.shape[0], ckt.shape[1]
    bb = SAMPLE_BB
    assert nseq % bb == 0
    wl = lambda *z: lambda i: (layer,) + z
    full2 = lambda i: (0, 0)
    cache_spec = pl.BlockSpec((None, bb, KV_WIDTH, WINDOW), lambda i: (layer, i, 0, 0))
    n_in = 16
    kern = functools.partial(_mixer_sample_kernel, bb=bb)
    return pl.pallas_call(
        kern,
        grid=(nseq // bb,),
        in_specs=[
            _smem_spec(), _smem_spec(),
            _const_spec((nseq, D_MODEL), full2),
            _const_spec((None, VEC_ROWS, D_MODEL), wl(0, 0)),
            _const_spec((1, WINDOW), full2),
            _const_spec((Q_WIDTH, Q_WIDTH), full2),
            _const_spec((None, D_MODEL, QKV_WIDTH), wl(0, 0)),
            _const_spec((D_MODEL, IN_WIDTH), full2),
            _const_spec((None, Q_WIDTH, D_MODEL), wl(0, 0)),
            _const_spec((None, LRU_HEADS, LRU_BLOCK, 2 * LRU_BLOCK), wl(0, 0, 0)),
            _const_spec((None, LRU_WIDTH, D_MODEL), wl(0, 0)),
            _const_spec((None, D_MODEL, D_MODEL), wl(0, 0)),
            cache_spec, cache_spec,
            _const_spec((None, nseq, LRU_WIDTH), wl(0, 0)),
            _const_spec((None, CONV_WIDTH - 1, nseq, LRU_WIDTH), wl(0, 0, 0)),
            pl.BlockSpec(memory_space=pl.ANY), pl.BlockSpec(memory_space=pl.ANY),
        ],
        out_specs=[
            pl.BlockSpec((nseq, D_MODEL), full2),
            pl.BlockSpec((nseq, LRU_WIDTH), full2),
            pl.BlockSpec((nseq, LRU_WIDTH), full2),
            cache_spec, cache_spec,
        ],
        out_shape=[
            jax.ShapeDtypeStruct((nseq, D_MODEL), F32),
            jax.ShapeDtypeStruct((nseq, LRU_WIDTH), F32),
            jax.ShapeDtypeStruct((nseq, LRU_WIDTH), F32),
            jax.ShapeDtypeStruct((depth, nseq, KV_WIDTH, WINDOW), F32),
            jax.ShapeDtypeStruct((depth, nseq, KV_WIDTH, WINDOW), F32),
        ],
        input_output_aliases={n_in: 3, n_in + 1: 4},
        scratch_shapes=[
            pltpu.VMEM((nseq, D_MODEL), BF16),
            pltpu.VMEM((nseq, Q_WIDTH), F32),
            pltpu.VMEM((nseq, Q_WIDTH), F32),
            pltpu.VMEM((N_HEADS * bb, bb * WINDOW), F32),
            pltpu.VMEM((nseq, KV_WIDTH), F32),
            pltpu.VMEM((nseq, KV_WIDTH), F32),
        ],
        compiler_params=pltpu.CompilerParams(
            dimension_semantics=("arbitrary",), vmem_limit_bytes=VMEM_LIMIT_BYTES),
        name=f"mixer_sample_l{layer}",
    )(prm["t5"], prm["sinks"][layer], x, prm["vecs"], prm["bucket_s"], prm["seg"], prm["w_qkv"],
      w_in, prm["w_o_attn"], prm["w_ax"], prm["w_o_lru"], prm["w_out"], ckt, cvt, h0, sc,
      nk_buf, nv_buf)


FF_CHUNKS = ((0, 1024), (1024, 2048), (2048, D_FF))
FFN_WEIGHT_NAMES = ("w_gate", "w_up", "w_down", "w_ple_gate", "w_ple")


def _ffn_rows(x, p, vec_ref, wg_ref, wu_ref, wd_ref, wpg_ref, wp_ref, act_scr):
    h2 = (x * vec_ref[R_LN2:R_LN2 + 1, :]).astype(BF16)
    r2 = lax.rsqrt(jnp.mean(x * x, axis=-1, keepdims=True) + EPS)
    for lo, hi in FF_CHUNKS:
        g = _dot(h2, wg_ref[:, lo:hi]) * r2
        act = g * _sigmoid(g) * (_dot(h2, wu_ref[:, lo:hi]) * r2)
        act_scr[:, lo:hi] = act.astype(BF16)
    x = x + _dot(act_scr[...], wd_ref[...])
    h3 = (x * vec_ref[R_LN3:R_LN3 + 1, :]).astype(BF16)
    r3 = lax.rsqrt(jnp.mean(x * x, axis=-1, keepdims=True) + EPS)
    gate = _sigmoid(_dot(h3, wpg_ref[...]) * r3)
    return x + gate * _dot(p.astype(BF16), wp_ref[...])


def _ffn_kernel(x_ref, p_ref, xs_ref, ps_ref, vec_ref, wg_ref, wu_ref, wd_ref, wpg_ref, wp_ref,
                *rest, convert):
    if convert:
        win_f32, o_ref, os_ref, win_bf16, act_scr = rest
        _convert_blocks([win_f32], [win_bf16])
    else:
        o_ref, os_ref, act_scr = rest
    weights = (vec_ref, wg_ref, wu_ref, wd_ref, wpg_ref, wp_ref)
    o_ref[...] = _ffn_rows(x_ref[...], p_ref[...], *weights, act_scr)

    @pl.when(pl.program_id(0) == pl.num_programs(0) - 1)
    def _sample():
        ns = xs_ref.shape[0]
        os_ref[...] = _ffn_rows(xs_ref[...], ps_ref[...], *weights, act_scr.at[0:ns, :])


def _ffn(layer, x, p, xs, ps, ffn_w, prm):
    rows, ns = x.shape[0], xs.shape[0]
    tm = min(FFN_TM, rows)
    assert rows % tm == 0 and ns <= tm
    depth = prm["w_in_f32"].shape[0]
    convert = layer + 1 < depth
    conv = [_convert_specs(prm["w_in_f32"], layer + 1, rows // tm)] if convert else []
    wl = lambda *z: lambda r: (layer,) + z
    whole = lambda r: (0, 0)
    outs = pl.pallas_call(
        functools.partial(_ffn_kernel, convert=convert),
        grid=(rows // tm,),
        in_specs=[
            pl.BlockSpec((tm, D_MODEL), lambda r: (r, 0)),
            pl.BlockSpec((None, tm, PLE_DIM), lambda r: (layer, r, 0)),
            _const_spec((ns, D_MODEL), whole),
            _const_spec((None, ns, PLE_DIM), wl(0, 0)),
            _const_spec((None, VEC_ROWS, D_MODEL), wl(0, 0)),
            _const_spec((D_MODEL, D_FF), whole),
            _const_spec((D_MODEL, D_FF), whole),
            _const_spec((D_FF, D_MODEL), whole),
            _const_spec((D_MODEL, D_MODEL), whole),
            _const_spec((PLE_DIM, D_MODEL), whole),
        ] + [c[0] for c in conv],
        out_specs=[
            pl.BlockSpec((tm, D_MODEL), lambda r: (r, 0)),
            pl.BlockSpec((ns, D_MODEL), whole),
        ] + [c[1] for c in conv],
        out_shape=[
            jax.ShapeDtypeStruct((rows, D_MODEL), F32),
            jax.ShapeDtypeStruct((ns, D_MODEL), F32),
        ] + [c[2] for c in conv],
        scratch_shapes=[pltpu.VMEM((tm, D_FF), BF16)],
        compiler_params=pltpu.CompilerParams(
            dimension_semantics=("arbitrary",), vmem_limit_bytes=FFN_VMEM_LIMIT_BYTES),
        name=f"ffn_l{layer}",
    )(x, p, xs, ps, prm["vecs"], *ffn_w, *([prm["w_in_f32"]] if convert else []))
    return outs[0], outs[1], (outs[2] if convert else None)


def _t5_bucket(dist):
    n = np.maximum(dist, 0)
    max_exact = N_BUCKETS // 2
    nf = np.maximum(n, 1).astype(np.float32)
    large = max_exact + (np.log(nf / max_exact) / math.log(MAX_DISTANCE / max_exact)
                         * (N_BUCKETS - max_exact)).astype(np.int32)
    large = np.minimum(large, N_BUCKETS - 1)
    return np.where(n < max_exact, n, large)


def _bucket_table(dist):
    return np.where((dist >= 0) & (dist < WINDOW), _t5_bucket(dist), -1).astype(np.int32)


def _regroup_heads(w, axis):
    shape = w.shape
    w = w.reshape(shape[:axis] + (N_KV_HEADS, GROUP, HEAD_DIM) + shape[axis + 1:])
    return jnp.swapaxes(w, axis, axis + 1).reshape(shape)


def _prepare(t5_table, ln1, w_in, q_gain, k_gain, sinks, w_o_attn, conv_w, conv_b, w_a, b_a,
             w_x, b_x, lam, w_o_lru, w_out, ln2, w_gate, w_up, w_down, ln3, w_ple, w_ple_gate):
    depth = w_in.shape[0]
    w_att = lax.optimization_barrier(w_in[:, :, 0:QKV_WIDTH])
    w_qkv = jnp.concatenate(
        [_regroup_heads(w_att[:, :, 0:Q_WIDTH], 2), w_att[:, :, Q_WIDTH:QKV_WIDTH]], axis=2).astype(BF16)

    def row(v):
        return jnp.pad(v, ((0, 0), (0, D_MODEL - v.shape[1])))[:, None, :]

    parts = [(R_LN1, row(ln1)), (R_CW0, conv_w), (R_CB, row(conv_b)), (R_BA, row(b_a)),
             (R_BX, row(b_x)), (R_LAM, row(lam)), (R_QG, row(jnp.tile(q_gain, (1, N_HEADS)))),
             (R_KG, row(jnp.tile(k_gain, (1, N_KV_HEADS)))), (R_LN2, row(ln2)), (R_LN3, row(ln3))]
    used = 0
    for first_row, part in parts:
        assert first_row == used
        used += part.shape[1]
    vecs = jnp.concatenate([part for _, part in parts]
                           + [jnp.zeros((depth, VEC_ROWS - used, D_MODEL), F32)], axis=1)
    head_id = np.arange(Q_WIDTH) // HEAD_DIM
    seg = jnp.asarray(head_id[:, None] == head_id[None, :], BF16)
    bucket_p = _bucket_table((WINDOW + np.arange(WINDOW))[:, None] - np.arange(2 * WINDOW)[None, :])
    bucket_s = _bucket_table((WINDOW - np.arange(WINDOW))[None, :])
    return {
        "t5": t5_table.reshape(-1),
        "sinks": sinks,
        "vecs": vecs,
        "seg": seg,
        "bucket_p": jnp.asarray(bucket_p),
        "bucket_s": jnp.asarray(bucket_s),
        "w_qkv": w_qkv,
        "w_in_f32": w_in,
        "w_in_first": lax.optimization_barrier(w_in[0]).astype(BF16),
        "w_o_attn": _regroup_heads(w_o_attn, 1).astype(BF16),
        "w_ax": jnp.concatenate([w_a, w_x], axis=-1).astype(BF16),
        "w_o_lru": w_o_lru.astype(BF16),
        "w_out": w_out.astype(BF16),
        "w_gate": w_gate,
        "w_up": w_up,
        "w_down": w_down,
        "w_ple_gate": w_ple_gate,
        "w_ple": w_ple,
    }


def kernel(x_prompt, x_sample, cache_k_win, cache_v_win, state_lru_h, state_conv, p_prompt,
           p_sample, t5_table, ln1, w_in, q_gain, k_gain, sinks, w_o_attn, conv_w, conv_b, w_a,
           b_a, w_x, b_x, lam, w_o_lru, w_out, ln2, w_gate, w_up, w_down, ln3, w_ple, w_ple_gate):
    depth = w_in.shape[0]
    nb, seq, _ = x_prompt.shape
    nseq = x_sample.shape[0]
    assert x_sample.shape[1] == 1 and cache_k_win.shape[2] == WINDOW
    prm = _prepare(t5_table, ln1, w_in, q_gain, k_gain, sinks, w_o_attn, conv_w, conv_b, w_a, b_a,
                   w_x, b_x, lam, w_o_lru, w_out, ln2, w_gate, w_up, w_down, ln3, w_ple, w_ple_gate)
    ckt = jnp.transpose(cache_k_win, (0, 1, 3, 4, 2)).reshape(depth, nseq, KV_WIDTH, WINDOW)
    cvt = jnp.transpose(cache_v_win, (0, 1, 3, 4, 2)).reshape(depth, nseq, KV_WIDTH, WINDOW)
    sc = jnp.transpose(state_conv, (0, 2, 1, 3))
    pp = p_prompt.reshape(depth, nb * seq, PLE_DIM)
    ps = p_sample.reshape(depth, nseq, PLE_DIM)

    yp = x_prompt
    ys = x_sample.reshape(nseq, D_MODEL)
    outs = [[] for _ in range(6)]
    nk_buf = jnp.zeros((depth, nseq, KV_WIDTH, WINDOW), F32)
    nv_buf = jnp.zeros((depth, nseq, KV_WIDTH, WINDOW), F32)
    w_in_l = prm["w_in_first"]
    for layer in range(depth):
        x1, nk, nv, nh, nc, *ffn_w = _mixer_prompt(layer, yp, w_in_l, prm)
        outs[0].append(nk.reshape(nb, WINDOW, N_KV_HEADS, HEAD_DIM))
        outs[1].append(nv.reshape(nb, WINDOW, N_KV_HEADS, HEAD_DIM))
        outs[2].append(nh)
        outs[3].append(nc[:, SUBLANES - (CONV_WIDTH - 1):, :])

        x1s, nhs, xr, nk_buf, nv_buf = _mixer_sample(layer, ys, w_in_l, ckt, cvt, state_lru_h, sc,
                                                     prm, nk_buf, nv_buf)
        yp, ys, w_in_l = _ffn(layer, x1.reshape(nb * seq, D_MODEL), pp, x1s, ps, ffn_w, prm)
        yp = yp.reshape(nb, seq, D_MODEL)
        outs[4].append(nhs)
        outs[5].append(jnp.concatenate([state_conv[layer, :, 1:], xr[:, None, :]], axis=1))

    def untranspose(buf):
        return jnp.transpose(buf.reshape(depth, nseq, N_KV_HEADS, HEAD_DIM, WINDOW), (0, 1, 4, 2, 3))

    stacked = [jnp.stack(o) for o in outs]
    return (yp, ys.reshape(nseq, 1, D_MODEL), stacked[0], stacked[1], stacked[2], stacked[3],
            untranspose(nk_buf), untranspose(nv_buf), stacked[4], stacked[5])
```

```python
import functools
import math

import numpy as np
import jax
import jax.numpy as jnp
from jax import lax
from jax.experimental import pallas as pl
from jax.experimental.pallas import tpu as pltpu

F32 = jnp.float32
BF16 = jnp.bfloat16

D_MODEL = 1024
HEAD_DIM = 64
N_HEADS = 8
N_KV_HEADS = 2
GROUP = N_HEADS // N_KV_HEADS
Q_WIDTH = N_HEADS * HEAD_DIM
KV_WIDTH = N_KV_HEADS * HEAD_DIM
WINDOW = 128
N_BUCKETS = 32
MAX_DISTANCE = 128
LRU_WIDTH = D_MODEL
LRU_HEADS = 8
LRU_BLOCK = LRU_WIDTH // LRU_HEADS
LRU_C = 8.0
CONV_WIDTH = 4
D_FF = 2816
PLE_DIM = 256
EPS = 1e-6
NEG_INF = -1e30
TINY = 1e-30
LOG2E = math.log2(math.e)
GELU_C = math.sqrt(2.0 / math.pi)

SUBLANES = 8
BF16_SUBLANES = 16
LANES = 128
VMEM_LIMIT_BYTES = 56 * 1024 * 1024
FFN_VMEM_LIMIT_BYTES = 58 * 1024 * 1024

Q0 = 0
K0 = Q0 + Q_WIDTH
V0 = K0 + KV_WIDTH
QKV_WIDTH = V0 + KV_WIDTH
XR0 = QKV_WIDTH
XG0 = XR0 + LRU_WIDTH
GA0 = XG0 + LRU_WIDTH
GL0 = GA0 + D_MODEL
IN_WIDTH = GL0 + D_MODEL

R_LN1, R_CW0, R_CB, R_BA, R_BX, R_LAM, R_QG, R_KG, R_LN2, R_LN3 = 0, 1, 5, 6, 7, 8, 9, 10, 11, 12
VEC_ROWS = 16

PROMPT_TT = 64
FFN_TM = 1024
SAMPLE_BB = 16
PROJ_CHUNK = 512
N_FFN_WEIGHTS = 5


def _dot(a, b):
    return jnp.dot(a, b, preferred_element_type=F32)


def _dot_nt(a, b):
    return lax.dot_general(a, b, (((1,), (1,)), ((), ())), preferred_element_type=F32)


def _rms(x, g):
    ms = jnp.mean(x * x, axis=-1, keepdims=True)
    return x * lax.rsqrt(ms + EPS) * g


def _seg_rms(x, seg, g):
    x2 = x * x
    ssq = _dot(x2.astype(BF16), seg)
    return x * lax.rsqrt(ssq * (1.0 / HEAD_DIM) + EPS) * g


def _sigmoid(x):
    return 0.5 * jnp.tanh(0.5 * x) + 0.5


def _gelu_times(x, y):
    t = jnp.tanh(x * (GELU_C + (GELU_C * 0.044715) * (x * x)))
    hxy = (0.5 * x) * y
    return hxy + hxy * t


def _softplus(z):
    return jnp.maximum(z, 0.0) + jnp.log1p(jnp.exp(-jnp.abs(z)))


def _gather_bias(bucket, t5_ref, head):
    acc = jnp.full(bucket.shape, NEG_INF, F32)
    for bkt in range(N_BUCKETS):
        acc = jnp.where(bucket == bkt, t5_ref[bkt * N_HEADS + head], acc)
    return acc


def _lru_gates(xch, wax, b_a, b_x, c_row):
    gts = _dot(xch.astype(BF16), wax)
    r = _sigmoid(gts[:, :LRU_BLOCK] + b_a)
    ig = _sigmoid(gts[:, LRU_BLOCK:] + b_x)
    log_a = c_row * r
    a = jnp.exp(log_a)
    y = 1.0 - a * a
    return a, y * lax.rsqrt(jnp.maximum(y, TINY)) * (ig * xch)


def _rows_per_step(rows, nsteps):
    need = -(-rows // nsteps)
    for per in range(BF16_SUBLANES, rows + 1, BF16_SUBLANES):
        if rows % per == 0 and per >= need:
            return per
    raise ValueError((rows, nsteps))


def _convert_specs(stacked, layer, nsteps):
    _, rows, cols = stacked.shape
    per = _rows_per_step(rows, nsteps)
    last = rows // per - 1
    return (pl.BlockSpec((None, per, cols), lambda t: (layer, jnp.minimum(t, last), 0)),
            pl.BlockSpec((per, cols), lambda t: (jnp.minimum(t, last), 0)),
            jax.ShapeDtypeStruct((rows, cols), BF16))


def _convert_blocks(src_refs, dst_refs):
    for src, dst in zip(src_refs, dst_refs):
        dst[...] = src[...].astype(BF16)


def _mixer_prompt_kernel(t5_ref, sinks_ref, x_ref, vec_ref, bucket_ref, seg_ref, wqkv_ref, win_ref,
                         woa_ref, wax_ref, wol_ref, wout_ref, *rest, nb, tt):
    ffn_f32 = rest[0:N_FFN_WEIGHTS]
    x1_ref, nk_ref, nv_ref, nh_ref, nc_ref = rest[N_FFN_WEIGHTS:N_FFN_WEIGHTS + 5]
    ffn_bf16 = rest[N_FFN_WEIGHTS + 5:2 * N_FFN_WEIGHTS + 5]
    (bias_scr, bias_t, kband, vband, cs, a_scr, b_scr, hst, y_scr, o_scr,
     proj_scr, q_scr, m_scr) = rest[2 * N_FFN_WEIGHTS + 5:]
    tm = nb * tt
    t = pl.program_id(0)
    _convert_blocks(ffn_f32, ffn_bf16)

    @pl.when(t == 0)
    def _init():
        bucket = bucket_ref[...]
        for hh in range(N_HEADS):
            bias_scr[hh] = _gather_bias(bucket, t5_ref, hh) * LOG2E
        kband[...] = jnp.zeros((nb, 2 * WINDOW, KV_WIDTH), F32)
        vband[...] = jnp.zeros((nb, 2 * WINDOW, KV_WIDTH), F32)
        cs[:, 0:SUBLANES, :] = jnp.zeros((nb, SUBLANES, LRU_WIDTH), F32)
        hst[...] = jnp.zeros((nb, LRU_WIDTH), F32)

    x = x_ref[...].reshape(tm, D_MODEL)
    h = (x * vec_ref[R_LN1:R_LN1 + 1, :]).astype(BF16)
    rstd = lax.rsqrt(jnp.mean(x * x, axis=-1, keepdims=True) + EPS)

    cs[:, SUBLANES:SUBLANES + tt, :] = (_dot(h, win_ref[:, XR0:XG0]) * rstd).reshape(nb, tt, LRU_WIDTH)
    c_row = -LRU_C * _softplus(-vec_ref[R_LAM:R_LAM + 1, :])
    n_chunks = (IN_WIDTH - XG0) // PROJ_CHUNK
    qkv = None
    for hd in range(LRU_HEADS):
        lc = slice(hd * LRU_BLOCK, (hd + 1) * LRU_BLOCK)
        xc = vec_ref[R_CB:R_CB + 1, lc]
        for j in range(CONV_WIDTH):
            off = SUBLANES - (CONV_WIDTH - 1) + j
            xc = xc + vec_ref[R_CW0 + j:R_CW0 + j + 1, lc] * cs[:, off:off + tt, lc]
        a, bb = _lru_gates(xc.reshape(tm, LRU_BLOCK), wax_ref[hd], vec_ref[R_BA:R_BA + 1, lc],
                           vec_ref[R_BX:R_BX + 1, lc], c_row[:, lc])
        for b in range(nb):
            rows = pl.ds(b, tt, stride=nb)
            a_scr[hd, rows, :] = a[b * tt:(b + 1) * tt]
            b_scr[hd, rows, :] = bb[b * tt:(b + 1) * tt]
        if hd == 0:
            qkv = _dot(h, wqkv_ref[...]) * rstd
        elif hd == 1:
            seg = seg_ref[...]
            q_scr[...] = _seg_rms(qkv[:, Q0:K0], seg,
                                  vec_ref[R_QG:R_QG + 1, 0:Q_WIDTH] * (LOG2E * HEAD_DIM ** -0.5))
            kn = _seg_rms(qkv[:, K0:V0], seg[0:KV_WIDTH, 0:KV_WIDTH],
                          vec_ref[R_KG:R_KG + 1, 0:KV_WIDTH])
            kband[:, WINDOW:WINDOW + tt, :] = kn.reshape(nb, tt, KV_WIDTH)
            vband[:, WINDOW:WINDOW + tt, :] = qkv[:, V0:QKV_WIDTH].reshape(nb, tt, KV_WIDTH)
            nk_ref[...] = kband[:, tt:tt + WINDOW, :]
            nv_ref[...] = vband[:, tt:tt + WINDOW, :]
        if hd >= LRU_HEADS - n_chunks:
            c0 = (hd - (LRU_HEADS - n_chunks)) * PROJ_CHUNK
            proj_scr[:, c0:c0 + PROJ_CHUNK] = _dot(h, win_ref[:, XG0 + c0:XG0 + c0 + PROJ_CHUNK]) * rstd
    tail = cs[:, tt:tt + SUBLANES, :]
    nc_ref[...] = tail
    cs[:, 0:SUBLANES, :] = tail

    col = lax.broadcasted_iota(jnp.int32, (1, 2 * WINDOW), 1)
    colmask = jnp.where(col >= WINDOW - t * tt, 0.0, NEG_INF)
    for hh in range(N_HEADS):
        bias_t[hh] = bias_scr[hh, 0:tt, :] + colmask

    lane = lax.broadcasted_iota(jnp.int32, (1, LANES), 1)
    low = lane < HEAD_DIM
    for b in range(nb):
        kb = kband[b].astype(BF16)
        vb = vband[b].astype(BF16)
        o_kv = []
        q8 = jnp.concatenate(
            [jnp.where(low if kh == 0 else jnp.logical_not(low),
                       q_scr[b * tt:(b + 1) * tt, g * LANES:(g + 1) * LANES], 0.0)
             for kh in range(N_KV_HEADS) for g in range(GROUP)], axis=0).astype(BF16)
        s_all = _dot_nt(q8, kb)
        for kh in range(N_KV_HEADS):
            s = s_all[kh * GROUP * tt:(kh + 1) * GROUP * tt]
            ps, invs = [], []
            for g in range(GROUP):
                hh = kh * GROUP + g
                sg = s[g * tt:(g + 1) * tt] + bias_t[hh]
                sink = sinks_ref[hh] * LOG2E
                m = jnp.maximum(jnp.max(sg, axis=-1, keepdims=True), sink)
                e = jnp.exp2(sg - m)
                invs.append(1.0 / (jnp.sum(e, axis=-1, keepdims=True) + jnp.exp2(sink - m)))
                ps.append(e.astype(BF16))
            o4 = _dot(jnp.concatenate(ps, axis=0), vb)
            o_kv.append([o4[g * tt:(g + 1) * tt] * invs[g] for g in range(GROUP)])
        for g in range(GROUP):
            og = jnp.where(low, o_kv[0][g], o_kv[1][g])
            o_scr[b * tt:(b + 1) * tt, g * LANES:(g + 1) * LANES] = og.astype(BF16)

    kband[:, 0:WINDOW, :] = kband[:, tt:tt + WINDOW, :]
    vband[:, 0:WINDOW, :] = vband[:, tt:tt + WINDOW, :]


    hs = [hst[:, hd * LRU_BLOCK:(hd + 1) * LRU_BLOCK] for hd in range(LRU_HEADS)]
    for ts in range(tt):
        rows = slice(ts * nb, (ts + 1) * nb)
        for hd in range(LRU_HEADS):
            hs[hd] = a_scr[hd, rows, :] * hs[hd] + b_scr[hd, rows, :]
            b_scr[hd, rows, :] = hs[hd]
    hfin = jnp.concatenate(hs, axis=1)
    hst[...] = hfin
    nh_ref[...] = hfin

    for hd in range(LRU_HEADS):
        lc = slice(hd * LRU_BLOCK, (hd + 1) * LRU_BLOCK)
        hseq = jnp.concatenate([b_scr[hd, pl.ds(b, tt, stride=nb), :] for b in range(nb)], axis=0)
        y_scr[:, lc] = _gelu_times(proj_scr[:, lc], hseq).astype(BF16)

    half_w = D_MODEL // 2
    for c0 in (0, half_w):
        cc = slice(c0, c0 + half_w)
        m_c = (_sigmoid(proj_scr[:, GA0 - XG0 + c0:GA0 - XG0 + c0 + half_w]) * _dot(o_scr[...], woa_ref[:, cc])
               + _sigmoid(proj_scr[:, GL0 - XG0 + c0:GL0 - XG0 + c0 + half_w]) * _dot(y_scr[...], wol_ref[:, cc]))
        m_scr[:, cc] = m_c.astype(BF16)
    x1 = x + _dot(m_scr[...], wout_ref[...])
    x1_ref[...] = x1.reshape(nb, tt, D_MODEL)


def _const_spec(shape, index):
    return pl.BlockSpec(shape, index, pipeline_mode=pl.Buffered(1))


def _smem_spec():
    return pl.BlockSpec(memory_space=pltpu.SMEM)


def _mixer_prompt(layer, x, w_in, prm):
    nb, seq, _ = x.shape
    tt = PROMPT_TT
    assert nb == SUBLANES and seq % tt == 0 and WINDOW % tt == 0
    tm = nb * tt
    wl = lambda *z: lambda t: (layer,) + z
    kern = functools.partial(_mixer_prompt_kernel, nb=nb, tt=tt)
    conv = [_convert_specs(prm[name], layer, seq // tt) for name in FFN_WEIGHT_NAMES]
    return pl.pallas_call(
        kern,
        grid=(seq // tt,),
        in_specs=[
            _smem_spec(), _smem_spec(),
            pl.BlockSpec((nb, tt, D_MODEL), lambda t: (0, t, 0)),
            _const_spec((None, VEC_ROWS, D_MODEL), wl(0, 0)),
            _const_spec((WINDOW, 2 * WINDOW), lambda t: (0, 0)),
            _const_spec((Q_WIDTH, Q_WIDTH), lambda t: (0, 0)),
            _const_spec((None, D_MODEL, QKV_WIDTH), wl(0, 0)),
            _const_spec((D_MODEL, IN_WIDTH), lambda t: (0, 0)),
            _const_spec((None, Q_WIDTH, D_MODEL), wl(0, 0)),
            _const_spec((None, LRU_HEADS, LRU_BLOCK, 2 * LRU_BLOCK), wl(0, 0, 0)),
            _const_spec((None, LRU_WIDTH, D_MODEL), wl(0, 0)),
            _const_spec((None, D_MODEL, D_MODEL), wl(0, 0)),
        ] + [c[0] for c in conv],
        out_specs=[
            pl.BlockSpec((nb, tt, D_MODEL), lambda t: (0, t, 0)),
            pl.BlockSpec((nb, WINDOW, KV_WIDTH), lambda t: (0, 0, 0)),
            pl.BlockSpec((nb, WINDOW, KV_WIDTH), lambda t: (0, 0, 0)),
            pl.BlockSpec((nb, LRU_WIDTH), lambda t: (0, 0)),
            pl.BlockSpec((nb, SUBLANES, LRU_WIDTH), lambda t: (0, 0, 0)),
        ] + [c[1] for c in conv],
        out_shape=[
            jax.ShapeDtypeStruct((nb, seq, D_MODEL), F32),
            jax.ShapeDtypeStruct((nb, WINDOW, KV_WIDTH), F32),
            jax.ShapeDtypeStruct((nb, WINDOW, KV_WIDTH), F32),
            jax.ShapeDtypeStruct((nb, LRU_WIDTH), F32),
            jax.ShapeDtypeStruct((nb, SUBLANES, LRU_WIDTH), F32),
        ] + [c[2] for c in conv],
        scratch_shapes=[
            pltpu.VMEM((N_HEADS, WINDOW, 2 * WINDOW), F32),
            pltpu.VMEM((N_HEADS, tt, 2 * WINDOW), F32),
            pltpu.VMEM((nb, 2 * WINDOW, KV_WIDTH), F32),
            pltpu.VMEM((nb, 2 * WINDOW, KV_WIDTH), F32),
            pltpu.VMEM((nb, tt + SUBLANES, LRU_WIDTH), F32),
            pltpu.VMEM((LRU_HEADS, tm, LRU_BLOCK), F32),
            pltpu.VMEM((LRU_HEADS, tm, LRU_BLOCK), F32),
            pltpu.VMEM((nb, LRU_WIDTH), F32),
            pltpu.VMEM((tm, LRU_WIDTH), BF16),
            pltpu.VMEM((tm, Q_WIDTH), BF16),
            pltpu.VMEM((tm, IN_WIDTH - XG0), F32),
            pltpu.VMEM((tm, Q_WIDTH), F32),
            pltpu.VMEM((tm, D_MODEL), BF16),
        ],
        compiler_params=pltpu.CompilerParams(
            dimension_semantics=("arbitrary",), vmem_limit_bytes=VMEM_LIMIT_BYTES),
        name=f"mixer_prompt_l{layer}",
    )(prm["t5"], prm["sinks"][layer], x, prm["vecs"], prm["bucket_p"], prm["seg"], prm["w_qkv"],
      w_in, prm["w_o_attn"], prm["w_ax"], prm["w_o_lru"], prm["w_out"],
      *[prm[name] for name in FFN_WEIGHT_NAMES])


def _mixer_sample_kernel(t5_ref, sinks_ref, x_ref, vec_ref, bucket_ref, seg_ref, wqkv_ref, win_ref,
                         woa_ref, wax_ref, wol_ref, wout_ref, ck_ref, cv_ref, h0_ref, sc_ref,
                         nk_all_ref, nv_all_ref,
                         x1_ref, nh_ref, xr_ref, nk_ref, nv_ref,
                         h_scr, q_scr, o_scr, mb_scr, kn_scr, vn_scr,
                         *, bb):
    del nk_all_ref, nv_all_ref
    i = pl.program_id(0)
    nkeys = bb * WINDOW

    @pl.when(i == 0)
    def _project():
        x = x_ref[...]
        h = _rms(x, vec_ref[R_LN1:R_LN1 + 1, :]).astype(BF16)
        h_scr[...] = h
        qkv = _dot(h, wqkv_ref[...])
        seg = seg_ref[...]
        q_scr[...] = _seg_rms(qkv[:, Q0:K0], seg,
                              vec_ref[R_QG:R_QG + 1, 0:Q_WIDTH] * (HEAD_DIM ** -0.5))
        kn_scr[...] = _seg_rms(qkv[:, K0:V0], seg[0:KV_WIDTH, 0:KV_WIDTH],
                               vec_ref[R_KG:R_KG + 1, 0:KV_WIDTH])
        vn_scr[...] = qkv[:, V0:QKV_WIDTH]
        bucket = bucket_ref[...]
        rowb = lax.broadcasted_iota(jnp.int32, (bb, nkeys), 0)
        colb = lax.broadcasted_iota(jnp.int32, (bb, nkeys), 1) // WINDOW
        for hh in range(N_HEADS):
            brow = _gather_bias(bucket, t5_ref, hh)
            brow = jnp.concatenate([brow] * bb, axis=1)
            mb_scr[hh * bb:(hh + 1) * bb, :] = jnp.where(rowb == colb, brow, NEG_INF)

    rows = pl.ds(pl.multiple_of(i * bb, bb), bb)
    lane = lax.broadcasted_iota(jnp.int32, (1, LANES), 1)
    low = lane < HEAD_DIM
    qblk = q_scr[rows, :]
    qz = []
    for kh in range(N_KV_HEADS):
        keep = low if kh == 0 else jnp.logical_not(low)
        for g in range(GROUP):
            qz.append(jnp.where(keep, qblk[:, g * LANES:(g + 1) * LANES], 0.0))
    qz = jnp.concatenate(qz, axis=0).astype(BF16)
    kt = jnp.concatenate([ck_ref[b] for b in range(bb)], axis=1).astype(BF16)
    vt = jnp.concatenate([cv_ref[b] for b in range(bb)], axis=1).astype(BF16)
    s = _dot(qz, kt) + mb_scr[...]
    kn_blk = kn_scr[rows, :]
    vn_blk = vn_scr[rows, :]
    knew = jnp.concatenate([kn_blk.astype(BF16).astype(F32)] * N_HEADS, axis=0)
    vnew = jnp.concatenate([vn_blk.astype(BF16).astype(F32)] * N_HEADS, axis=0)
    self_bias = jnp.concatenate(
        [jnp.full((bb, 1), t5_ref[hh], F32) for hh in range(N_HEADS)], axis=0)
    sink = jnp.concatenate(
        [jnp.full((bb, 1), sinks_ref[hh], F32) for hh in range(N_HEADS)], axis=0)
    s_self = jnp.sum(qz.astype(F32) * knew, axis=-1, keepdims=True) + self_bias
    m = jnp.maximum(jnp.maximum(jnp.max(s, axis=-1, keepdims=True), s_self), sink)
    e = jnp.exp(s - m)
    e_self = jnp.exp(s_self - m)
    den = jnp.sum(e, axis=-1, keepdims=True) + e_self + jnp.exp(sink - m)
    inv = 1.0 / den
    o = (_dot_nt((e * inv).astype(BF16), vt)
         + (e_self * inv).astype(BF16).astype(F32) * vnew)
    half = GROUP * bb
    for g in range(GROUP):
        og = jnp.where(low, o[g * bb:(g + 1) * bb], o[half + g * bb:half + (g + 1) * bb])
        o_scr[rows, g * LANES:(g + 1) * LANES] = og

    knt = kn_blk.T
    vnt = vn_blk.T
    last = lax.broadcasted_iota(jnp.int32, (1, WINDOW), 1) == WINDOW - 1
    for b in range(bb):
        nk_ref[b] = jnp.where(last, knt[:, b:b + 1], pltpu.roll(ck_ref[b], WINDOW - 1, 1))
        nv_ref[b] = jnp.where(last, vnt[:, b:b + 1], pltpu.roll(cv_ref[b], WINDOW - 1, 1))

    @pl.when(i == pl.num_programs(0) - 1)
    def _finish():
        x = x_ref[...]
        h = h_scr[...]
        m_att = _sigmoid(_dot(h, win_ref[:, GA0:GL0])) * _dot(o_scr[...].astype(BF16), woa_ref[...])
        xr = _dot(h, win_ref[:, XR0:XG0])
        xr_ref[...] = xr
        xc = vec_ref[R_CB:R_CB + 1, :] + vec_ref[R_CW0 + CONV_WIDTH - 1:R_CW0 + CONV_WIDTH, :] * xr
        for j in range(CONV_WIDTH - 1):
            xc = xc + vec_ref[R_CW0 + j:R_CW0 + j + 1, :] * sc_ref[j]
        c_row = -LRU_C * _softplus(-vec_ref[R_LAM:R_LAM + 1, :])
        hn = []
        for hd in range(LRU_HEADS):
            lc = slice(hd * LRU_BLOCK, (hd + 1) * LRU_BLOCK)
            a, bb_ = _lru_gates(xc[:, lc], wax_ref[hd], vec_ref[R_BA:R_BA + 1, lc],
                                vec_ref[R_BX:R_BX + 1, lc], c_row[:, lc])
            hn.append(a * h0_ref[:, lc] + bb_)
        hn = jnp.concatenate(hn, axis=1)
        nh_ref[...] = hn
        y = _gelu_times(_dot(h, win_ref[:, XG0:GA0]), hn).astype(BF16)
        m_all = m_att + _sigmoid(_dot(h, win_ref[:, GL0:IN_WIDTH])) * _dot(y, wol_ref[...])
        x1_ref[...] = x + _dot(m_all.astype(BF16), wout_ref[...])


def _mixer_sample(layer, x, w_in, ckt, cvt, h0, sc, prm, nk_buf, nv_buf):
    depth, nseq = ckt.shape[0], ckt.shape[1]
    bb = SAMPLE_BB
    assert nseq % bb == 0
    wl = lambda *z: lambda i: (layer,) + z
    full2 = lambda i: (0, 0)
    cache_spec = pl.BlockSpec((None, bb, KV_WIDTH, WINDOW), lambda i: (layer, i, 0, 0))
    n_in = 16
    kern = functools.partial(_mixer_sample_kernel, bb=bb)
    return pl.pallas_call(
        kern,
        grid=(nseq // bb,),
        in_specs=[
            _smem_spec(), _smem_spec(),
            _const_spec((nseq, D_MODEL), full2),
            _const_spec((None, VEC_ROWS, D_MODEL), wl(0, 0)),
            _const_spec((1, WINDOW), full2),
            _const_spec((Q_WIDTH, Q_WIDTH), full2),
            _const_spec((None, D_MODEL, QKV_WIDTH), wl(0, 0)),
            _const_spec((D_MODEL, IN_WIDTH), full2),
            _const_spec((None, Q_WIDTH, D_MODEL), wl(0, 0)),
            _const_spec((None, LRU_HEADS, LRU_BLOCK, 2 * LRU_BLOCK), wl(0, 0, 0)),
            _const_spec((None, LRU_WIDTH, D_MODEL), wl(0, 0)),
            _const_spec((None, D_MODEL, D_MODEL), wl(0, 0)),
            cache_spec, cache_spec,
            _const_spec((None, nseq, LRU_WIDTH), wl(0, 0)),
            _const_spec((None, CONV_WIDTH - 1, nseq, LRU_WIDTH), wl(0, 0, 0)),
            pl.BlockSpec(memory_space=pl.ANY), pl.BlockSpec(memory_space=pl.ANY),
        ],
        out_specs=[
            pl.BlockSpec((nseq, D_MODEL), full2),
            pl.BlockSpec((nseq, LRU_WIDTH), full2),
            pl.BlockSpec((nseq, LRU_WIDTH), full2),
            cache_spec, cache_spec,
        ],
        out_shape=[
            jax.ShapeDtypeStruct((nseq, D_MODEL), F32),
            jax.ShapeDtypeStruct((nseq, LRU_WIDTH), F32),
            jax.ShapeDtypeStruct((nseq, LRU_WIDTH), F32),
            jax.ShapeDtypeStruct((depth, nseq, KV_WIDTH, WINDOW), F32),
            jax.ShapeDtypeStruct((depth, nseq, KV_WIDTH, WINDOW), F32),
        ],
        input_output_aliases={n_in: 3, n_in + 1: 4},
        scratch_shapes=[
            pltpu.VMEM((nseq, D_MODEL), BF16),
            pltpu.VMEM((nseq, Q_WIDTH), F32),
            pltpu.VMEM((nseq, Q_WIDTH), F32),
            pltpu.VMEM((N_HEADS * bb, bb * WINDOW), F32),
            pltpu.VMEM((nseq, KV_WIDTH), F32),
            pltpu.VMEM((nseq, KV_WIDTH), F32),
        ],
        compiler_params=pltpu.CompilerParams(
            dimension_semantics=("arbitrary",), vmem_limit_bytes=VMEM_LIMIT_BYTES),
        name=f"mixer_sample_l{layer}",
    )(prm["t5"], prm["sinks"][layer], x, prm["vecs"], prm["bucket_s"], prm["seg"], prm["w_qkv"],
      w_in, prm["w_o_attn"], prm["w_ax"], prm["w_o_lru"], prm["w_out"], ckt, cvt, h0, sc,
      nk_buf, nv_buf)


FF_CHUNKS = ((0, 1024), (1024, 2048), (2048, D_FF))
FFN_WEIGHT_NAMES = ("w_gate", "w_up", "w_down", "w_ple_gate", "w_ple")


def _ffn_rows(x, p, vec_ref, wg_ref, wu_ref, wd_ref, wpg_ref, wp_ref, act_scr):
    h2 = (x * vec_ref[R_LN2:R_LN2 + 1, :]).astype(BF16)
    r2 = lax.rsqrt(jnp.mean(x * x, axis=-1, keepdims=True) + EPS)
    for lo, hi in FF_CHUNKS:
        g = _dot(h2, wg_ref[:, lo:hi]) * r2
        act = g * _sigmoid(g) * (_dot(h2, wu_ref[:, lo:hi]) * r2)
        act_scr[:, lo:hi] = act.astype(BF16)
    x = x + _dot(act_scr[...], wd_ref[...])
    h3 = (x * vec_ref[R_LN3:R_LN3 + 1, :]).astype(BF16)
    r3 = lax.rsqrt(jnp.mean(x * x, axis=-1, keepdims=True) + EPS)
    gate = _sigmoid(_dot(h3, wpg_ref[...]) * r3)
    return x + gate * _dot(p.astype(BF16), wp_ref[...])


def _ffn_kernel(x_ref, p_ref, xs_ref, ps_ref, vec_ref, wg_ref, wu_ref, wd_ref, wpg_ref, wp_ref,
                *rest, convert):
    if convert:
        win_f32, o_ref, os_ref, win_bf16, act_scr = rest
        _convert_blocks([win_f32], [win_bf16])
    else:
        o_ref, os_ref, act_scr = rest
    weights = (vec_ref, wg_ref, wu_ref, wd_ref, wpg_ref, wp_ref)
    o_ref[...] = _ffn_rows(x_ref[...], p_ref[...], *weights, act_scr)

    @pl.when(pl.program_id(0) == pl.num_programs(0) - 1)
    def _sample():
        ns = xs_ref.shape[0]
        os_ref[...] = _ffn_rows(xs_ref[...], ps_ref[...], *weights, act_scr.at[0:ns, :])


def _ffn(layer, x, p, xs, ps, ffn_w, prm):
    rows, ns = x.shape[0], xs.shape[0]
    tm = min(FFN_TM, rows)
    assert rows % tm == 0 and ns <= tm
    depth = prm["w_in_f32"].shape[0]
    convert = layer + 1 < depth
    conv = [_convert_specs(prm["w_in_f32"], layer + 1, rows // tm)] if convert else []
    wl = lambda *z: lambda r: (layer,) + z
    whole = lambda r: (0, 0)
    outs = pl.pallas_call(
        functools.partial(_ffn_kernel, convert=convert),
        grid=(rows // tm,),
        in_specs=[
            pl.BlockSpec((tm, D_MODEL), lambda r: (r, 0)),
            pl.BlockSpec((None, tm, PLE_DIM), lambda r: (layer, r, 0)),
            _const_spec((ns, D_MODEL), whole),
            _const_spec((None, ns, PLE_DIM), wl(0, 0)),
            _const_spec((None, VEC_ROWS, D_MODEL), wl(0, 0)),
            _const_spec((D_MODEL, D_FF), whole),
            _const_spec((D_MODEL, D_FF), whole),
            _const_spec((D_FF, D_MODEL), whole),
            _const_spec((D_MODEL, D_MODEL), whole),
            _const_spec((PLE_DIM, D_MODEL), whole),
        ] + [c[0] for c in conv],
        out_specs=[
            pl.BlockSpec((tm, D_MODEL), lambda r: (r, 0)),
            pl.BlockSpec((ns, D_MODEL), whole),
        ] + [c[1] for c in conv],
        out_shape=[
            jax.ShapeDtypeStruct((rows, D_MODEL), F32),
            jax.ShapeDtypeStruct((ns, D_MODEL), F32),
        ] + [c[2] for c in conv],
        scratch_shapes=[pltpu.VMEM((tm, D_FF), BF16)],
        compiler_params=pltpu.CompilerParams(
            dimension_semantics=("arbitrary",), vmem_limit_bytes=FFN_VMEM_LIMIT_BYTES),
        name=f"ffn_l{layer}",
    )(x, p, xs, ps, prm["vecs"], *ffn_w, *([prm["w_in_f32"]] if convert else []))
    return outs[0], outs[1], (outs[2] if convert else None)


def _t5_bucket(dist):
    n = np.maximum(dist, 0)
    max_exact = N_BUCKETS // 2
    nf = np.maximum(n, 1).astype(np.float32)
    large = max_exact + (np.log(nf / max_exact) / math.log(MAX_DISTANCE / max_exact)
                         * (N_BUCKETS - max_exact)).astype(np.int32)
    large = np.minimum(large, N_BUCKETS - 1)
    return np.where(n < max_exact, n, large)


def _bucket_table(dist):
    return np.where((dist >= 0) & (dist < WINDOW), _t5_bucket(dist), -1).astype(np.int32)


def _regroup_heads(w, axis):
    shape = w.shape
    w = w.reshape(shape[:axis] + (N_KV_HEADS, GROUP, HEAD_DIM) + shape[axis + 1:])
    return jnp.swapaxes(w, axis, axis + 1).reshape(shape)


def _prepare(t5_table, ln1, w_in, q_gain, k_gain, sinks, w_o_attn, conv_w, conv_b, w_a, b_a,
             w_x, b_x, lam, w_o_lru, w_out, ln2, w_gate, w_up, w_down, ln3, w_ple, w_ple_gate):
    depth = w_in.shape[0]
    w_att = lax.optimization_barrier(w_in[:, :, 0:QKV_WIDTH])
    w_qkv = jnp.concatenate(
        [_regroup_heads(w_att[:, :, 0:Q_WIDTH], 2), w_att[:, :, Q_WIDTH:QKV_WIDTH]], axis=2).astype(BF16)

    def row(v):
        return jnp.pad(v, ((0, 0), (0, D_MODEL - v.shape[1])))[:, None, :]

    parts = [(R_LN1, row(ln1)), (R_CW0, conv_w), (R_CB, row(conv_b)), (R_BA, row(b_a)),
             (R_BX, row(b_x)), (R_LAM, row(lam)), (R_QG, row(jnp.tile(q_gain, (1, N_HEADS)))),
             (R_KG, row(jnp.tile(k_gain, (1, N_KV_HEADS)))), (R_LN2, row(ln2)), (R_LN3, row(ln3))]
    used = 0
    for first_row, part in parts:
        assert first_row == used
        used += part.shape[1]
    vecs = jnp.concatenate([part for _, part in parts]
                           + [jnp.zeros((depth, VEC_ROWS - used, D_MODEL), F32)], axis=1)
    head_id = np.arange(Q_WIDTH) // HEAD_DIM
    seg = jnp.asarray(head_id[:, None] == head_id[None, :], BF16)
    bucket_p = _bucket_table((WINDOW + np.arange(WINDOW))[:, None] - np.arange(2 * WINDOW)[None, :])
    bucket_s = _bucket_table((WINDOW - np.arange(WINDOW))[None, :])
    return {
        "t5": t5_table.reshape(-1),
        "sinks": sinks,
        "vecs": vecs,
        "seg": seg,
        "bucket_p": jnp.asarray(bucket_p),
        "bucket_s": jnp.asarray(bucket_s),
        "w_qkv": w_qkv,
        "w_in_f32": w_in,
        "w_in_first": lax.optimization_barrier(w_in[0]).astype(BF16),
        "w_o_attn": _regroup_heads(w_o_attn, 1).astype(BF16),
        "w_ax": jnp.concatenate([w_a, w_x], axis=-1).astype(BF16),
        "w_o_lru": w_o_lru.astype(BF16),
        "w_out": w_out.astype(BF16),
        "w_gate": w_gate,
        "w_up": w_up,
        "w_down": w_down,
        "w_ple_gate": w_ple_gate,
        "w_ple": w_ple,
    }


def kernel(x_prompt, x_sample, cache_k_win, cache_v_win, state_lru_h, state_conv, p_prompt,
           p_sample, t5_table, ln1, w_in, q_gain, k_gain, sinks, w_o_attn, conv_w, conv_b, w_a,
           b_a, w_x, b_x, lam, w_o_lru, w_out, ln2, w_gate, w_up, w_down, ln3, w_ple, w_ple_gate):
    depth = w_in.shape[0]
    nb, seq, _ = x_prompt.shape
    nseq = x_sample.shape[0]
    assert x_sample.shape[1] == 1 and cache_k_win.shape[2] == WINDOW
    prm = _prepare(t5_table, ln1, w_in, q_gain, k_gain, sinks, w_o_attn, conv_w, conv_b, w_a, b_a,
                   w_x, b_x, lam, w_o_lru, w_out, ln2, w_gate, w_up, w_down, ln3, w_ple, w_ple_gate)
    ckt = jnp.transpose(cache_k_win, (0, 1, 3, 4, 2)).reshape(depth, nseq, KV_WIDTH, WINDOW)
    cvt = jnp.transpose(cache_v_win, (0, 1, 3, 4, 2)).reshape(depth, nseq, KV_WIDTH, WINDOW)
    sc = jnp.transpose(state_conv, (0, 2, 1, 3))
    pp = p_prompt.reshape(depth, nb * seq, PLE_DIM)
    ps = p_sample.reshape(depth, nseq, PLE_DIM)

    yp = x_prompt
    ys = x_sample.reshape(nseq, D_MODEL)
    outs = [[] for _ in range(6)]
    nk_buf = jnp.zeros((depth, nseq, KV_WIDTH, WINDOW), F32)
    nv_buf = jnp.zeros((depth, nseq, KV_WIDTH, WINDOW), F32)
    w_in_l = prm["w_in_first"]
    for layer in range(depth):
        x1, nk, nv, nh, nc, *ffn_w = _mixer_prompt(layer, yp, w_in_l, prm)
        outs[0].append(nk.reshape(nb, WINDOW, N_KV_HEADS, HEAD_DIM))
        outs[1].append(nv.reshape(nb, WINDOW, N_KV_HEADS, HEAD_DIM))
        outs[2].append(nh)
        outs[3].append(nc[:, SUBLANES - (CONV_WIDTH - 1):, :])

        x1s, nhs, xr, nk_buf, nv_buf = _mixer_sample(layer, ys, w_in_l, ckt, cvt, state_lru_h, sc,
                                                     prm, nk_buf, nv_buf)
        yp, ys, w_in_l = _ffn(layer, x1.reshape(nb * seq, D_MODEL), pp, x1s, ps, ffn_w, prm)
        yp = yp.reshape(nb, seq, D_MODEL)
        outs[4].append(nhs)
        outs[5].append(jnp.concatenate([state_conv[layer, :, 1:], xr[:, None, :]], axis=1))

    def untranspose(buf):
        return jnp.transpose(buf.reshape(depth, nseq, N_KV_HEADS, HEAD_DIM, WINDOW), (0, 1, 4, 2, 3))

    stacked = [jnp.stack(o) for o in outs]
    return (yp, ys.reshape(nseq, 1, D_MODEL), stacked[0], stacked[1], stacked[2], stacked[3],
            untranspose(nk_buf), untranspose(nv_buf), stacked[4], stacked[5])
```
